```python
import math
import jax
import jax.numpy as jnp
from jax import lax
import numpy as np

D_MODEL = 2048
BATCH = 4
SEQ = 2048
DEPTH = 2
DEC_BATCH = 128
DEC_SEQ = 1
PAST_LEN = 8192
PAGE_SIZE = 128

MEM_LEN = 256
CHUNK = 128
G_GROUPS = 8
G_WIDTH = 768
G_GDIM = G_WIDTH // G_GROUPS
SWA_HEADS = 12
SWA_KV = 4
SWA_HD = 64
SWA_WIDTH = SWA_HEADS * SWA_HD
SWA_KV_WIDTH = SWA_KV * SWA_HD
WINDOW = 128
QBLOCK = 128
MEM_HEADS = 4
MEM_HD = 128
MEM_WIDTH = MEM_HEADS * MEM_HD
N_BRANCH = 3
D_FF = 5632
CONV_W = 3
EPS = 1e-6
N_IN = 2 * G_WIDTH + SWA_WIDTH + 2 * SWA_KV_WIDTH + MEM_WIDTH + N_BRANCH * D_MODEL

kernel_name = 'hybrid_gmlp_swa_mem_convffn_step'


def rmsnorm(x, g):
    xf = x.astype(jnp.float32)
    y = xf * lax.rsqrt(jnp.mean(xf * xf, axis=-1, keepdims=True) + EPS)
    return (y * g.astype(jnp.float32)).astype(x.dtype)


def alibi_slopes(n):
    p = 2 ** int(math.floor(math.log2(n)))
    base = [2.0 ** (-8.0 * (i + 1) / p) for i in range(p)]
    extra = [2.0 ** (-8.0 * (2 * i + 1) / (2 * p)) for i in range(n - p)]
    return jnp.asarray(base + extra, dtype=jnp.float32)


def gmlp_spatial(v, ws, bs):
    B, T, _ = v.shape
    L = min(CHUNK, T)
    nc = -(-T // L)
    tp = nc * L
    if tp > T:
        v = jnp.pad(v, ((0, 0), (0, tp - T), (0, 0)))
    vr = v.reshape(B, nc, L, G_GROUPS, G_GDIM)
    w = jnp.tril(ws[:, :L, :L])
    m = jnp.einsum('gts,bnsgc->bntgc', w, vr) + jnp.transpose(bs[:, :L])[None, None, :, :, None]
    return m.reshape(B, tp, G_WIDTH)[:, :T]


def swa_attention(q, k, v, prefix_k, prefix_v, start, sinks, slopes):
    B, T = q.shape[0], q.shape[1]
    G = SWA_HEADS // SWA_KV
    P = prefix_k.shape[1]
    if P < WINDOW:
        padw = ((0, 0), (WINDOW - P, 0), (0, 0), (0, 0))
        prefix_k = jnp.pad(prefix_k, padw)
        prefix_v = jnp.pad(prefix_v, padw)
    kcat = jnp.concatenate([prefix_k.astype(k.dtype), k], axis=1)
    vcat = jnp.concatenate([prefix_v.astype(v.dtype), v], axis=1)
    new_k_buf = kcat[:, T:]
    new_v_buf = vcat[:, T:]
    qb = min(QBLOCK, T)
    nb = -(-T // qb)
    tp = nb * qb
    if tp > T:
        q = jnp.pad(q, ((0, 0), (0, tp - T), (0, 0), (0, 0)))
        kcat = jnp.pad(kcat, ((0, 0), (0, tp - T), (0, 0), (0, 0)))
        vcat = jnp.pad(vcat, ((0, 0), (0, tp - T), (0, 0), (0, 0)))
    kw = WINDOW + qb
    idx = jnp.arange(nb)[:, None] * qb + jnp.arange(kw)[None, :]
    kb = kcat[:, idx]
    vb = vcat[:, idx]
    qr = q.reshape(B, nb, qb, SWA_KV, G, SWA_HD)
    s = jnp.einsum('bnqkgd,bnskd->bnkgqs', qr, kb).astype(jnp.float32) * (SWA_HD ** -0.5)
    qpos = start + jnp.arange(tp).reshape(nb, qb)
    kpos = start - WINDOW + idx
    dist = qpos[:, :, None] - kpos[:, None, :]
    valid = (dist >= 0) & (dist <= WINDOW) & (kpos[:, None, :] >= 0)
    bias = -slopes.reshape(SWA_KV, G)[None, :, :, None, None] * dist[:, None, None].astype(jnp.float32)
    s = jnp.where(valid[:, None, None], s + bias, -jnp.inf)
    sink = jnp.broadcast_to(sinks.astype(jnp.float32).reshape(1, 1, SWA_KV, G, 1, 1), s.shape[:-1] + (1,))
    p = jax.nn.softmax(jnp.concatenate([s, sink], axis=-1), axis=-1)[..., :-1]
    o = jnp.einsum('bnkgqs,bnskd->bnqkgd', p.astype(vb.dtype), vb)
    return o.reshape(B, tp, SWA_WIDTH)[:, :T], new_k_buf, new_v_buf


def memory_attention(q, mk, mv):
    B, T = q.shape[0], q.shape[1]
    s = jnp.einsum('bthd,bmhd->bhtm', q, mk.astype(q.dtype)).astype(jnp.float32) * (MEM_HD ** -0.5)
    p = jax.nn.softmax(s, axis=-1).astype(q.dtype)
    return jnp.einsum('bhtm,bmhd->bthd', p, mv.astype(q.dtype)).reshape(B, T, MEM_WIDTH)


def layer_forward(x, start, swa_pk, swa_pv, mem_k, mem_v, conv_prefix, slopes,
                  norm_mix_g, w_in, gmlp_norm_g, gmlp_ws, gmlp_bs, attn_sinks,
                  w_br_g, w_br_a, w_br_m, w_out, norm_ffn_g, w_up, conv_w, conv_b, w_down):
    B, T, _ = x.shape
    xn = rmsnorm(x, norm_mix_g)
    z = xn @ w_in
    c1 = G_WIDTH
    c2 = c1 + G_WIDTH
    c3 = c2 + SWA_WIDTH
    c4 = c3 + SWA_KV_WIDTH
    c5 = c4 + SWA_KV_WIDTH
    c6 = c5 + MEM_WIDTH
    zu, zv, zq, zk, zvv, zm, zg = jnp.split(z, [c1, c2, c3, c4, c5, c6], axis=-1)
    u = jax.nn.gelu(zu)
    vg = rmsnorm(jax.nn.gelu(zv), gmlp_norm_g)
    o_g = u * gmlp_spatial(vg, gmlp_ws, gmlp_bs)
    c0 = ((start + T - 1) // CHUNK) * CHUNK - start
    v_rows = vg[:, c0:]
    q = zq.reshape(B, T, SWA_HEADS, SWA_HD)
    k = zk.reshape(B, T, SWA_KV, SWA_HD)
    v = zvv.reshape(B, T, SWA_KV, SWA_HD)
    o_a, kbuf, vbuf = swa_attention(q, k, v, swa_pk, swa_pv, start, attn_sinks, slopes)
    o_m = memory_attention(zm.reshape(B, T, MEM_HEADS, MEM_HD), mem_k, mem_v)
    gates = jax.nn.sigmoid(zg.astype(jnp.float32)).astype(x.dtype).reshape(B, T, N_BRANCH, D_MODEL)
    merged = (gates[:, :, 0] * (o_g @ w_br_g) + gates[:, :, 1] * (o_a @ w_br_a)
              + gates[:, :, 2] * (o_m @ w_br_m))
    x = x + merged @ w_out
    h = rmsnorm(x, norm_ffn_g) @ w_up
    hc = jnp.concatenate([conv_prefix.astype(h.dtype), h], axis=1)
    hconv = conv_b
    for j in range(CONV_W):
        hconv = hconv + hc[:, j:j + T] * conv_w[j]
    a, b = jnp.split(hconv, 2, axis=-1)
    x = x + (jax.nn.gelu(a) * b) @ w_down
    conv_buf = hc[:, -(CONV_W - 1):]
    return x, kbuf, vbuf, v_rows, conv_buf


def setup_inputs(seed: int = 0) -> dict:
    key = jax.random.key(seed)
    ks = jax.random.split(key, 32)

    def nrm(k, shape, scale):
        return jax.random.normal(k, shape, jnp.float32) * scale

    win_buf = min(WINDOW, PAST_LEN)
    return {
        'x_prompt': nrm(ks[0], (BATCH, SEQ, D_MODEL), 1.0),
        'x_sample': nrm(ks[1], (DEC_BATCH, DEC_SEQ, D_MODEL), 1.0),
        'cache_swa_k': nrm(ks[2], (DEPTH, DEC_BATCH, win_buf, SWA_KV, SWA_HD), 1.0),
        'cache_swa_v': nrm(ks[3], (DEPTH, DEC_BATCH, win_buf, SWA_KV, SWA_HD), 1.0),
        'cache_mem_k': nrm(ks[4], (DEPTH, DEC_BATCH, MEM_LEN, MEM_HEADS, MEM_HD), 1.0),
        'cache_mem_v': nrm(ks[5], (DEPTH, DEC_BATCH, MEM_LEN, MEM_HEADS, MEM_HD), 1.0),
        'state_conv': nrm(ks[6], (DEPTH, DEC_BATCH, CONV_W - 1, 2 * D_FF), 1.0),
        'mem_prompt': nrm(ks[7], (BATCH, MEM_LEN, D_MODEL), 1.0),
        'norm_mix_g': 1.0 + nrm(ks[8], (DEPTH, D_MODEL), 0.02),
        'w_in': nrm(ks[9], (DEPTH, D_MODEL, N_IN), D_MODEL ** -0.5),
        'gmlp_norm_g': 1.0 + nrm(ks[10], (DEPTH, G_WIDTH), 0.02),
        'gmlp_ws': nrm(ks[11], (DEPTH, G_GROUPS, CHUNK, CHUNK), 0.5 * CHUNK ** -0.5),
        'gmlp_bs': 1.0 + nrm(ks[12], (DEPTH, G_GROUPS, CHUNK), 0.02),
        'attn_sinks': nrm(ks[13], (DEPTH, SWA_HEADS), 0.5),
        'mem_norm_g': 1.0 + nrm(ks[14], (DEPTH, D_MODEL), 0.02),
        'w_mem_kv': nrm(ks[15], (DEPTH, D_MODEL, 2 * MEM_WIDTH), D_MODEL ** -0.5),
        'w_br_g': nrm(ks[16], (DEPTH, G_WIDTH, D_MODEL), G_WIDTH ** -0.5),
        'w_br_a': nrm(ks[17], (DEPTH, SWA_WIDTH, D_MODEL), SWA_WIDTH ** -0.5),
        'w_br_m': nrm(ks[18], (DEPTH, MEM_WIDTH, D_MODEL), MEM_WIDTH ** -0.5),
        'w_out': nrm(ks[19], (DEPTH, D_MODEL, D_MODEL), D_MODEL ** -0.5),
        'norm_ffn_g': 1.0 + nrm(ks[20], (DEPTH, D_MODEL), 0.02),
        'w_up': nrm(ks[21], (DEPTH, D_MODEL, 2 * D_FF), D_MODEL ** -0.5),
        'conv_w': nrm(ks[22], (DEPTH, CONV_W, 2 * D_FF), CONV_W ** -0.5),
        'conv_b': nrm(ks[23], (DEPTH, 2 * D_FF), 0.02),
        'w_down': nrm(ks[24], (DEPTH, D_FF, D_MODEL), D_FF ** -0.5),
        'final_norm_g': 1.0 + nrm(ks[25], (D_MODEL,), 0.02),
    }


def reference(x_prompt, x_sample, cache_swa_k, cache_swa_v, cache_mem_k, cache_mem_v, state_conv,
              mem_prompt, norm_mix_g, w_in, gmlp_norm_g, gmlp_ws, gmlp_bs, attn_sinks, mem_norm_g,
              w_mem_kv, w_br_g, w_br_a, w_br_m, w_out, norm_ffn_g, w_up, conv_w, conv_b, w_down,
              final_norm_g):
    slopes = alibi_slopes(SWA_HEADS)
    xp = x_prompt
    xs = x_sample
    Bp = xp.shape[0]
    pk, pv, sk, sv, mkp, mvp, gvp, gvs, cvp, cvs = ([] for _ in range(10))
    zero_kv = jnp.zeros((Bp, 0, SWA_KV, SWA_HD), xp.dtype)
    zero_conv = jnp.zeros((Bp, CONV_W - 1, 2 * D_FF), xp.dtype)
    for l in range(DEPTH):
        lw = dict(norm_mix_g=norm_mix_g[l], w_in=w_in[l], gmlp_norm_g=gmlp_norm_g[l],
                  gmlp_ws=gmlp_ws[l], gmlp_bs=gmlp_bs[l], attn_sinks=attn_sinks[l],
                  w_br_g=w_br_g[l], w_br_a=w_br_a[l], w_br_m=w_br_m[l], w_out=w_out[l],
                  norm_ffn_g=norm_ffn_g[l], w_up=w_up[l], conv_w=conv_w[l], conv_b=conv_b[l],
                  w_down=w_down[l])
        mkv = rmsnorm(mem_prompt, mem_norm_g[l]) @ w_mem_kv[l]
        mk = mkv[..., :MEM_WIDTH].reshape(Bp, MEM_LEN, MEM_HEADS, MEM_HD)
        mv = mkv[..., MEM_WIDTH:].reshape(Bp, MEM_LEN, MEM_HEADS, MEM_HD)
        xp, kb, vb, gv, cb = layer_forward(xp, 0, zero_kv, zero_kv, mk, mv, zero_conv, slopes, **lw)
        pk.append(kb); pv.append(vb); mkp.append(mk); mvp.append(mv); gvp.append(gv); cvp.append(cb)
        xs, kb, vb, gv, cb = layer_forward(xs, PAST_LEN, cache_swa_k[l], cache_swa_v[l], cache_mem_k[l],
                                           cache_mem_v[l], state_conv[l], slopes, **lw)
        sk.append(kb); sv.append(vb); gvs.append(gv); cvs.append(cb)
    y_prompt = rmsnorm(xp, final_norm_g)
    y_sample = rmsnorm(xs, final_norm_g)
    return (y_prompt, y_sample, jnp.stack(pk), jnp.stack(pv), jnp.stack(sk), jnp.stack(sv),
            jnp.stack(mkp), jnp.stack(mvp), jnp.stack(gvp), jnp.stack(gvs), jnp.stack(cvp), jnp.stack(cvs))
```

```python
import functools
import math

import jax
import jax.numpy as jnp
from jax import lax
from jax.experimental import pallas as pl
from jax.experimental.pallas import tpu as pltpu

D_MODEL = 2048
DEPTH = 2
PAST_LEN = 8192
MEM_LEN = 256
CHUNK = 128
G_GROUPS = 8
G_WIDTH = 768
G_GDIM = G_WIDTH // G_GROUPS
SWA_HEADS = 12
SWA_KV = 4
SWA_GROUP = SWA_HEADS // SWA_KV
SWA_HD = 64
SWA_WIDTH = SWA_HEADS * SWA_HD
SWA_KV_WIDTH = SWA_KV * SWA_HD
WINDOW = 128
MEM_HEADS = 4
MEM_HD = 128
MEM_WIDTH = MEM_HEADS * MEM_HD
N_BRANCH = 3
D_FF = 5632
CONV_W = 3
EPS = 1e-6

LANES = 128
SUBLANES = 8
G_PAD = G_GROUPS * LANES
VMEM_LIMIT = 56 * 1024 * 1024

C_U, C_V, C_Q, C_K, C_VV, C_M, C_G = 0, 768, 1536, 2304, 2560, 2816, 3328
S_U, S_V, S_Q, S_K, S_VV, S_M, S_END = 0, 1024, 2048, 2816, 3072, 3328, 3840

BF = jnp.bfloat16
F32 = jnp.float32
NT_DIMS = (((1,), (1,)), ((), ()))


def _alibi_slopes(n):
    p = 2 ** int(math.floor(math.log2(n)))
    base = [2.0 ** (-8.0 * (i + 1) / p) for i in range(p)]
    extra = [2.0 ** (-8.0 * (2 * i + 1) / (2 * p)) for i in range(n - p)]
    return base + extra


SLOPES = _alibi_slopes(SWA_HEADS)


def _params(*sem):
    return pltpu.CompilerParams(dimension_semantics=sem, vmem_limit_bytes=VMEM_LIMIT)


def _resident(shape):
    return pl.BlockSpec(shape, lambda *_: (0,) * len(shape), pipeline_mode=pl.Buffered(1))


def _rms(x, g):
    ms = jnp.mean(x * x, axis=-1, keepdims=True)
    return x * lax.rsqrt(ms + EPS) * g


def _in_kernel(x_ref, g_ref, w_ref, gg_ref, xn_ref, u_ref, vg_ref, q_ref, k_ref, v_ref, qm_ref):
    xn = _rms(x_ref[...], g_ref[...]).astype(BF)
    xn_ref[...] = xn

    def seg(a, b):
        return jnp.dot(xn, w_ref[:, a:b], preferred_element_type=F32)

    u_ref[...] = jax.nn.gelu(seg(S_U, S_V))
    gv = jax.nn.gelu(seg(S_V, S_Q))
    ms = jnp.sum(gv * gv, axis=-1, keepdims=True) / G_WIDTH
    vg_ref[...] = gv * lax.rsqrt(ms + EPS) * gg_ref[...]
    q_ref[...] = seg(S_Q, S_K).astype(BF)
    k_ref[...] = seg(S_K, S_VV)
    v_ref[...] = seg(S_VV, S_M)
    qm_ref[...] = seg(S_M, S_END).astype(BF)


def _in_proj(x, g, w_small, gg, tm):
    m = x.shape[0]
    row = lambda n: pl.BlockSpec((tm, n), lambda i: (i, 0))
    outs = [(D_MODEL, BF), (G_PAD, F32), (G_PAD, F32), (SWA_WIDTH, BF),
            (SWA_KV_WIDTH, F32), (SWA_KV_WIDTH, F32), (MEM_WIDTH, BF)]
    return pl.pallas_call(
        _in_kernel,
        grid=(m // tm,),
        in_specs=[row(D_MODEL), _resident((1, D_MODEL)), _resident((D_MODEL, S_END)), _resident((1, G_PAD))],
        out_specs=[row(n) for n, _ in outs],
        out_shape=[jax.ShapeDtypeStruct((m, n), dt) for n, dt in outs],
        compiler_params=_params("arbitrary"),
        name="in_proj",
    )(x, g, w_small, gg)


def _gmlp_kernel(u_ref, vg_ref, ws_ref, bsb_ref, og_ref, *, nchunk):
    r = lax.broadcasted_iota(jnp.int32, (CHUNK, CHUNK), 0)
    c = lax.broadcasted_iota(jnp.int32, (CHUNK, CHUNK), 1)
    tri = r >= c
    for g in range(G_GROUPS):
        w = jnp.where(tri, ws_ref[g], 0.0).astype(BF)
        cols = slice(g * LANES, (g + 1) * LANES)
        for n in range(nchunk):
            rows = slice(n * CHUNK, (n + 1) * CHUNK)
            mix = jnp.dot(w, vg_ref[rows, cols].astype(BF), preferred_element_type=F32) + bsb_ref[:, cols]
            og_ref[rows, cols] = (u_ref[rows, cols] * mix).astype(BF)


def _gmlp(u, vg, ws, bsb, tm):
    m = u.shape[0]
    row = pl.BlockSpec((tm, G_PAD), lambda i: (i, 0))
    return pl.pallas_call(
        functools.partial(_gmlp_kernel, nchunk=tm // CHUNK),
        grid=(m // tm,),
        in_specs=[row, row, _resident((G_GROUPS, CHUNK, CHUNK)), _resident((CHUNK, G_PAD))],
        out_specs=row,
        out_shape=jax.ShapeDtypeStruct((m, G_PAD), BF),
        compiler_params=_params("arbitrary"),
        name="gmlp_spatial",
    )(u, vg, ws, bsb)


def _gmlp_sample_kernel(u_ref, vg_ref, coef_ref, og_ref):
    og_ref[...] = (u_ref[...] * (coef_ref[0:1, :] * vg_ref[...] + coef_ref[1:2, :])).astype(BF)


def _gmlp_sample(u, vg, coef):
    m = u.shape[0]
    return pl.pallas_call(
        _gmlp_sample_kernel,
        out_shape=jax.ShapeDtypeStruct((m, G_PAD), BF),
        name="gmlp_sample",
    )(u, vg, coef)


def _swa_kernel(sink_ref, q_ref, kp_ref, kc_ref, vp_ref, vc_ref, o_ref):
    n = pl.program_id(1)
    kcat = jnp.concatenate([kp_ref[...], kc_ref[...]], axis=0).astype(BF)
    vcat = jnp.concatenate([vp_ref[...], vc_ref[...]], axis=0).astype(BF)
    r = lax.broadcasted_iota(jnp.int32, (WINDOW, 2 * WINDOW), 0)
    c = lax.broadcasted_iota(jnp.int32, (WINDOW, 2 * WINDOW), 1)
    dist = r + WINDOW - c
    valid = (dist >= 0) & (dist <= WINDOW) & ((c >= WINDOW) | (n > 0))
    distf = dist.astype(F32)
    for j in range(SWA_KV):
        kj = kcat[:, j * SWA_HD:(j + 1) * SWA_HD]
        vj = vcat[:, j * SWA_HD:(j + 1) * SWA_HD]
        for g in range(SWA_GROUP):
            h = j * SWA_GROUP + g
            qh = q_ref[:, h * SWA_HD:(h + 1) * SWA_HD]
            s = lax.dot_general(qh, kj, NT_DIMS, preferred_element_type=F32) * (SWA_HD ** -0.5)
            s = jnp.where(valid, s - SLOPES[h] * distf, -jnp.inf)
            sink = sink_ref[h]
            mx = jnp.maximum(jnp.max(s, axis=-1, keepdims=True), sink)
            p = jnp.exp(s - mx)
            den = jnp.sum(p, axis=-1, keepdims=True) + jnp.exp(sink - mx)
            p = (p / den).astype(BF)
            o = jnp.dot(p, vj, preferred_element_type=F32)
            o_ref[:, h * SWA_HD:(h + 1) * SWA_HD] = o.astype(BF)


def _swa(sinks, q, k, v, batch, seq):
    nb = seq // WINDOW
    cur = lambda n: pl.BlockSpec((WINDOW, n), lambda b, i: (b * nb + i, 0))
    prev = lambda n: pl.BlockSpec((WINDOW, n), lambda b, i: (b * nb + jnp.maximum(i - 1, 0), 0))
    return pl.pallas_call(
        _swa_kernel,
        grid=(batch, nb),
        in_specs=[pl.BlockSpec(memory_space=pltpu.SMEM), cur(SWA_WIDTH), prev(SWA_KV_WIDTH), cur(SWA_KV_WIDTH),
                  prev(SWA_KV_WIDTH), cur(SWA_KV_WIDTH)],
        out_specs=cur(SWA_WIDTH),
        out_shape=jax.ShapeDtypeStruct((batch * seq, SWA_WIDTH), BF),
        compiler_params=_params("arbitrary", "arbitrary"),
        name="swa_prompt",
    )(sinks, q, k, k, v, v)


def _swa_sample_kernel(q_ref, kc_ref, vc_ref, kn_ref, vn_ref, slope_ref, sink_ref, o_ref, *, nreq):
    hrow = lax.broadcasted_iota(jnp.int32, (SWA_HEADS, SWA_KV_WIDTH), 0) // SWA_GROUP
    lblk = lax.broadcasted_iota(jnp.int32, (SWA_HEADS, SWA_KV_WIDTH), 1) // SWA_HD
    own = hrow == lblk
    kvh = lax.broadcasted_iota(jnp.int32, (SWA_HEADS, SWA_HD), 0) // SWA_GROUP
    c = lax.broadcasted_iota(jnp.int32, (SWA_HEADS, WINDOW), 1)
    bias = slope_ref[...] * (WINDOW - c).astype(F32)
    sink = sink_ref[:, 0:1]
    scale = SWA_HD ** -0.5
    for r in range(nreq):
        q = q_ref[r]
        qm = jnp.where(own, jnp.concatenate([q] * SWA_KV, axis=1), 0.0).astype(BF)
        kn = kn_ref[r:r + 1, :].astype(BF).astype(F32)
        vn = vn_ref[r:r + 1, :].astype(BF).astype(F32)
        s_c = lax.dot_general(qm, kc_ref[r].astype(BF), NT_DIMS, preferred_element_type=F32) * scale - bias
        s_n = jnp.sum(qm.astype(F32) * kn, axis=-1, keepdims=True) * scale
        mx = jnp.maximum(jnp.maximum(jnp.max(s_c, axis=-1, keepdims=True), s_n), sink)
        p_c = jnp.exp(s_c - mx)
        p_n = jnp.exp(s_n - mx)
        den = jnp.sum(p_c, axis=-1, keepdims=True) + p_n + jnp.exp(sink - mx)
        p_c = (p_c / den).astype(BF)
        p_n = (p_n / den).astype(BF).astype(F32)
        o_all = jnp.dot(p_c, vc_ref[r].astype(BF), preferred_element_type=F32) + p_n * vn
        o = jnp.zeros((SWA_HEADS, SWA_HD), F32)
        for j in range(SWA_KV):
            o = jnp.where(kvh == j, o_all[:, j * SWA_HD:(j + 1) * SWA_HD], o)
        o_ref[r] = o


def _swa_sample(q3, kc, vc, kn, vn, slope_b, sink_b, nreq):
    m = q3.shape[0]
    return pl.pallas_call(
        functools.partial(_swa_sample_kernel, nreq=nreq),
        grid=(m // nreq,),
        in_specs=[pl.BlockSpec((nreq, SWA_HEADS, SWA_HD), lambda i: (i, 0, 0)),
                  pl.BlockSpec((nreq, WINDOW, SWA_KV_WIDTH), lambda i: (i, 0, 0)),
                  pl.BlockSpec((nreq, WINDOW, SWA_KV_WIDTH), lambda i: (i, 0, 0)),
                  pl.BlockSpec((nreq, SWA_KV_WIDTH), lambda i: (i, 0)),
                  pl.BlockSpec((nreq, SWA_KV_WIDTH), lambda i: (i, 0)),
                  _resident((SWA_HEADS, LANES)), _resident((SWA_HEADS, LANES))],
        out_specs=pl.BlockSpec((nreq, SWA_HEADS, SWA_HD), lambda i: (i, 0, 0)),
        out_shape=jax.ShapeDtypeStruct((m, SWA_HEADS, SWA_HD), F32),
        compiler_params=_params("arbitrary"),
        name="swa_sample",
    )(q3, kc, vc, kn, vn, slope_b, sink_b)


def _memkv_kernel(x_ref, g_ref, w_ref, mk_ref, mv_ref):
    xn = _rms(x_ref[...], g_ref[...]).astype(BF)
    mk_ref[...] = jnp.dot(xn, w_ref[:, :MEM_WIDTH], preferred_element_type=F32)
    mv_ref[...] = jnp.dot(xn, w_ref[:, MEM_WIDTH:], preferred_element_type=F32)


def _memkv(mem, g, w, tm):
    m = mem.shape[0]
    out = pl.BlockSpec((tm, MEM_WIDTH), lambda i: (i, 0))
    return pl.pallas_call(
        _memkv_kernel,
        grid=(m // tm,),
        in_specs=[pl.BlockSpec((tm, D_MODEL), lambda i: (i, 0)), _resident((1, D_MODEL)),
                  _resident((D_MODEL, 2 * MEM_WIDTH))],
        out_specs=[out, out],
        out_shape=[jax.ShapeDtypeStruct((m, MEM_WIDTH), F32)] * 2,
        compiler_params=_params("arbitrary"),
        name="mem_kv",
    )(mem, g, w)


def _mem_attn_kernel(q_ref, mk_ref, mv_ref, o_ref):
    for h in range(MEM_HEADS):
        cols = slice(h * MEM_HD, (h + 1) * MEM_HD)
        s = lax.dot_general(q_ref[:, cols], mk_ref[:, cols].astype(BF), NT_DIMS,
                            preferred_element_type=F32) * (MEM_HD ** -0.5)
        e = jnp.exp(s - jnp.max(s, axis=-1, keepdims=True))
        p = (e / jnp.sum(e, axis=-1, keepdims=True)).astype(BF)
        o_ref[:, cols] = jnp.dot(p, mv_ref[:, cols].astype(BF), preferred_element_type=F32).astype(BF)


def _mem_attn(qm, mk, mv, batch, seq, tq):
    nq = seq // tq
    qspec = pl.BlockSpec((tq, MEM_WIDTH), lambda b, i: (b * nq + i, 0))
    kvspec = pl.BlockSpec((MEM_LEN, MEM_WIDTH), lambda b, i: (b, 0))
    return pl.pallas_call(
        _mem_attn_kernel,
        grid=(batch, nq),
        in_specs=[qspec, kvspec, kvspec],
        out_specs=qspec,
        out_shape=jax.ShapeDtypeStruct((batch * seq, MEM_WIDTH), BF),
        compiler_params=_params("arbitrary", "arbitrary"),
        name="mem_attn_prompt",
    )(qm, mk, mv)


def _mem_sample_kernel(q_ref, mk_ref, mv_ref, o_ref, *, nreq):
    hrow = lax.broadcasted_iota(jnp.int32, (MEM_HEADS, MEM_WIDTH), 0)
    lblk = lax.broadcasted_iota(jnp.int32, (MEM_HEADS, MEM_WIDTH), 1) // MEM_HD
    own = hrow == lblk
    hsel = lax.broadcasted_iota(jnp.int32, (MEM_HEADS, MEM_HD), 0)
    for r in range(nreq):
        q = q_ref[r]
        qm = jnp.where(own, jnp.concatenate([q] * MEM_HEADS, axis=1), 0.0).astype(BF)
        s = lax.dot_general(qm, mk_ref[r].astype(BF), NT_DIMS, preferred_element_type=F32) * (MEM_HD ** -0.5)
        e = jnp.exp(s - jnp.max(s, axis=-1, keepdims=True))
        p = (e / jnp.sum(e, axis=-1, keepdims=True)).astype(BF)
        o_all = jnp.dot(p, mv_ref[r].astype(BF), preferred_element_type=F32)
        o = jnp.zeros((MEM_HEADS, MEM_HD), F32)
        for h in range(MEM_HEADS):
            o = jnp.where(hsel == h, o_all[:, h * MEM_HD:(h + 1) * MEM_HD], o)
        o_ref[r] = o


def _mem_sample(q3, mk, mv, nreq):
    m = q3.shape[0]
    qspec = pl.BlockSpec((nreq, MEM_HEADS, MEM_HD), lambda i: (i, 0, 0))
    kvspec = pl.BlockSpec((nreq, MEM_LEN, MEM_WIDTH), lambda i: (i, 0, 0))
    return pl.pallas_call(
        functools.partial(_mem_sample_kernel, nreq=nreq),
        grid=(m // nreq,),
        in_specs=[qspec, kvspec, kvspec],
        out_specs=qspec,
        out_shape=jax.ShapeDtypeStruct((m, MEM_HEADS, MEM_HD), F32),
        compiler_params=_params("arbitrary"),
        name="mem_attn_sample",
    )(q3, mk, mv)


def _merge_kernel(xn_ref, og_ref, oa_ref, om_ref, wg0_ref, wg1_ref, wg2_ref, wbg_ref, wba_ref, wbm_ref, out_ref):
    xn = xn_ref[...]

    def branch(wg_ref, o_ref, wb_ref):
        gate = jax.nn.sigmoid(jnp.dot(xn, wg_ref[...], preferred_element_type=F32))
        return gate * jnp.dot(o_ref[...], wb_ref[...], preferred_element_type=F32)

    merged = branch(wg0_ref, og_ref, wbg_ref) + branch(wg1_ref, oa_ref, wba_ref) + branch(wg2_ref, om_ref, wbm_ref)
    out_ref[...] = merged.astype(BF)


def _merge(xn, og, oa, om, w_gate, wbg, wba, wbm, tm, tn):
    m = xn.shape[0]
    nt = D_MODEL // tn
    row = lambda n: pl.BlockSpec((tm, n), lambda i, j: (i, 0))
    gate = lambda b: pl.BlockSpec((D_MODEL, tn), lambda i, j: (0, b * nt + j))
    col = lambda n: pl.BlockSpec((n, tn), lambda i, j: (0, j))
    return pl.pallas_call(
        _merge_kernel,
        grid=(m // tm, nt),
        in_specs=[row(D_MODEL), row(G_PAD), row(SWA_WIDTH), row(MEM_WIDTH), gate(0), gate(1), gate(2),
                  col(G_PAD), col(SWA_WIDTH), col(MEM_WIDTH)],
        out_specs=pl.BlockSpec((tm, tn), lambda i, j: (i, j)),
        out_shape=jax.ShapeDtypeStruct((m, D_MODEL), BF),
        compiler_params=_params("arbitrary", "arbitrary"),
        name="merge",
    )(xn, og, oa, om, w_gate, w_gate, w_gate, wbg, wba, wbm)


def _out_kernel(x_ref, mg_ref, w_ref, g_ref, xo_ref, xn_ref):
    x = x_ref[...] + jnp.dot(mg_ref[...], w_ref[...], preferred_element_type=F32)
    xo_ref[...] = x
    xn_ref[...] = _rms(x, g_ref[...]).astype(BF)


def _out_proj(x, merged, w_out, g, tm):
    m = x.shape[0]
    row = pl.BlockSpec((tm, D_MODEL), lambda i: (i, 0))
    return pl.pallas_call(
        _out_kernel,
        grid=(m // tm,),
        in_specs=[row, row, _resident((D_MODEL, D_MODEL)), _resident((1, D_MODEL))],
        out_specs=[row, row],
        out_shape=[jax.ShapeDtypeStruct((m, D_MODEL), F32), jax.ShapeDtypeStruct((m, D_MODEL), BF)],
        compiler_params=_params("arbitrary"),
        name="out_proj",
    )(x, merged, w_out, g)


def _up_kernel(xn_ref, wa_ref, wb_ref, cwa_ref, cwb_ref, g_ref, ca_ref, cb_ref, ha_ref, hb_ref, *, tm, tiles_per_seq):
    @pl.when(pl.program_id(1) % tiles_per_seq == 0)
    def _():
        ha_ref[0:SUBLANES, :] = jnp.zeros((SUBLANES, ha_ref.shape[1]), F32)
        hb_ref[0:SUBLANES, :] = jnp.zeros((SUBLANES, hb_ref.shape[1]), F32)

    xn = xn_ref[...]

    def conv(w_ref, h_ref, cw_ref, tail_ref):
        h_ref[SUBLANES:SUBLANES + tm, :] = jnp.dot(xn, w_ref[...], preferred_element_type=F32)
        acc = cw_ref[CONV_W:CONV_W + 1, :]
        for j in range(CONV_W):
            lo = SUBLANES - (CONV_W - 1) + j
            acc = acc + h_ref[lo:lo + tm, :] * cw_ref[j:j + 1, :]
        tail = h_ref[tm:tm + SUBLANES, :]
        tail_ref[...] = tail
        h_ref[0:SUBLANES, :] = tail
        return acc

    a = conv(wa_ref, ha_ref, cwa_ref, ca_ref)
    b = conv(wb_ref, hb_ref, cwb_ref, cb_ref)
    g_ref[...] = (jax.nn.gelu(a) * b).astype(BF)


def _up_conv(xn, w_up, cw, batch, seq, tm, tn):
    m = xn.shape[0]
    nt = D_FF // tn
    tiles_per_seq = seq // tm
    a_col = lambda r: pl.BlockSpec((r, tn), lambda j, i: (0, j))
    b_col = lambda r: pl.BlockSpec((r, tn), lambda j, i: (0, nt + j))
    tail = pl.BlockSpec((SUBLANES, tn), lambda j, i: (i // tiles_per_seq, j))
    return pl.pallas_call(
        functools.partial(_up_kernel, tm=tm, tiles_per_seq=tiles_per_seq),
        grid=(nt, m // tm),
        in_specs=[pl.BlockSpec((tm, D_MODEL), lambda j, i: (i, 0)), a_col(D_MODEL), b_col(D_MODEL),
                  a_col(SUBLANES), b_col(SUBLANES)],
        out_specs=[pl.BlockSpec((tm, tn), lambda j, i: (i, j)), tail, tail],
        out_shape=[jax.ShapeDtypeStruct((m, D_FF), BF),
                   jax.ShapeDtypeStruct((batch * SUBLANES, D_FF), F32),
                   jax.ShapeDtypeStruct((batch * SUBLANES, D_FF), F32)],
        scratch_shapes=[pltpu.VMEM((tm + SUBLANES, tn), F32), pltpu.VMEM((tm + SUBLANES, tn), F32)],
        compiler_params=_params("arbitrary", "arbitrary"),
        name="up_conv_glu",
    )(xn, w_up, w_up, cw, cw)


def _up_sample_kernel(xn_ref, wa_ref, wb_ref, cwa_ref, cwb_ref, s0a_ref, s0b_ref, s1a_ref, s1b_ref,
                      g_ref, ha_ref, hb_ref):
    xn = xn_ref[...]

    def conv(w_ref, cw_ref, s0_ref, s1_ref, h_ref):
        h = jnp.dot(xn, w_ref[...], preferred_element_type=F32)
        h_ref[...] = h
        return (cw_ref[3:4, :] + s0_ref[...] * cw_ref[0:1, :] + s1_ref[...] * cw_ref[1:2, :]
                + h * cw_ref[2:3, :])

    a = conv(wa_ref, cwa_ref, s0a_ref, s1a_ref, ha_ref)
    b = conv(wb_ref, cwb_ref, s0b_ref, s1b_ref, hb_ref)
    g_ref[...] = (jax.nn.gelu(a) * b).astype(BF)


def _up_sample(xn, w_up, cw, state, tn):
    m = xn.shape[0]
    nt = D_FF // tn
    col = lambda r, off: pl.BlockSpec((r, tn), lambda j: (0, off * nt + j))
    out = pl.BlockSpec((m, tn), lambda j: (0, j))
    return pl.pallas_call(
        _up_sample_kernel,
        grid=(nt,),
        in_specs=[_resident((m, D_MODEL)), col(D_MODEL, 0), col(D_MODEL, 1), col(SUBLANES, 0), col(SUBLANES, 1),
                  col(m, 0), col(m, 1), col(m, 2), col(m, 3)],
        out_specs=[out, out, out],
        out_shape=[jax.ShapeDtypeStruct((m, D_FF), BF), jax.ShapeDtypeStruct((m, D_FF), F32),
                   jax.ShapeDtypeStruct((m, D_FF), F32)],
        compiler_params=_params("arbitrary"),
        name="up_conv_glu_sample",
    )(xn, w_up, w_up, cw, cw, state, state, state, state)


def _down_kernel(x_ref, g_ref, w_ref, o_ref):
    o_ref[...] = x_ref[...] + jnp.dot(g_ref[...], w_ref[...], preferred_element_type=F32)


def _down_proj(x, g, w_down, tm, tn):
    m = x.shape[0]
    return pl.pallas_call(
        _down_kernel,
        grid=(D_MODEL // tn, m // tm),
        in_specs=[pl.BlockSpec((tm, tn), lambda j, i: (i, j)), pl.BlockSpec((tm, D_FF), lambda j, i: (i, 0)),
                  pl.BlockSpec((D_FF, tn), lambda j, i: (0, j))],
        out_specs=pl.BlockSpec((tm, tn), lambda j, i: (i, j)),
        out_shape=jax.ShapeDtypeStruct((m, D_MODEL), F32),
        compiler_params=_params("arbitrary", "arbitrary"),
        name="down_proj",
    )(x, g, w_down)


def _norm_kernel(x_ref, g_ref, o_ref):
    o_ref[...] = _rms(x_ref[...], g_ref[...])


def _final_norm(x, g, tm):
    m = x.shape[0]
    row = pl.BlockSpec((tm, D_MODEL), lambda i: (i, 0))
    return pl.pallas_call(
        _norm_kernel,
        grid=(m // tm,),
        in_specs=[row, _resident((1, D_MODEL))],
        out_specs=row,
        out_shape=jax.ShapeDtypeStruct((m, D_MODEL), F32),
        compiler_params=_params("arbitrary"),
        name="final_norm",
    )(x, g)


def _pad_groups(w, axis):
    shp = w.shape
    w = w.reshape(shp[:axis] + (G_GROUPS, G_GDIM) + shp[axis + 1:])
    pad = [(0, 0)] * w.ndim
    pad[axis + 1] = (0, LANES - G_GDIM)
    return jnp.pad(w, pad).reshape(shp[:axis] + (G_PAD,) + shp[axis + 1:])


def _unpad_groups(v):
    return v.reshape(v.shape[:-1] + (G_GROUPS, LANES))[..., :G_GDIM].reshape(v.shape[:-1] + (G_WIDTH,))


def _row(v):
    return v.reshape(1, -1)


def _layer_weights(l, norm_mix_g, w_in, gmlp_norm_g, gmlp_ws, gmlp_bs, attn_sinks, mem_norm_g, w_mem_kv,
                   w_br_g, w_br_a, w_br_m, w_out, norm_ffn_g, w_up, conv_w, conv_b, w_down):
    wi = w_in[l]
    w_small = jnp.concatenate(
        [_pad_groups(wi[:, C_U:C_V], 1), _pad_groups(wi[:, C_V:C_Q], 1), wi[:, C_Q:C_G]], axis=1).astype(BF)
    bs = gmlp_bs[l]
    ws = gmlp_ws[l]
    lw = dict(
        norm_mix_g=_row(norm_mix_g[l]),
        w_small=w_small,
        w_gate=wi[:, C_G:].astype(BF),
        gmlp_norm_g=_row(_pad_groups(gmlp_norm_g[l], 0)),
        gmlp_ws=ws,
        gmlp_bsb=jnp.repeat(bs.T, LANES, axis=1),
        gmlp_coef=jnp.stack([jnp.repeat(ws[:, 0, 0], LANES), jnp.repeat(bs[:, 0], LANES)]),
        sinks=attn_sinks[l],
        sink_b=jnp.broadcast_to(attn_sinks[l][:, None], (SWA_HEADS, LANES)),
        mem_norm_g=_row(mem_norm_g[l]),
        w_mem_kv=w_mem_kv[l].astype(BF),
        w_br_g=_pad_groups(w_br_g[l], 0).astype(BF),
        w_br_a=w_br_a[l].astype(BF),
        w_br_m=w_br_m[l].astype(BF),
        w_out=w_out[l].astype(BF),
        norm_ffn_g=_row(norm_ffn_g[l]),
        w_up=w_up[l].astype(BF),
        cw=jnp.concatenate([conv_w[l], conv_b[l][None], jnp.zeros((SUBLANES - CONV_W - 1, 2 * D_FF), F32)], axis=0),
        w_down=w_down[l].astype(BF),
    )
    return lw


def _dense_tail(x, xn, og, oa, om, lw, tm_merge, tm_out):
    merged = _merge(xn, og, oa, om, lw["w_gate"], lw["w_br_g"], lw["w_br_a"], lw["w_br_m"], tm_merge, 512)
    return _out_proj(x, merged, lw["w_out"], lw["norm_ffn_g"], tm_out)


def _prompt_layer(x, mem, lw, batch, seq):
    xn, u, vg, q, k, v, qm = _in_proj(x, lw["norm_mix_g"], lw["w_small"], lw["gmlp_norm_g"], 512)
    og = _gmlp(u, vg, lw["gmlp_ws"], lw["gmlp_bsb"], 512)
    oa = _swa(lw["sinks"], q, k, v, batch, seq)
    mk, mv = _memkv(mem, lw["mem_norm_g"], lw["w_mem_kv"], 512)
    om = _mem_attn(qm, mk, mv, batch, seq, 512)
    x, xn2 = _dense_tail(x, xn, og, oa, om, lw, 1024, 512)
    g, ca, cb = _up_conv(xn2, lw["w_up"], lw["cw"], batch, seq, 1024, 512)
    x = _down_proj(x, g, lw["w_down"], 512, 1024)
    tail = lambda t: t.reshape(batch, SUBLANES, D_FF)[:, SUBLANES - (CONV_W - 1):]
    state = dict(
        k=k.reshape(batch, seq, SWA_KV, SWA_HD)[:, seq - WINDOW:],
        v=v.reshape(batch, seq, SWA_KV, SWA_HD)[:, seq - WINDOW:],
        mk=mk.reshape(batch, MEM_LEN, MEM_HEADS, MEM_HD),
        mv=mv.reshape(batch, MEM_LEN, MEM_HEADS, MEM_HD),
        gv=_unpad_groups(vg.reshape(batch, seq, G_PAD)[:, ((seq - 1) // CHUNK) * CHUNK:]),
        conv=jnp.concatenate([tail(ca), tail(cb)], axis=-1),
    )
    return x, state


def _sample_layer(x, swa_k, swa_v, mem_k, mem_v, conv_state, slope_b, lw):
    m = x.shape[0]
    xn, u, vg, q, k, v, qm = _in_proj(x, lw["norm_mix_g"], lw["w_small"], lw["gmlp_norm_g"], m)
    og = _gmlp_sample(u, vg, lw["gmlp_coef"])
    oa = _swa_sample(q.astype(F32).reshape(m, SWA_HEADS, SWA_HD), swa_k.reshape(m, WINDOW, SWA_KV_WIDTH),
                     swa_v.reshape(m, WINDOW, SWA_KV_WIDTH), k, v, slope_b, lw["sink_b"], 8)
    om = _mem_sample(qm.astype(F32).reshape(m, MEM_HEADS, MEM_HD), mem_k.reshape(m, MEM_LEN, MEM_WIDTH),
                     mem_v.reshape(m, MEM_LEN, MEM_WIDTH), 8)
    x, xn2 = _dense_tail(x, xn, og, oa.reshape(m, SWA_WIDTH).astype(BF), om.reshape(m, MEM_WIDTH).astype(BF), lw, m, m)
    g, ha, hb = _up_sample(xn2, lw["w_up"], lw["cw"], conv_state.reshape(m, (CONV_W - 1) * 2 * D_FF), 512)
    x = _down_proj(x, g, lw["w_down"], m, 1024)
    state = dict(
        k=jnp.concatenate([swa_k[:, 1:], k.reshape(m, 1, SWA_KV, SWA_HD)], axis=1),
        v=jnp.concatenate([swa_v[:, 1:], v.reshape(m, 1, SWA_KV, SWA_HD)], axis=1),
        gv=_unpad_groups(vg).reshape(m, 1, G_WIDTH),
        conv=jnp.concatenate([conv_state[:, 1:], jnp.concatenate([ha, hb], axis=-1)[:, None]], axis=1),
    )
    return x, state


def kernel(x_prompt, x_sample, cache_swa_k, cache_swa_v, cache_mem_k, cache_mem_v, state_conv, mem_prompt, norm_mix_g, w_in, gmlp_norm_g, gmlp_ws, gmlp_bs, attn_sinks, mem_norm_g, w_mem_kv, w_br_g, w_br_a, w_br_m, w_out, norm_ffn_g, w_up, conv_w, conv_b, w_down, final_norm_g):
    batch, seq, _ = x_prompt.shape
    dec = x_sample.shape[0]
    assert x_sample.shape[1] == 1 and PAST_LEN % CHUNK == 0 and seq % 1024 == 0
    xp = x_prompt.reshape(batch * seq, D_MODEL)
    xs = x_sample.reshape(dec, D_MODEL)
    mem = mem_prompt.reshape(batch * MEM_LEN, D_MODEL)
    slope_b = jnp.broadcast_to(jnp.asarray(SLOPES, F32)[:, None], (SWA_HEADS, LANES))
    ps, ss = [], []
    for l in range(DEPTH):
        lw = _layer_weights(l, norm_mix_g, w_in, gmlp_norm_g, gmlp_ws, gmlp_bs, attn_sinks, mem_norm_g, w_mem_kv,
                            w_br_g, w_br_a, w_br_m, w_out, norm_ffn_g, w_up, conv_w, conv_b, w_down)
        xp, st = _prompt_layer(xp, mem, lw, batch, seq)
        ps.append(st)
        xs, st = _sample_layer(xs, cache_swa_k[l], cache_swa_v[l], cache_mem_k[l], cache_mem_v[l], state_conv[l],
                               slope_b, lw)
        ss.append(st)
    fg = _row(final_norm_g)
    y_prompt = _final_norm(xp, fg, 512).reshape(batch, seq, D_MODEL)
    y_sample = _final_norm(xs, fg, dec).reshape(dec, 1, D_MODEL)
    stack = lambda sts, key: jnp.stack([s[key] for s in sts])
    return (y_prompt, y_sample, stack(ps, "k"), stack(ps, "v"), stack(ss, "k"), stack(ss, "v"),
            stack(ps, "mk"), stack(ps, "mv"), stack(ps, "gv"), stack(ss, "gv"), stack(ps, "conv"), stack(ss, "conv"))
```

```python
import functools
import math

import jax
import jax.numpy as jnp
from jax import lax
from jax.experimental import pallas as pl
from jax.experimental.pallas import tpu as pltpu

D_MODEL = 2048
DEPTH = 2
PAST_LEN = 8192
MEM_LEN = 256
CHUNK = 128
G_GROUPS = 8
G_WIDTH = 768
G_GDIM = G_WIDTH // G_GROUPS
SWA_HEADS = 12
SWA_KV = 4
SWA_GROUP = SWA_HEADS // SWA_KV
SWA_HD = 64
SWA_WIDTH = SWA_HEADS * SWA_HD
SWA_KV_WIDTH = SWA_KV * SWA_HD
WINDOW = 128
MEM_HEADS = 4
MEM_HD = 128
MEM_WIDTH = MEM_HEADS * MEM_HD
N_BRANCH = 3
D_FF = 5632
CONV_W = 3
EPS = 1e-6

LANES = 128
SUBLANES = 8
VMEM_LIMIT = 56 * 1024 * 1024

C_U, C_V, C_Q, C_K, C_VV, C_M, C_G = 0, 768, 1536, 2304, 2560, 2816, 3328
ZT = 256
N_ACT = C_Q // ZT
assert all(c % ZT == 0 for c in (C_V, C_Q, C_K, C_VV, C_M, C_G))

BF = jnp.bfloat16
F32 = jnp.float32
NT_DIMS = (((1,), (1,)), ((), ()))


def _alibi_slopes(n):
    p = 2 ** int(math.floor(math.log2(n)))
    base = [2.0 ** (-8.0 * (i + 1) / p) for i in range(p)]
    extra = [2.0 ** (-8.0 * (2 * i + 1) / (2 * p)) for i in range(n - p)]
    return base + extra


SLOPES = _alibi_slopes(SWA_HEADS)


def _params(*sem):
    return pltpu.CompilerParams(dimension_semantics=sem, vmem_limit_bytes=VMEM_LIMIT)


def _resident(shape):
    return pl.BlockSpec(shape, lambda *_: (0,) * len(shape), pipeline_mode=pl.Buffered(1))


def _rms(x, g):
    ms = jnp.mean(x * x, axis=-1, keepdims=True)
    return x * lax.rsqrt(ms + EPS) * g


def _dot(a, b):
    return jnp.dot(a, b, preferred_element_type=F32)


def _first_inner():
    return pl.program_id(1) == 0


def _last_inner():
    return pl.program_id(1) == pl.num_programs(1) - 1


def _norm_kernel(x_ref, xs_ref, g_ref, o_ref, os_ref, *, l):
    g = g_ref[l:l + 1, :]
    o_ref[...] = _rms(x_ref[...], g).astype(o_ref.dtype)

    @pl.when(pl.program_id(0) == pl.num_programs(0) - 1)
    def _():
        os_ref[...] = _rms(xs_ref[...], g).astype(os_ref.dtype)


def _norm(x, xs, g, l, dtype, tm):
    m, ms = x.shape[0], xs.shape[0]
    row = pl.BlockSpec((tm, D_MODEL), lambda i: (i, 0))
    return pl.pallas_call(
        functools.partial(_norm_kernel, l=l),
        grid=(m // tm,),
        in_specs=[row, _resident((ms, D_MODEL)), _resident(g.shape)],
        out_specs=[row, pl.BlockSpec((ms, D_MODEL), lambda i: (0, 0))],
        out_shape=[jax.ShapeDtypeStruct((m, D_MODEL), dtype), jax.ShapeDtypeStruct((ms, D_MODEL), dtype)],
        compiler_params=_params("arbitrary"),
        name="rmsnorm",
    )(x, xs, g)


def _zsmall_kernel(xn_ref, xs_ref, w_ref, z_ref, zs_ref, wbf_ref):
    @pl.when(_first_inner())
    def _():
        wbf_ref[...] = w_ref[...].astype(BF)

    act = pl.program_id(0) < N_ACT

    @pl.when(act)
    def _():
        z_ref[...] = jax.nn.gelu(_dot(xn_ref[...], wbf_ref[...]))

    @pl.when(jnp.logical_not(act))
    def _():
        z_ref[...] = _dot(xn_ref[...], wbf_ref[...])

    @pl.when(_last_inner())
    def _():
        zs = _dot(xs_ref[...], wbf_ref[...])
        zs_ref[...] = jnp.where(act, jax.nn.gelu(zs), zs)


def _zsmall(xn, xs, w_in, l, tm):
    m, ms = xn.shape[0], xs.shape[0]
    return pl.pallas_call(
        _zsmall_kernel,
        grid=(C_G // ZT, m // tm),
        in_specs=[pl.BlockSpec((tm, D_MODEL), lambda j, i: (i, 0)), _resident((ms, D_MODEL)),
                  pl.BlockSpec((None, D_MODEL, ZT), lambda j, i: (l, 0, j))],
        out_specs=[pl.BlockSpec((tm, ZT), lambda j, i: (i, j)), pl.BlockSpec((ms, ZT), lambda j, i: (0, j))],
        out_shape=[jax.ShapeDtypeStruct((m, C_G), F32), jax.ShapeDtypeStruct((ms, C_G), F32)],
        scratch_shapes=[pltpu.VMEM((D_MODEL, ZT), BF)],
        compiler_params=_params("arbitrary", "arbitrary"),
        name="in_proj",
    )(xn, xs, w_in)


G_TILES = G_WIDTH // LANES
GA = [(t * LANES) // G_GDIM for t in range(G_TILES)]
BND = [(GA[t] + 1) * G_GDIM - t * LANES for t in range(G_TILES)]
assert all(0 < b < LANES and (GA[t] + 2) * G_GDIM >= (t + 1) * LANES for t, b in enumerate(BND))


def _gmlp_kernel(u_ref, gv_ref, ws_ref, gg_ref, bsb_ref, og_ref, vg_ref, *, l, nchunk):
    r = lax.broadcasted_iota(jnp.int32, (CHUNK, CHUNK), 0)
    c = lax.broadcasted_iota(jnp.int32, (CHUNK, CHUNK), 1)
    tri = r >= c
    wtri = [jnp.where(tri, ws_ref[g], 0.0).astype(BF) for g in range(G_GROUPS)]
    wst = [jnp.concatenate([wtri[GA[t]], wtri[GA[t] + 1]], axis=0) for t in range(G_TILES)]
    lane = lax.broadcasted_iota(jnp.int32, (CHUNK, LANES), 1)
    g = gg_ref[l:l + 1, :]
    for n in range(nchunk):
        rows = slice(n * CHUNK, (n + 1) * CHUNK)
        vg = _rms(gv_ref[rows, :], g)
        if n == nchunk - 1:
            vg_ref[...] = vg
        for t in range(G_TILES):
            cols = slice(t * LANES, (t + 1) * LANES)
            both = _dot(wst[t], vg[:, cols].astype(BF))
            mix = jnp.where(lane < BND[t], both[:CHUNK], both[CHUNK:]) + bsb_ref[:, cols]
            og_ref[rows, cols] = (u_ref[rows, cols] * mix).astype(BF)


def _gmlp(z, ws, gg, bsb, l, batch, seq, tr):
    nr = seq // tr
    zcol = lambda cb: pl.BlockSpec((tr, G_WIDTH), lambda b, i: (b * nr + i, cb))
    return pl.pallas_call(
        functools.partial(_gmlp_kernel, l=l, nchunk=tr // CHUNK),
        grid=(batch, nr),
        in_specs=[zcol(C_U // G_WIDTH), zcol(C_V // G_WIDTH),
                  pl.BlockSpec((None, G_GROUPS, CHUNK, CHUNK), lambda b, i: (l, 0, 0, 0)),
                  _resident(gg.shape), _resident((CHUNK, G_WIDTH))],
        out_specs=[pl.BlockSpec((tr, G_WIDTH), lambda b, i: (b * nr + i, 0)),
                   pl.BlockSpec((CHUNK, G_WIDTH), lambda b, i: (b, 0))],
        out_shape=[jax.ShapeDtypeStruct((batch * seq, G_WIDTH), BF),
                   jax.ShapeDtypeStruct((batch * CHUNK, G_WIDTH), F32)],
        compiler_params=_params("arbitrary", "arbitrary"),
        name="gmlp_spatial",
    )(z, z, ws, gg, bsb)


def _gmlp_sample_kernel(zs_ref, gg_ref, coef_ref, og_ref, vg_ref, *, l):
    vg = _rms(zs_ref[:, C_V:C_Q], gg_ref[l:l + 1, :])
    vg_ref[...] = vg
    og_ref[...] = (zs_ref[:, C_U:C_V] * (coef_ref[0:1, :] * vg + coef_ref[1:2, :])).astype(BF)


def _gmlp_sample(zs, gg, coef, l):
    m = zs.shape[0]
    return pl.pallas_call(
        functools.partial(_gmlp_sample_kernel, l=l),
        out_shape=[jax.ShapeDtypeStruct((m, G_WIDTH), BF), jax.ShapeDtypeStruct((m, G_WIDTH), F32)],
        name="gmlp_sample",
    )(zs, gg, coef)


def _swa_kernel(sink_ref, q_ref, kp_ref, kc_ref, vp_ref, vc_ref, o_ref, *, l):
    n = pl.program_id(1)
    kcat = jnp.concatenate([kp_ref[...], kc_ref[...]], axis=0).astype(BF)
    vcat = jnp.concatenate([vp_ref[...], vc_ref[...]], axis=0).astype(BF)
    r = lax.broadcasted_iota(jnp.int32, (WINDOW, 2 * WINDOW), 0)
    c = lax.broadcasted_iota(jnp.int32, (WINDOW, 2 * WINDOW), 1)
    dist = r + WINDOW - c
    valid = (dist >= 0) & (dist <= WINDOW) & ((c >= WINDOW) | (n > 0))
    distf = dist.astype(F32)
    for j in range(SWA_KV):
        kj = kcat[:, j * SWA_HD:(j + 1) * SWA_HD]
        vj = vcat[:, j * SWA_HD:(j + 1) * SWA_HD]
        for g in range(SWA_GROUP):
            h = j * SWA_GROUP + g
            qh = q_ref[:, h * SWA_HD:(h + 1) * SWA_HD].astype(BF)
            s = lax.dot_general(qh, kj, NT_DIMS, preferred_element_type=F32) * (SWA_HD ** -0.5)
            s = jnp.where(valid, s - SLOPES[h] * distf, -jnp.inf)
            sink = sink_ref[l, h]
            mx = jnp.maximum(jnp.max(s, axis=-1, keepdims=True), sink)
            p = jnp.exp(s - mx)
            den = jnp.sum(p, axis=-1, keepdims=True) + jnp.exp(sink - mx)
            p = (p / den).astype(BF)
            o_ref[:, h * SWA_HD:(h + 1) * SWA_HD] = _dot(p, vj).astype(BF)


def _swa(sinks, z, l, batch, seq):
    nb = seq // WINDOW
    cur = lambda n, cb: pl.BlockSpec((WINDOW, n), lambda b, i: (b * nb + i, cb))
    prev = lambda n, cb: pl.BlockSpec((WINDOW, n), lambda b, i: (b * nb + jnp.maximum(i - 1, 0), cb))
    kcb, vcb = C_K // SWA_KV_WIDTH, C_VV // SWA_KV_WIDTH
    return pl.pallas_call(
        functools.partial(_swa_kernel, l=l),
        grid=(batch, nb),
        in_specs=[pl.BlockSpec(memory_space=pltpu.SMEM), cur(SWA_WIDTH, C_Q // SWA_WIDTH),
                  prev(SWA_KV_WIDTH, kcb), cur(SWA_KV_WIDTH, kcb), prev(SWA_KV_WIDTH, vcb), cur(SWA_KV_WIDTH, vcb)],
        out_specs=cur(SWA_WIDTH, 0),
        out_shape=jax.ShapeDtypeStruct((batch * seq, SWA_WIDTH), BF),
        compiler_params=_params("arbitrary", "arbitrary"),
        name="swa_prompt",
    )(sinks, z, z, z, z, z)


def _swa_sample_kernel(q_ref, kt_ref, vt_ref, kn_ref, vn_ref, knc_ref, vnc_ref, slope_ref, sink_ref, *rest, l, nreq):
    o_ref, nk_ref, nv_ref = rest[-3:]
    kvw = SWA_KV_WIDTH
    hrow = lax.broadcasted_iota(jnp.int32, (SWA_HEADS, kvw), 0) // SWA_GROUP
    lblk = lax.broadcasted_iota(jnp.int32, (SWA_HEADS, kvw), 1) // SWA_HD
    own = hrow == lblk
    kvh = lax.broadcasted_iota(jnp.int32, (SWA_HEADS, SWA_HD), 0) // SWA_GROUP
    c = lax.broadcasted_iota(jnp.int32, (SWA_HEADS, WINDOW), 1)
    bias = slope_ref[...] * (WINDOW - c).astype(F32)
    sink = sink_ref[:, 0:1]
    pos = lax.broadcasted_iota(jnp.int32, (kvw, WINDOW), 1)
    scale = SWA_HD ** -0.5
    for r in range(nreq):
        q = q_ref[r]
        qm = jnp.where(own, jnp.concatenate([q] * SWA_KV, axis=1), 0.0).astype(BF)
        kt = kt_ref[r].reshape(kvw, WINDOW)
        vt = vt_ref[r].reshape(kvw, WINDOW)
        kn = kn_ref[r:r + 1, :].astype(BF).astype(F32)
        vn = vn_ref[r:r + 1, :].astype(BF).astype(F32)
        s_c = _dot(qm, kt.astype(BF)) * scale - bias
        s_n = jnp.sum(qm.astype(F32) * kn, axis=-1, keepdims=True) * scale
        mx = jnp.maximum(jnp.maximum(jnp.max(s_c, axis=-1, keepdims=True), s_n), sink)
        p_c = jnp.exp(s_c - mx)
        p_n = jnp.exp(s_n - mx)
        den = jnp.sum(p_c, axis=-1, keepdims=True) + p_n + jnp.exp(sink - mx)
        p_c = (p_c / den).astype(BF)
        p_n = (p_n / den).astype(BF).astype(F32)
        o_all = lax.dot_general(p_c, vt.astype(BF), NT_DIMS, preferred_element_type=F32) + p_n * vn
        o = jnp.zeros((SWA_HEADS, SWA_HD), F32)
        for j in range(SWA_KV):
            o = jnp.where(kvh == j, o_all[:, j * SWA_HD:(j + 1) * SWA_HD], o)
        o_ref[r] = o
        last = pos == WINDOW - 1
        nk_ref[r] = jnp.where(last, knc_ref[:, r:r + 1], pltpu.roll(kt, WINDOW - 1, axis=1)).reshape(SWA_KV, SWA_HD, WINDOW)
        nv_ref[r] = jnp.where(last, vnc_ref[:, r:r + 1], pltpu.roll(vt, WINDOW - 1, axis=1)).reshape(SWA_KV, SWA_HD, WINDOW)


def _swa_sample(q3, kt, vt, zs, knc, vnc, slope_b, sink_b, prev_nk, prev_nv, l, nreq):
    m = q3.shape[0]
    cache = pl.BlockSpec((None, nreq, SWA_KV, SWA_HD, WINDOW), lambda i: (l, i, 0, 0, 0))
    newcol = pl.BlockSpec((None, SWA_KV_WIDTH, nreq), lambda i: (i, 0, 0))
    in_specs = [pl.BlockSpec((nreq, SWA_HEADS, SWA_HD), lambda i: (i, 0, 0)), cache, cache,
                pl.BlockSpec((nreq, SWA_KV_WIDTH), lambda i: (i, C_K // SWA_KV_WIDTH)),
                pl.BlockSpec((nreq, SWA_KV_WIDTH), lambda i: (i, C_VV // SWA_KV_WIDTH)),
                newcol, newcol, _resident((SWA_HEADS, LANES)),
                pl.BlockSpec((None, SWA_HEADS, LANES), lambda i: (l, 0, 0))]
    args = [q3, kt, vt, zs, zs, knc, vnc, slope_b, sink_b]
    aliases = {}
    if prev_nk is not None:
        in_specs += [pl.BlockSpec(memory_space=pl.ANY)] * 2
        aliases = {len(args): 1, len(args) + 1: 2}
        args += [prev_nk, prev_nv]
    return pl.pallas_call(
        functools.partial(_swa_sample_kernel, l=l, nreq=nreq),
        grid=(m // nreq,),
        in_specs=in_specs,
        out_specs=[pl.BlockSpec((nreq, SWA_HEADS, SWA_HD), lambda i: (i, 0, 0)), cache, cache],
        out_shape=[jax.ShapeDtypeStruct((m, SWA_HEADS, SWA_HD), F32),
                   jax.ShapeDtypeStruct(kt.shape, F32), jax.ShapeDtypeStruct(vt.shape, F32)],
        input_output_aliases=aliases,
        compiler_params=_params("arbitrary"),
        name="swa_sample",
    )(*args)


def _memkv_kernel(x_ref, g_ref, w_ref, o_ref, xn_ref, *, l):
    @pl.when(pl.program_id(0) == 0)
    def _():
        xn_ref[...] = _rms(x_ref[...], g_ref[l:l + 1, :]).astype(BF)

    o_ref[...] = _dot(xn_ref[...], w_ref[...].astype(BF))


def _memkv(mem, g, w, l, tn):
    m = mem.shape[0]
    return pl.pallas_call(
        functools.partial(_memkv_kernel, l=l),
        grid=(2 * MEM_WIDTH // tn,),
        in_specs=[_resident((m, D_MODEL)), _resident(g.shape),
                  pl.BlockSpec((None, D_MODEL, tn), lambda j: (l, 0, j))],
        out_specs=pl.BlockSpec((m, tn), lambda j: (0, j)),
        out_shape=jax.ShapeDtypeStruct((m, 2 * MEM_WIDTH), F32),
        scratch_shapes=[pltpu.VMEM((m, D_MODEL), BF)],
        compiler_params=_params("arbitrary"),
        name="mem_kv",
    )(mem, g, w)


def _mem_attn_kernel(qa_ref, qb_ref, mk_ref, mv_ref, o_ref):
    for h in range(MEM_HEADS):
        q_ref = qa_ref if h < MEM_HEADS // 2 else qb_ref
        qcols = slice((h % (MEM_HEADS // 2)) * MEM_HD, (h % (MEM_HEADS // 2) + 1) * MEM_HD)
        cols = slice(h * MEM_HD, (h + 1) * MEM_HD)
        s = lax.dot_general(q_ref[:, qcols].astype(BF), mk_ref[:, cols].astype(BF), NT_DIMS,
                            preferred_element_type=F32) * (MEM_HD ** -0.5)
        e = jnp.exp(s - jnp.max(s, axis=-1, keepdims=True))
        p = (e / jnp.sum(e, axis=-1, keepdims=True)).astype(BF)
        o_ref[:, cols] = _dot(p, mv_ref[:, cols].astype(BF)).astype(BF)


def _mem_attn(z, mkv, batch, seq, tq):
    nq = seq // tq
    half = MEM_WIDTH // 2
    qspec = lambda cb: pl.BlockSpec((tq, half), lambda b, i: (b * nq + i, cb))
    kvspec = lambda cb: pl.BlockSpec((MEM_LEN, MEM_WIDTH), lambda b, i: (b, cb))
    return pl.pallas_call(
        _mem_attn_kernel,
        grid=(batch, nq),
        in_specs=[qspec(C_M // half), qspec(C_M // half + 1), kvspec(0), kvspec(1)],
        out_specs=pl.BlockSpec((tq, MEM_WIDTH), lambda b, i: (b * nq + i, 0)),
        out_shape=jax.ShapeDtypeStruct((batch * seq, MEM_WIDTH), BF),
        compiler_params=_params("arbitrary", "arbitrary"),
        name="mem_attn_prompt",
    )(z, z, mkv, mkv)


def _mem_sample_kernel(q_ref, mk_ref, mv_ref, o_ref, *, nreq):
    nrow = MEM_LEN * MEM_HEADS
    own = (lax.broadcasted_iota(jnp.int32, (MEM_HEADS, nrow), 1) % MEM_HEADS
           == lax.broadcasted_iota(jnp.int32, (MEM_HEADS, nrow), 0))
    for r in range(nreq):
        s = lax.dot_general(q_ref[r].astype(BF), mk_ref[r].astype(BF), NT_DIMS,
                            preferred_element_type=F32) * (MEM_HD ** -0.5)
        s = jnp.where(own, s, -jnp.inf)
        e = jnp.exp(s - jnp.max(s, axis=-1, keepdims=True))
        p = (e / jnp.sum(e, axis=-1, keepdims=True)).astype(BF)
        o_ref[r] = _dot(p, mv_ref[r].astype(BF))


def _mem_sample(q3, mk, mv, l, nreq):
    m = q3.shape[0]
    qspec = pl.BlockSpec((nreq, MEM_HEADS, MEM_HD), lambda i: (i, 0, 0))
    kvspec = pl.BlockSpec((None, nreq, MEM_LEN * MEM_HEADS, MEM_HD), lambda i: (l, i, 0, 0))
    return pl.pallas_call(
        functools.partial(_mem_sample_kernel, nreq=nreq),
        grid=(m // nreq,),
        in_specs=[qspec, kvspec, kvspec],
        out_specs=qspec,
        out_shape=jax.ShapeDtypeStruct((m, MEM_HEADS, MEM_HD), F32),
        compiler_params=_params("arbitrary"),
        name="mem_attn_sample",
    )(q3, mk, mv)


def _merge_kernel(xn_ref, og_ref, oa_ref, om_ref, xns_ref, ogs_ref, oas_ref, oms_ref,
                  wg0_ref, wg1_ref, wg2_ref, wbg_ref, wba_ref, wbm_ref, out_ref, outs_ref,
                  bg0_ref, bg1_ref, bg2_ref, bbg_ref, bba_ref, bbm_ref):
    pairs = [(wg0_ref, bg0_ref), (wg1_ref, bg1_ref), (wg2_ref, bg2_ref),
             (wbg_ref, bbg_ref), (wba_ref, bba_ref), (wbm_ref, bbm_ref)]

    @pl.when(_first_inner())
    def _():
        for w_ref, b_ref in pairs:
            b_ref[...] = w_ref[...].astype(BF)

    def merged(xn, og, oa, om):
        def branch(bg_ref, o, bb_ref):
            return jax.nn.sigmoid(_dot(xn, bg_ref[...])) * _dot(o, bb_ref[...])
        return (branch(bg0_ref, og, bbg_ref) + branch(bg1_ref, oa, bba_ref) + branch(bg2_ref, om, bbm_ref)).astype(BF)

    out_ref[...] = merged(xn_ref[...], og_ref[...], oa_ref[...], om_ref[...])

    @pl.when(_last_inner())
    def _():
        outs_ref[...] = merged(xns_ref[...], ogs_ref[...], oas_ref[...], oms_ref[...])


def _merge(xn, og, oa, om, xns, ogs, oas, oms, w_in, wbg, wba, wbm, l, tm, tn):
    m, ms = xn.shape[0], xns.shape[0]
    nt = D_MODEL // tn
    row = lambda n: pl.BlockSpec((tm, n), lambda j, i: (i, 0))
    gate = lambda b: pl.BlockSpec((None, D_MODEL, tn), lambda j, i: (l, 0, C_G // tn + b * nt + j))
    col = lambda n: pl.BlockSpec((None, n, tn), lambda j, i: (l, 0, j))
    widths = (G_WIDTH, SWA_WIDTH, MEM_WIDTH)
    return pl.pallas_call(
        _merge_kernel,
        grid=(nt, m // tm),
        in_specs=[row(D_MODEL)] + [row(n) for n in widths]
                 + [_resident((ms, D_MODEL))] + [_resident((ms, n)) for n in widths]
                 + [gate(0), gate(1), gate(2)] + [col(n) for n in widths],
        out_specs=[pl.BlockSpec((tm, tn), lambda j, i: (i, j)), pl.BlockSpec((ms, tn), lambda j, i: (0, j))],
        out_shape=[jax.ShapeDtypeStruct((m, D_MODEL), BF), jax.ShapeDtypeStruct((ms, D_MODEL), BF)],
        scratch_shapes=[pltpu.VMEM((D_MODEL, tn), BF)] * 3 + [pltpu.VMEM((n, tn), BF) for n in widths],
        compiler_params=_params("arbitrary", "arbitrary"),
        name="merge",
    )(xn, og, oa, om, xns, ogs, oas, oms, w_in, w_in, w_in, wbg, wba, wbm)


def _out_kernel(x_ref, mg_ref, xs_ref, mgs_ref, w_ref, g_ref, xo_ref, xn_ref, xos_ref, xns_ref, wbf_ref, *, l):
    @pl.when(pl.program_id(0) == 0)
    def _():
        wbf_ref[...] = w_ref[...].astype(BF)

    g = g_ref[l:l + 1, :]
    x = x_ref[...] + _dot(mg_ref[...], wbf_ref[...])
    xo_ref[...] = x
    xn_ref[...] = _rms(x, g).astype(BF)

    @pl.when(pl.program_id(0) == pl.num_programs(0) - 1)
    def _():
        xs = xs_ref[...] + _dot(mgs_ref[...], wbf_ref[...])
        xos_ref[...] = xs
        xns_ref[...] = _rms(xs, g).astype(BF)


def _out_proj(x, merged, xs, mergeds, w_out, g, l, tm):
    m, ms = x.shape[0], xs.shape[0]
    row = pl.BlockSpec((tm, D_MODEL), lambda i: (i, 0))
    srow = pl.BlockSpec((ms, D_MODEL), lambda i: (0, 0))
    return pl.pallas_call(
        functools.partial(_out_kernel, l=l),
        grid=(m // tm,),
        in_specs=[row, row, _resident((ms, D_MODEL)), _resident((ms, D_MODEL)),
                  pl.BlockSpec((None, D_MODEL, D_MODEL), lambda i: (l, 0, 0), pipeline_mode=pl.Buffered(1)),
                  _resident(g.shape)],
        out_specs=[row, row, srow, srow],
        out_shape=[jax.ShapeDtypeStruct((m, D_MODEL), F32), jax.ShapeDtypeStruct((m, D_MODEL), BF),
                   jax.ShapeDtypeStruct((ms, D_MODEL), F32), jax.ShapeDtypeStruct((ms, D_MODEL), BF)],
        scratch_shapes=[pltpu.VMEM((D_MODEL, D_MODEL), BF)],
        compiler_params=_params("arbitrary"),
        name="out_proj",
    )(x, merged, xs, mergeds, w_out, g)


def _up_kernel(xn_ref, xns_ref, wa_ref, wb_ref, cwa_ref, cwb_ref, cba_ref, cbb_ref, sta_ref, stb_ref,
               g_ref, ca_ref, cb_ref, gs_ref, has_ref, hbs_ref, ha_ref, hb_ref, ba_ref, bb_ref,
               *, l, tm, tiles_per_seq):
    @pl.when(_first_inner())
    def _():
        ba_ref[...] = wa_ref[...].astype(BF)
        bb_ref[...] = wb_ref[...].astype(BF)

    @pl.when(pl.program_id(1) % tiles_per_seq == 0)
    def _():
        ha_ref[0:SUBLANES, :] = jnp.zeros((SUBLANES, ha_ref.shape[1]), F32)
        hb_ref[0:SUBLANES, :] = jnp.zeros((SUBLANES, hb_ref.shape[1]), F32)

    xn = xn_ref[...]

    def conv(bw_ref, h_ref, cw_ref, bias_ref, tail_ref):
        h_ref[SUBLANES:SUBLANES + tm, :] = _dot(xn, bw_ref[...])
        acc = bias_ref[l:l + 1, :]
        for j in range(CONV_W):
            lo = SUBLANES - (CONV_W - 1) + j
            acc = acc + h_ref[lo:lo + tm, :] * cw_ref[j:j + 1, :]
        tail = h_ref[tm:tm + SUBLANES, :]
        tail_ref[...] = tail
        h_ref[0:SUBLANES, :] = tail
        return acc

    a = conv(ba_ref, ha_ref, cwa_ref, cba_ref, ca_ref)
    b = conv(bb_ref, hb_ref, cwb_ref, cbb_ref, cb_ref)
    g_ref[...] = (jax.nn.gelu(a) * b).astype(BF)

    @pl.when(_last_inner())
    def _():
        xns = xns_ref[...]

        def conv_s(bw_ref, cw_ref, bias_ref, st_ref, h_out_ref):
            h = _dot(xns, bw_ref[...])
            h_out_ref[...] = h
            return (bias_ref[l:l + 1, :] + st_ref[:, 0, :] * cw_ref[0:1, :] + st_ref[:, 1, :] * cw_ref[1:2, :]
                    + h * cw_ref[2:3, :])

        a_s = conv_s(ba_ref, cwa_ref, cba_ref, sta_ref, has_ref)
        b_s = conv_s(bb_ref, cwb_ref, cbb_ref, stb_ref, hbs_ref)
        gs_ref[...] = (jax.nn.gelu(a_s) * b_s).astype(BF)


def _up_conv(xn, xns, w_up, conv_w, conv_b, state, l, batch, seq, tm, tn):
    m, ms = xn.shape[0], xns.shape[0]
    nt = D_FF // tn
    tiles_per_seq = seq // tm
    half = lambda off: (lambda j, i: (l, 0, off * nt + j))
    wspec = lambda off: pl.BlockSpec((None, D_MODEL, tn), half(off))
    cwspec = lambda off: pl.BlockSpec((None, CONV_W, tn), half(off))
    cbspec = lambda off: pl.BlockSpec((DEPTH, tn), lambda j, i: (0, off * nt + j))
    stspec = lambda off: pl.BlockSpec((None, ms, CONV_W - 1, tn), lambda j, i: (l, 0, 0, off * nt + j))
    tail = pl.BlockSpec((SUBLANES, tn), lambda j, i: (i // tiles_per_seq, j))
    scol = pl.BlockSpec((ms, tn), lambda j, i: (0, j))
    return pl.pallas_call(
        functools.partial(_up_kernel, l=l, tm=tm, tiles_per_seq=tiles_per_seq),
        grid=(nt, m // tm),
        in_specs=[pl.BlockSpec((tm, D_MODEL), lambda j, i: (i, 0)), _resident((ms, D_MODEL)),
                  wspec(0), wspec(1), cwspec(0), cwspec(1), cbspec(0), cbspec(1), stspec(0), stspec(1)],
        out_specs=[pl.BlockSpec((tm, tn), lambda j, i: (i, j)), tail, tail, scol, scol, scol],
        out_shape=[jax.ShapeDtypeStruct((m, D_FF), BF),
                   jax.ShapeDtypeStruct((batch * SUBLANES, D_FF), F32),
                   jax.ShapeDtypeStruct((batch * SUBLANES, D_FF), F32),
                   jax.ShapeDtypeStruct((ms, D_FF), BF),
                   jax.ShapeDtypeStruct((ms, D_FF), F32), jax.ShapeDtypeStruct((ms, D_FF), F32)],
        scratch_shapes=[pltpu.VMEM((tm + SUBLANES, tn), F32), pltpu.VMEM((tm + SUBLANES, tn), F32),
                        pltpu.VMEM((D_MODEL, tn), BF), pltpu.VMEM((D_MODEL, tn), BF)],
        compiler_params=_params("arbitrary", "arbitrary"),
        name="up_conv_glu",
    )(xn, xns, w_up, w_up, conv_w, conv_w, conv_b, conv_b, state, state)


def _down_kernel(x_ref, g_ref, xs_ref, gs_ref, w_ref, o_ref, os_ref, wbf_ref):
    @pl.when(_first_inner())
    def _():
        wbf_ref[...] = w_ref[...].astype(BF)

    o_ref[...] = x_ref[...] + _dot(g_ref[...], wbf_ref[...])

    @pl.when(_last_inner())
    def _():
        os_ref[...] = xs_ref[...] + _dot(gs_ref[...], wbf_ref[...])


def _down_proj(x, g, xs, gs, w_down, l, tm, tn):
    m, ms = x.shape[0], xs.shape[0]
    return pl.pallas_call(
        _down_kernel,
        grid=(D_MODEL // tn, m // tm),
        in_specs=[pl.BlockSpec((tm, tn), lambda j, i: (i, j)), pl.BlockSpec((tm, D_FF), lambda j, i: (i, 0)),
                  pl.BlockSpec((ms, tn), lambda j, i: (0, j)), _resident((ms, D_FF)),
                  pl.BlockSpec((None, D_FF, tn), lambda j, i: (l, 0, j))],
        out_specs=[pl.BlockSpec((tm, tn), lambda j, i: (i, j)), pl.BlockSpec((ms, tn), lambda j, i: (0, j))],
        out_shape=[jax.ShapeDtypeStruct((m, D_MODEL), F32), jax.ShapeDtypeStruct((ms, D_MODEL), F32)],
        scratch_shapes=[pltpu.VMEM((D_FF, tn), BF)],
        compiler_params=_params("arbitrary", "arbitrary"),
        name="down_proj",
    )(x, g, xs, gs, w_down)


def _layer(l, xp, xs, mem, batch, seq, kt, vt, mk4, mv4, state_conv, prev_nk, prev_nv, slope_b, sink_b,
           norm_mix_g, w_in, gmlp_norm_g, gmlp_ws, gmlp_bs, attn_sinks, mem_norm_g, w_mem_kv,
           w_br_g, w_br_a, w_br_m, w_out, norm_ffn_g, w_up, conv_w, conv_b, w_down):
    dec = xs.shape[0]
    nreq = 8
    xn, xns = _norm(xp, xs, norm_mix_g, l, BF, 512)
    z, zs = _zsmall(xn, xns, w_in, l, 2048)

    bs = gmlp_bs[l]
    bsb = jnp.repeat(bs.T, G_GDIM, axis=1)
    og, vg_last = _gmlp(z, gmlp_ws, gmlp_norm_g, bsb, l, batch, seq, 512)
    coef = jnp.stack([jnp.repeat(gmlp_ws[l, :, 0, 0], G_GDIM), jnp.repeat(bs[:, 0], G_GDIM)])
    ogs, vgs = _gmlp_sample(zs, gmlp_norm_g, coef, l)

    oa = _swa(attn_sinks, z, l, batch, seq)
    q3 = zs[:, C_Q:C_K].reshape(dec, SWA_HEADS, SWA_HD)
    newcol = lambda a, b: zs[:, a:b].reshape(dec // nreq, nreq, SWA_KV_WIDTH).transpose(0, 2, 1)
    oas, nk, nv = _swa_sample(q3, kt, vt, zs, newcol(C_K, C_VV), newcol(C_VV, C_M), slope_b, sink_b,
                              prev_nk, prev_nv, l, nreq)

    mkv = _memkv(mem, mem_norm_g, w_mem_kv, l, 256)
    om = _mem_attn(z, mkv, batch, seq, 512)
    oms = _mem_sample(zs[:, C_M:C_G].reshape(dec, MEM_HEADS, MEM_HD), mk4, mv4, l, nreq)

    merged, mergeds = _merge(xn, og, oa, om, xns, ogs, oas.reshape(dec, SWA_WIDTH).astype(BF),
                             oms.reshape(dec, MEM_WIDTH).astype(BF), w_in, w_br_g, w_br_a, w_br_m, l, 1024, 256)
    xp, xn2, xs, xn2s = _out_proj(xp, merged, xs, mergeds, w_out, norm_ffn_g, l, 256)
    g, ca, cb, gs, has, hbs = _up_conv(xn2, xn2s, w_up, conv_w, conv_b, state_conv, l, batch, seq, 1024, 512)
    xp, xs = _down_proj(xp, g, xs, gs, w_down, l, 512, 512)

    tail = lambda t: t.reshape(batch, SUBLANES, D_FF)[:, SUBLANES - (CONV_W - 1):]
    zb = z.reshape(batch, seq, C_G)[:, seq - WINDOW:]
    st = dict(
        pk=zb[:, :, C_K:C_VV].reshape(batch, WINDOW, SWA_KV, SWA_HD),
        pv=zb[:, :, C_VV:C_M].reshape(batch, WINDOW, SWA_KV, SWA_HD),
        mk=mkv[:, :MEM_WIDTH].reshape(batch, MEM_LEN, MEM_HEADS, MEM_HD),
        mv=mkv[:, MEM_WIDTH:].reshape(batch, MEM_LEN, MEM_HEADS, MEM_HD),
        gvp=vg_last.reshape(batch, CHUNK, G_WIDTH),
        gvs=vgs.reshape(dec, 1, G_WIDTH),
        cvp=jnp.concatenate([tail(ca), tail(cb)], axis=-1),
        cvs=jnp.concatenate([state_conv[l, :, 1:], jnp.concatenate([has, hbs], axis=-1)[:, None]], axis=1),
    )
    return xp, xs, nk, nv, st


def kernel(x_prompt, x_sample, cache_swa_k, cache_swa_v, cache_mem_k, cache_mem_v, state_conv, mem_prompt, norm_mix_g, w_in, gmlp_norm_g, gmlp_ws, gmlp_bs, attn_sinks, mem_norm_g, w_mem_kv, w_br_g, w_br_a, w_br_m, w_out, norm_ffn_g, w_up, conv_w, conv_b, w_down, final_norm_g):
    batch, seq, _ = x_prompt.shape
    dec = x_sample.shape[0]
    assert x_sample.shape[1] == 1 and PAST_LEN % CHUNK == 0 and PAST_LEN >= WINDOW
    assert seq % 2048 == 0 and seq >= WINDOW and (seq - 1) // CHUNK * CHUNK == seq - CHUNK
    xp = x_prompt.reshape(batch * seq, D_MODEL)
    xs = x_sample.reshape(dec, D_MODEL)
    mem = mem_prompt.reshape(batch * MEM_LEN, D_MODEL)
    kt = cache_swa_k.transpose(0, 1, 3, 4, 2)
    vt = cache_swa_v.transpose(0, 1, 3, 4, 2)
    mk4 = cache_mem_k.reshape(DEPTH, dec, MEM_LEN * MEM_HEADS, MEM_HD)
    mv4 = cache_mem_v.reshape(DEPTH, dec, MEM_LEN * MEM_HEADS, MEM_HD)
    slope_b = jnp.broadcast_to(jnp.asarray(SLOPES, F32)[:, None], (SWA_HEADS, LANES))
    sink_b = jnp.broadcast_to(attn_sinks[:, :, None], (DEPTH, SWA_HEADS, LANES))
    nk = nv = None
    sts = []
    for l in range(DEPTH):
        xp, xs, nk, nv, st = _layer(l, xp, xs, mem, batch, seq, kt, vt, mk4, mv4, state_conv, nk, nv, slope_b, sink_b,
                                    norm_mix_g, w_in, gmlp_norm_g, gmlp_ws, gmlp_bs, attn_sinks, mem_norm_g,
                                    w_mem_kv, w_br_g, w_br_a, w_br_m, w_out, norm_ffn_g, w_up, conv_w, conv_b, w_down)
        sts.append(st)
    y_prompt, y_sample = _norm(xp, xs, final_norm_g.reshape(1, D_MODEL), 0, F32, 512)
    stack = lambda key: jnp.stack([s[key] for s in sts])
    return (y_prompt.reshape(batch, seq, D_MODEL), y_sample.reshape(dec, 1, D_MODEL), stack("pk"), stack("pv"),
            nk.transpose(0, 1, 4, 2, 3), nv.transpose(0, 1, 4, 2, 3),
            stack("mk"), stack("mv"), stack("gvp"), stack("gvs"), stack("cvp"), stack("cvs"))
```

```python
import functools
import math

import jax
import jax.numpy as jnp
from jax import lax
from jax.experimental import pallas as pl
from jax.experimental.pallas import tpu as pltpu

D_MODEL = 2048
DEPTH = 2
PAST_LEN = 8192
MEM_LEN = 256
CHUNK = 128
G_GROUPS = 8
G_WIDTH = 768
G_GDIM = G_WIDTH // G_GROUPS
SWA_HEADS = 12
SWA_KV = 4
SWA_GROUP = SWA_HEADS // SWA_KV
SWA_HD = 64
SWA_WIDTH = SWA_HEADS * SWA_HD
SWA_KV_WIDTH = SWA_KV * SWA_HD
WINDOW = 128
MEM_HEADS = 4
MEM_HD = 128
MEM_WIDTH = MEM_HEADS * MEM_HD
N_BRANCH = 3
D_FF = 5632
CONV_W = 3
EPS = 1e-6

LANES = 128
SUBLANES = 8
VMEM_LIMIT = 56 * 1024 * 1024

C_U, C_V, C_Q, C_K, C_VV, C_M, C_G = 0, 768, 1536, 2304, 2560, 2816, 3328
ZT = 256
N_ACT = C_Q // ZT
assert all(c % ZT == 0 for c in (C_V, C_Q, C_K, C_VV, C_M, C_G))

BF = jnp.bfloat16
F32 = jnp.float32
NT_DIMS = (((1,), (1,)), ((), ()))


def _alibi_slopes(n):
    p = 2 ** int(math.floor(math.log2(n)))
    base = [2.0 ** (-8.0 * (i + 1) / p) for i in range(p)]
    extra = [2.0 ** (-8.0 * (2 * i + 1) / (2 * p)) for i in range(n - p)]
    return base + extra


SLOPES = _alibi_slopes(SWA_HEADS)


def _params(*sem):
    return pltpu.CompilerParams(dimension_semantics=sem, vmem_limit_bytes=VMEM_LIMIT)


def _resident(shape):
    return pl.BlockSpec(shape, lambda *_: (0,) * len(shape), pipeline_mode=pl.Buffered(1))


def _rms(x, g):
    ms = jnp.mean(x * x, axis=-1, keepdims=True)
    return x * lax.rsqrt(ms + EPS) * g


def _dot(a, b):
    return jnp.dot(a, b, preferred_element_type=F32)


def _first_inner():
    return pl.program_id(1) == 0


def _last_inner():
    return pl.program_id(1) == pl.num_programs(1) - 1


def _norm_kernel(x_ref, xs_ref, g_ref, o_ref, os_ref, *, l):
    g = g_ref[l:l + 1, :]
    o_ref[...] = _rms(x_ref[...], g).astype(o_ref.dtype)

    @pl.when(pl.program_id(0) == pl.num_programs(0) - 1)
    def _():
        os_ref[...] = _rms(xs_ref[...], g).astype(os_ref.dtype)


def _norm(x, xs, g, l, dtype, tm):
    m, ms = x.shape[0], xs.shape[0]
    row = pl.BlockSpec((tm, D_MODEL), lambda i: (i, 0))
    return pl.pallas_call(
        functools.partial(_norm_kernel, l=l),
        grid=(m // tm,),
        in_specs=[row, _resident((ms, D_MODEL)), _resident(g.shape)],
        out_specs=[row, pl.BlockSpec((ms, D_MODEL), lambda i: (0, 0))],
        out_shape=[jax.ShapeDtypeStruct((m, D_MODEL), dtype), jax.ShapeDtypeStruct((ms, D_MODEL), dtype)],
        compiler_params=_params("arbitrary"),
        name="rmsnorm",
    )(x, xs, g)


NZ = C_G // ZT


def _zsmall_kernel(xn_ref, xs_ref, w_ref, z_ref, zs_ref, wbf_ref):
    s = pl.program_id(0)

    @pl.when(s < NZ)
    def _():
        wbf_ref[s] = w_ref[...].astype(BF)

    def project(x, o_ref):
        for c in range(NZ):
            h = _dot(x, wbf_ref[c])
            o_ref[:, c * ZT:(c + 1) * ZT] = jax.nn.gelu(h) if c < N_ACT else h

    @pl.when(s >= NZ)
    def _():
        project(xn_ref[...], z_ref)

    @pl.when(s == pl.num_programs(0) - 1)
    def _():
        project(xs_ref[...], zs_ref)


def _zsmall(xn, xs, w_in, l, tm):
    m, ms = xn.shape[0], xs.shape[0]
    rows = lambda s: (jnp.maximum(s - NZ, 0), 0)
    return pl.pallas_call(
        _zsmall_kernel,
        grid=(NZ + m // tm,),
        in_specs=[pl.BlockSpec((tm, D_MODEL), rows), _resident((ms, D_MODEL)),
                  pl.BlockSpec((None, D_MODEL, ZT), lambda s: (l, 0, jnp.minimum(s, NZ - 1)))],
        out_specs=[pl.BlockSpec((tm, C_G), rows), pl.BlockSpec((ms, C_G), lambda s: (0, 0))],
        out_shape=[jax.ShapeDtypeStruct((m, C_G), F32), jax.ShapeDtypeStruct((ms, C_G), F32)],
        scratch_shapes=[pltpu.VMEM((NZ, D_MODEL, ZT), BF)],
        compiler_params=_params("arbitrary"),
        name="in_proj",
    )(xn, xs, w_in)


G_TILES = G_WIDTH // LANES
GA = [(t * LANES) // G_GDIM for t in range(G_TILES)]
BND = [(GA[t] + 1) * G_GDIM - t * LANES for t in range(G_TILES)]
assert all(0 < b < LANES and (GA[t] + 2) * G_GDIM >= (t + 1) * LANES for t, b in enumerate(BND))


def _gmlp_kernel(u_ref, gv_ref, ws_ref, gg_ref, bsb_ref, og_ref, vg_ref, *, l, nchunk):
    r = lax.broadcasted_iota(jnp.int32, (CHUNK, CHUNK), 0)
    c = lax.broadcasted_iota(jnp.int32, (CHUNK, CHUNK), 1)
    tri = r >= c
    wtri = [jnp.where(tri, ws_ref[g], 0.0).astype(BF) for g in range(G_GROUPS)]
    wst = [jnp.concatenate([wtri[GA[t]], wtri[GA[t] + 1]], axis=0) for t in range(G_TILES)]
    lane = lax.broadcasted_iota(jnp.int32, (CHUNK, LANES), 1)
    g = gg_ref[l:l + 1, :]
    for n in range(nchunk):
        rows = slice(n * CHUNK, (n + 1) * CHUNK)
        vg = _rms(gv_ref[rows, :], g)
        if n == nchunk - 1:
            vg_ref[...] = vg
        for t in range(G_TILES):
            cols = slice(t * LANES, (t + 1) * LANES)
            both = _dot(wst[t], vg[:, cols].astype(BF))
            mix = jnp.where(lane < BND[t], both[:CHUNK], both[CHUNK:]) + bsb_ref[:, cols]
            og_ref[rows, cols] = (u_ref[rows, cols] * mix).astype(BF)


def _gmlp(z, ws, gg, bsb, l, batch, seq, tr):
    nr = seq // tr
    zcol = lambda cb: pl.BlockSpec((tr, G_WIDTH), lambda b, i: (b * nr + i, cb))
    return pl.pallas_call(
        functools.partial(_gmlp_kernel, l=l, nchunk=tr // CHUNK),
        grid=(batch, nr),
        in_specs=[zcol(C_U // G_WIDTH), zcol(C_V // G_WIDTH),
                  pl.BlockSpec((None, G_GROUPS, CHUNK, CHUNK), lambda b, i: (l, 0, 0, 0)),
                  _resident(gg.shape), _resident((CHUNK, G_WIDTH))],
        out_specs=[pl.BlockSpec((tr, G_WIDTH), lambda b, i: (b * nr + i, 0)),
                   pl.BlockSpec((CHUNK, G_WIDTH), lambda b, i: (b, 0))],
        out_shape=[jax.ShapeDtypeStruct((batch * seq, G_WIDTH), BF),
                   jax.ShapeDtypeStruct((batch * CHUNK, G_WIDTH), F32)],
        compiler_params=_params("arbitrary", "arbitrary"),
        name="gmlp_spatial",
    )(z, z, ws, gg, bsb)


def _gmlp_sample_kernel(zs_ref, gg_ref, coef_ref, og_ref, vg_ref, *, l):
    vg = _rms(zs_ref[:, C_V:C_Q], gg_ref[l:l + 1, :])
    vg_ref[...] = vg
    og_ref[...] = (zs_ref[:, C_U:C_V] * (coef_ref[0:1, :] * vg + coef_ref[1:2, :])).astype(BF)


def _gmlp_sample(zs, gg, coef, l):
    m = zs.shape[0]
    return pl.pallas_call(
        functools.partial(_gmlp_sample_kernel, l=l),
        out_shape=[jax.ShapeDtypeStruct((m, G_WIDTH), BF), jax.ShapeDtypeStruct((m, G_WIDTH), F32)],
        name="gmlp_sample",
    )(zs, gg, coef)


def _swa_kernel(sink_ref, q_ref, kp_ref, kc_ref, vp_ref, vc_ref, o_ref, bias_ref, *, l):
    n = pl.program_id(1)
    rows = SWA_GROUP * WINDOW

    @pl.when((pl.program_id(0) == 0) & (n == 0))
    def _():
        r = lax.broadcasted_iota(jnp.int32, (WINDOW, 2 * WINDOW), 0)
        c = lax.broadcasted_iota(jnp.int32, (WINDOW, 2 * WINDOW), 1)
        dist = r + WINDOW - c
        valid = (dist >= 0) & (dist <= WINDOW)
        distf = dist.astype(F32)
        for h in range(SWA_HEADS):
            j, g = divmod(h, SWA_GROUP)
            pen = -SLOPES[h] * distf
            bias_ref[j, g * WINDOW:(g + 1) * WINDOW, :] = jnp.where(valid, pen, -jnp.inf)
            bias_ref[SWA_KV + j, g * WINDOW:(g + 1) * WINDOW, :] = jnp.where(valid & (c >= WINDOW), pen, -jnp.inf)

    first = (n == 0).astype(jnp.int32) * SWA_KV
    kcat = jnp.concatenate([kp_ref[...], kc_ref[...]], axis=0).astype(BF)
    vcat = jnp.concatenate([vp_ref[...], vc_ref[...]], axis=0).astype(BF)
    head = lax.broadcasted_iota(jnp.int32, (rows, 1), 0) // WINDOW
    for j in range(SWA_KV):
        kj = kcat[:, j * SWA_HD:(j + 1) * SWA_HD]
        vj = vcat[:, j * SWA_HD:(j + 1) * SWA_HD]
        hs = range(j * SWA_GROUP, (j + 1) * SWA_GROUP)
        q = jnp.concatenate([q_ref[:, h * SWA_HD:(h + 1) * SWA_HD] for h in hs], axis=0).astype(BF)
        sink = jnp.full((rows, 1), sink_ref[l, hs[-1]], F32)
        for g in range(SWA_GROUP - 1):
            sink = jnp.where(head == g, sink_ref[l, hs[g]], sink)
        s = lax.dot_general(q, kj, NT_DIMS, preferred_element_type=F32) * (SWA_HD ** -0.5) + bias_ref[first + j]
        mx = jnp.maximum(jnp.max(s, axis=-1, keepdims=True), sink)
        p = jnp.exp(s - mx)
        den = jnp.sum(p, axis=-1, keepdims=True) + jnp.exp(sink - mx)
        o = _dot(p.astype(BF), vj) / den
        for g, h in enumerate(hs):
            o_ref[:, h * SWA_HD:(h + 1) * SWA_HD] = o[g * WINDOW:(g + 1) * WINDOW].astype(BF)


def _swa(sinks, z, l, batch, seq):
    nb = seq // WINDOW
    cur = lambda n, cb: pl.BlockSpec((WINDOW, n), lambda b, i: (b * nb + i, cb))
    prev = lambda n, cb: pl.BlockSpec((WINDOW, n), lambda b, i: (b * nb + jnp.maximum(i - 1, 0), cb))
    kcb, vcb = C_K // SWA_KV_WIDTH, C_VV // SWA_KV_WIDTH
    return pl.pallas_call(
        functools.partial(_swa_kernel, l=l),
        grid=(batch, nb),
        in_specs=[pl.BlockSpec(memory_space=pltpu.SMEM), cur(SWA_WIDTH, C_Q // SWA_WIDTH),
                  prev(SWA_KV_WIDTH, kcb), cur(SWA_KV_WIDTH, kcb), prev(SWA_KV_WIDTH, vcb), cur(SWA_KV_WIDTH, vcb)],
        out_specs=cur(SWA_WIDTH, 0),
        out_shape=jax.ShapeDtypeStruct((batch * seq, SWA_WIDTH), BF),
        scratch_shapes=[pltpu.VMEM((2 * SWA_KV, SWA_GROUP * WINDOW, 2 * WINDOW), F32)],
        compiler_params=_params("arbitrary", "arbitrary"),
        name="swa_prompt",
    )(sinks, z, z, z, z, z)


def _swa_sample_kernel(q_ref, kt_ref, vt_ref, kn_ref, vn_ref, knc_ref, vnc_ref, slope_ref, sink_ref, *rest, l, nreq):
    o_ref, nk_ref, nv_ref = rest[-3:]
    kvw = SWA_KV_WIDTH
    hrow = lax.broadcasted_iota(jnp.int32, (SWA_HEADS, kvw), 0) // SWA_GROUP
    lblk = lax.broadcasted_iota(jnp.int32, (SWA_HEADS, kvw), 1) // SWA_HD
    own = hrow == lblk
    kvh = lax.broadcasted_iota(jnp.int32, (SWA_HEADS, SWA_HD), 0) // SWA_GROUP
    c = lax.broadcasted_iota(jnp.int32, (SWA_HEADS, WINDOW), 1)
    bias = slope_ref[...] * (WINDOW - c).astype(F32)
    sink = sink_ref[:, 0:1]
    pos = lax.broadcasted_iota(jnp.int32, (kvw, WINDOW), 1)
    scale = SWA_HD ** -0.5
    for r in range(nreq):
        q = q_ref[r]
        qm = jnp.where(own, jnp.concatenate([q] * SWA_KV, axis=1), 0.0).astype(BF)
        kt = kt_ref[r].reshape(kvw, WINDOW)
        vt = vt_ref[r].reshape(kvw, WINDOW)
        kn = kn_ref[r:r + 1, :].astype(BF).astype(F32)
        vn = vn_ref[r:r + 1, :].astype(BF).astype(F32)
        s_c = _dot(qm, kt.astype(BF)) * scale - bias
        s_n = jnp.sum(qm.astype(F32) * kn, axis=-1, keepdims=True) * scale
        mx = jnp.maximum(jnp.maximum(jnp.max(s_c, axis=-1, keepdims=True), s_n), sink)
        p_c = jnp.exp(s_c - mx)
        p_n = jnp.exp(s_n - mx)
        den = jnp.sum(p_c, axis=-1, keepdims=True) + p_n + jnp.exp(sink - mx)
        p_c = (p_c / den).astype(BF)
        p_n = (p_n / den).astype(BF).astype(F32)
        o_all = lax.dot_general(p_c, vt.astype(BF), NT_DIMS, preferred_element_type=F32) + p_n * vn
        o = jnp.zeros((SWA_HEADS, SWA_HD), F32)
        for j in range(SWA_KV):
            o = jnp.where(kvh == j, o_all[:, j * SWA_HD:(j + 1) * SWA_HD], o)
        o_ref[r] = o
        last = pos == WINDOW - 1
        nk_ref[r] = jnp.where(last, knc_ref[:, r:r + 1], pltpu.roll(kt, WINDOW - 1, axis=1)).reshape(SWA_KV, SWA_HD, WINDOW)
        nv_ref[r] = jnp.where(last, vnc_ref[:, r:r + 1], pltpu.roll(vt, WINDOW - 1, axis=1)).reshape(SWA_KV, SWA_HD, WINDOW)


def _swa_sample(q3, kt, vt, zs, knc, vnc, slope_b, sink_b, prev_nk, prev_nv, l, nreq):
    m = q3.shape[0]
    cache = pl.BlockSpec((None, nreq, SWA_KV, SWA_HD, WINDOW), lambda i: (l, i, 0, 0, 0))
    newcol = pl.BlockSpec((None, SWA_KV_WIDTH, nreq), lambda i: (i, 0, 0))
    in_specs = [pl.BlockSpec((nreq, SWA_HEADS, SWA_HD), lambda i: (i, 0, 0)), cache, cache,
                pl.BlockSpec((nreq, SWA_KV_WIDTH), lambda i: (i, C_K // SWA_KV_WIDTH)),
                pl.BlockSpec((nreq, SWA_KV_WIDTH), lambda i: (i, C_VV // SWA_KV_WIDTH)),
                newcol, newcol, _resident((SWA_HEADS, LANES)),
                pl.BlockSpec((None, SWA_HEADS, LANES), lambda i: (l, 0, 0))]
    args = [q3, kt, vt, zs, zs, knc, vnc, slope_b, sink_b]
    aliases = {}
    if prev_nk is not None:
        in_specs += [pl.BlockSpec(memory_space=pl.ANY)] * 2
        aliases = {len(args): 1, len(args) + 1: 2}
        args += [prev_nk, prev_nv]
    return pl.pallas_call(
        functools.partial(_swa_sample_kernel, l=l, nreq=nreq),
        grid=(m // nreq,),
        in_specs=in_specs,
        out_specs=[pl.BlockSpec((nreq, SWA_HEADS, SWA_HD), lambda i: (i, 0, 0)), cache, cache],
        out_shape=[jax.ShapeDtypeStruct((m, SWA_HEADS, SWA_HD), F32),
                   jax.ShapeDtypeStruct(kt.shape, F32), jax.ShapeDtypeStruct(vt.shape, F32)],
        input_output_aliases=aliases,
        compiler_params=_params("arbitrary"),
        name="swa_sample",
    )(*args)


def _memkv_kernel(x_ref, g_ref, w_ref, o_ref, xn_ref, *, l):
    @pl.when(pl.program_id(0) == 0)
    def _():
        xn_ref[...] = _rms(x_ref[...], g_ref[l:l + 1, :]).astype(BF)

    o_ref[...] = _dot(xn_ref[...], w_ref[...].astype(BF))


def _memkv(mem, g, w, l, tn):
    m = mem.shape[0]
    return pl.pallas_call(
        functools.partial(_memkv_kernel, l=l),
        grid=(2 * MEM_WIDTH // tn,),
        in_specs=[_resident((m, D_MODEL)), _resident(g.shape),
                  pl.BlockSpec((None, D_MODEL, tn), lambda j: (l, 0, j))],
        out_specs=pl.BlockSpec((m, tn), lambda j: (0, j)),
        out_shape=jax.ShapeDtypeStruct((m, 2 * MEM_WIDTH), F32),
        scratch_shapes=[pltpu.VMEM((m, D_MODEL), BF)],
        compiler_params=_params("arbitrary"),
        name="mem_kv",
    )(mem, g, w)


def _mem_attn_kernel(qa_ref, qb_ref, mk_ref, mv_ref, o_ref):
    for h in range(MEM_HEADS):
        q_ref = qa_ref if h < MEM_HEADS // 2 else qb_ref
        qcols = slice((h % (MEM_HEADS // 2)) * MEM_HD, (h % (MEM_HEADS // 2) + 1) * MEM_HD)
        cols = slice(h * MEM_HD, (h + 1) * MEM_HD)
        s = lax.dot_general(q_ref[:, qcols].astype(BF), mk_ref[:, cols].astype(BF), NT_DIMS,
                            preferred_element_type=F32) * (MEM_HD ** -0.5)
        e = jnp.exp(s - jnp.max(s, axis=-1, keepdims=True))
        p = (e / jnp.sum(e, axis=-1, keepdims=True)).astype(BF)
        o_ref[:, cols] = _dot(p, mv_ref[:, cols].astype(BF)).astype(BF)


def _mem_attn(z, mkv, batch, seq, tq):
    nq = seq // tq
    half = MEM_WIDTH // 2
    qspec = lambda cb: pl.BlockSpec((tq, half), lambda b, i: (b * nq + i, cb))
    kvspec = lambda cb: pl.BlockSpec((MEM_LEN, MEM_WIDTH), lambda b, i: (b, cb))
    return pl.pallas_call(
        _mem_attn_kernel,
        grid=(batch, nq),
        in_specs=[qspec(C_M // half), qspec(C_M // half + 1), kvspec(0), kvspec(1)],
        out_specs=pl.BlockSpec((tq, MEM_WIDTH), lambda b, i: (b * nq + i, 0)),
        out_shape=jax.ShapeDtypeStruct((batch * seq, MEM_WIDTH), BF),
        compiler_params=_params("arbitrary", "arbitrary"),
        name="mem_attn_prompt",
    )(z, z, mkv, mkv)


def _mem_sample_kernel(q_ref, mk_ref, mv_ref, o_ref, *, nreq):
    nrow = MEM_LEN * MEM_HEADS
    own = (lax.broadcasted_iota(jnp.int32, (MEM_HEADS, nrow), 1) % MEM_HEADS
           == lax.broadcasted_iota(jnp.int32, (MEM_HEADS, nrow), 0))
    for r in range(nreq):
        s = lax.dot_general(q_ref[r].astype(BF), mk_ref[r].astype(BF), NT_DIMS,
                            preferred_element_type=F32) * (MEM_HD ** -0.5)
        s = jnp.where(own, s, -jnp.inf)
        e = jnp.exp(s - jnp.max(s, axis=-1, keepdims=True))
        p = (e / jnp.sum(e, axis=-1, keepdims=True)).astype(BF)
        o_ref[r] = _dot(p, mv_ref[r].astype(BF))


def _mem_sample(q3, mk, mv, l, nreq):
    m = q3.shape[0]
    qspec = pl.BlockSpec((nreq, MEM_HEADS, MEM_HD), lambda i: (i, 0, 0))
    kvspec = pl.BlockSpec((None, nreq, MEM_LEN * MEM_HEADS, MEM_HD), lambda i: (l, i, 0, 0))
    return pl.pallas_call(
        functools.partial(_mem_sample_kernel, nreq=nreq),
        grid=(m // nreq,),
        in_specs=[qspec, kvspec, kvspec],
        out_specs=qspec,
        out_shape=jax.ShapeDtypeStruct((m, MEM_HEADS, MEM_HD), F32),
        compiler_params=_params("arbitrary"),
        name="mem_attn_sample",
    )(q3, mk, mv)


def _merge_kernel(xn_ref, og_ref, oa_ref, om_ref, xns_ref, ogs_ref, oas_ref, oms_ref,
                  wg0_ref, wg1_ref, wg2_ref, wbg_ref, wba_ref, wbm_ref, out_ref, outs_ref,
                  bg0_ref, bg1_ref, bg2_ref, bbg_ref, bba_ref, bbm_ref):
    pairs = [(wg0_ref, bg0_ref), (wg1_ref, bg1_ref), (wg2_ref, bg2_ref),
             (wbg_ref, bbg_ref), (wba_ref, bba_ref), (wbm_ref, bbm_ref)]

    @pl.when(_first_inner())
    def _():
        for w_ref, b_ref in pairs:
            b_ref[...] = w_ref[...].astype(BF)

    def merged(xn, og, oa, om):
        def branch(bg_ref, o, bb_ref):
            return jax.nn.sigmoid(_dot(xn, bg_ref[...])) * _dot(o, bb_ref[...])
        return (branch(bg0_ref, og, bbg_ref) + branch(bg1_ref, oa, bba_ref) + branch(bg2_ref, om, bbm_ref)).astype(BF)

    out_ref[...] = merged(xn_ref[...], og_ref[...], oa_ref[...], om_ref[...])

    @pl.when(_last_inner())
    def _():
        outs_ref[...] = merged(xns_ref[...], ogs_ref[...], oas_ref[...], oms_ref[...])


def _merge(xn, og, oa, om, xns, ogs, oas, oms, w_in, wbg, wba, wbm, l, tm, tn):
    m, ms = xn.shape[0], xns.shape[0]
    nt = D_MODEL // tn
    row = lambda n: pl.BlockSpec((tm, n), lambda j, i: (i, 0))
    gate = lambda b: pl.BlockSpec((None, D_MODEL, tn), lambda j, i: (l, 0, C_G // tn + b * nt + j))
    col = lambda n: pl.BlockSpec((None, n, tn), lambda j, i: (l, 0, j))
    widths = (G_WIDTH, SWA_WIDTH, MEM_WIDTH)
    return pl.pallas_call(
        _merge_kernel,
        grid=(nt, m // tm),
        in_specs=[row(D_MODEL)] + [row(n) for n in widths]
                 + [_resident((ms, D_MODEL))] + [_resident((ms, n)) for n in widths]
                 + [gate(0), gate(1), gate(2)] + [col(n) for n in widths],
        out_specs=[pl.BlockSpec((tm, tn), lambda j, i: (i, j)), pl.BlockSpec((ms, tn), lambda j, i: (0, j))],
        out_shape=[jax.ShapeDtypeStruct((m, D_MODEL), BF), jax.ShapeDtypeStruct((ms, D_MODEL), BF)],
        scratch_shapes=[pltpu.VMEM((D_MODEL, tn), BF)] * 3 + [pltpu.VMEM((n, tn), BF) for n in widths],
        compiler_params=_params("arbitrary", "arbitrary"),
        name="merge",
    )(xn, og, oa, om, xns, ogs, oas, oms, w_in, w_in, w_in, wbg, wba, wbm)


def _out_kernel(x_ref, mg_ref, xs_ref, mgs_ref, w_ref, g_ref, xo_ref, xn_ref, xos_ref, xns_ref, wbf_ref, *, l):
    @pl.when(pl.program_id(0) == 0)
    def _():
        wbf_ref[...] = w_ref[...].astype(BF)

    g = g_ref[l:l + 1, :]
    x = x_ref[...] + _dot(mg_ref[...], wbf_ref[...])
    xo_ref[...] = x
    xn_ref[...] = _rms(x, g).astype(BF)

    @pl.when(pl.program_id(0) == pl.num_programs(0) - 1)
    def _():
        xs = xs_ref[...] + _dot(mgs_ref[...], wbf_ref[...])
        xos_ref[...] = xs
        xns_ref[...] = _rms(xs, g).astype(BF)


def _out_proj(x, merged, xs, mergeds, w_out, g, l, tm):
    m, ms = x.shape[0], xs.shape[0]
    row = pl.BlockSpec((tm, D_MODEL), lambda i: (i, 0))
    srow = pl.BlockSpec((ms, D_MODEL), lambda i: (0, 0))
    return pl.pallas_call(
        functools.partial(_out_kernel, l=l),
        grid=(m // tm,),
        in_specs=[row, row, _resident((ms, D_MODEL)), _resident((ms, D_MODEL)),
                  pl.BlockSpec((None, D_MODEL, D_MODEL), lambda i: (l, 0, 0), pipeline_mode=pl.Buffered(1)),
                  _resident(g.shape)],
        out_specs=[row, row, srow, srow],
        out_shape=[jax.ShapeDtypeStruct((m, D_MODEL), F32), jax.ShapeDtypeStruct((m, D_MODEL), BF),
                   jax.ShapeDtypeStruct((ms, D_MODEL), F32), jax.ShapeDtypeStruct((ms, D_MODEL), BF)],
        scratch_shapes=[pltpu.VMEM((D_MODEL, D_MODEL), BF)],
        compiler_params=_params("arbitrary"),
        name="out_proj",
    )(x, merged, xs, mergeds, w_out, g)


def _up_kernel(xn_ref, xns_ref, wa_ref, wb_ref, cwa_ref, cwb_ref, cba_ref, cbb_ref, sta_ref, stb_ref,
               g_ref, ca_ref, cb_ref, gs_ref, has_ref, hbs_ref, ha_ref, hb_ref, ba_ref, bb_ref,
               *, l, tm, tiles_per_seq):
    @pl.when(_first_inner())
    def _():
        ba_ref[...] = wa_ref[...].astype(BF)
        bb_ref[...] = wb_ref[...].astype(BF)

    @pl.when(pl.program_id(1) % tiles_per_seq == 0)
    def _():
        ha_ref[...] = jnp.zeros(ha_ref.shape, F32)
        hb_ref[...] = jnp.zeros(hb_ref.shape, F32)

    top = lax.broadcasted_iota(jnp.int32, (SUBLANES, ha_ref.shape[1]), 0)
    xn = xn_ref[...]

    def conv(h, prev_ref, cw_ref, bias_ref, tail_ref):
        prev = prev_ref[...]
        acc = bias_ref[l:l + 1, :]
        for j in range(CONV_W - 1):
            back = CONV_W - 1 - j
            rolled = pltpu.roll(h, back, axis=0)
            head = jnp.where(top < back, pltpu.roll(prev, back, axis=0), rolled[0:SUBLANES])
            acc = acc + jnp.concatenate([head, rolled[SUBLANES:]], axis=0) * cw_ref[j:j + 1, :]
        tail = h[tm - SUBLANES:]
        tail_ref[...] = tail
        prev_ref[...] = tail
        return acc + h * cw_ref[CONV_W - 1:CONV_W, :]

    a = conv(_dot(xn, ba_ref[...]), ha_ref, cwa_ref, cba_ref, ca_ref)
    b = conv(_dot(xn, bb_ref[...]), hb_ref, cwb_ref, cbb_ref, cb_ref)
    g_ref[...] = (jax.nn.gelu(a) * b).astype(BF)

    @pl.when(_last_inner())
    def _():
        xns = xns_ref[...]

        def conv_s(bw_ref, cw_ref, bias_ref, st_ref, h_out_ref):
            h = _dot(xns, bw_ref[...])
            h_out_ref[...] = h
            return (bias_ref[l:l + 1, :] + st_ref[:, 0, :] * cw_ref[0:1, :] + st_ref[:, 1, :] * cw_ref[1:2, :]
                    + h * cw_ref[2:3, :])

        a_s = conv_s(ba_ref, cwa_ref, cba_ref, sta_ref, has_ref)
        b_s = conv_s(bb_ref, cwb_ref, cbb_ref, stb_ref, hbs_ref)
        gs_ref[...] = (jax.nn.gelu(a_s) * b_s).astype(BF)


def _up_conv(xn, xns, w_up, conv_w, conv_b, state, l, batch, seq, tm, tn):
    m, ms = xn.shape[0], xns.shape[0]
    nt = D_FF // tn
    tiles_per_seq = seq // tm
    half = lambda off: (lambda j, i: (l, 0, off * nt + j))
    wspec = lambda off: pl.BlockSpec((None, D_MODEL, tn), half(off))
    cwspec = lambda off: pl.BlockSpec((None, CONV_W, tn), half(off))
    cbspec = lambda off: pl.BlockSpec((DEPTH, tn), lambda j, i: (0, off * nt + j))
    stspec = lambda off: pl.BlockSpec((None, ms, CONV_W - 1, tn), lambda j, i: (l, 0, 0, off * nt + j))
    tail = pl.BlockSpec((SUBLANES, tn), lambda j, i: (i // tiles_per_seq, j))
    scol = pl.BlockSpec((ms, tn), lambda j, i: (0, j))
    return pl.pallas_call(
        functools.partial(_up_kernel, l=l, tm=tm, tiles_per_seq=tiles_per_seq),
        grid=(nt, m // tm),
        in_specs=[pl.BlockSpec((tm, D_MODEL), lambda j, i: (i, 0)), _resident((ms, D_MODEL)),
                  wspec(0), wspec(1), cwspec(0), cwspec(1), cbspec(0), cbspec(1), stspec(0), stspec(1)],
        out_specs=[pl.BlockSpec((tm, tn), lambda j, i: (i, j)), tail, tail, scol, scol, scol],
        out_shape=[jax.ShapeDtypeStruct((m, D_FF), BF),
                   jax.ShapeDtypeStruct((batch * SUBLANES, D_FF), F32),
                   jax.ShapeDtypeStruct((batch * SUBLANES, D_FF), F32),
                   jax.ShapeDtypeStruct((ms, D_FF), BF),
                   jax.ShapeDtypeStruct((ms, D_FF), F32), jax.ShapeDtypeStruct((ms, D_FF), F32)],
        scratch_shapes=[pltpu.VMEM((SUBLANES, tn), F32), pltpu.VMEM((SUBLANES, tn), F32),
                        pltpu.VMEM((D_MODEL, tn), BF), pltpu.VMEM((D_MODEL, tn), BF)],
        compiler_params=_params("arbitrary", "arbitrary"),
        name="up_conv_glu",
    )(xn, xns, w_up, w_up, conv_w, conv_w, conv_b, conv_b, state, state)


def _down_kernel(x_ref, g_ref, xs_ref, gs_ref, w_ref, gain_ref, *rest, tk, ncast, emit_x):
    wbf_ref = rest[-1]
    outs = rest[:-1]
    s = pl.program_id(0)

    @pl.when(s < ncast)
    def _():
        wbf_ref[pl.ds(pl.multiple_of(s * tk, tk), tk), :] = w_ref[...].astype(BF)

    def tile(x, g, o_refs):
        x = x + _dot(g, wbf_ref[...])
        if emit_x:
            o_refs[0][...] = x
        o_refs[-1][...] = _rms(x, gain_ref[...]).astype(o_refs[-1].dtype)

    nout = len(outs) // 2

    @pl.when(s >= ncast)
    def _():
        tile(x_ref[...], g_ref[...], outs[:nout])

    @pl.when(s == pl.num_programs(0) - 1)
    def _():
        tile(xs_ref[...], gs_ref[...], outs[nout:])


def _down_proj(x, g, xs, gs, w_down, gain, l, tm, tk, norm_dtype, emit_x):
    m, ms = x.shape[0], xs.shape[0]
    ncast = D_FF // tk
    rows = lambda n: pl.BlockSpec((tm, n), lambda s: (jnp.maximum(s - ncast, 0), 0))
    fixed = lambda n: pl.BlockSpec((ms, n), lambda s: (0, 0))
    dts = ([F32] if emit_x else []) + [norm_dtype]
    return pl.pallas_call(
        functools.partial(_down_kernel, tk=tk, ncast=ncast, emit_x=emit_x),
        grid=(ncast + m // tm,),
        in_specs=[rows(D_MODEL), rows(D_FF), _resident((ms, D_MODEL)), _resident((ms, D_FF)),
                  pl.BlockSpec((None, tk, D_MODEL), lambda s: (l, jnp.minimum(s, ncast - 1), 0)),
                  _resident((1, D_MODEL))],
        out_specs=[rows(D_MODEL) for _ in dts] + [fixed(D_MODEL) for _ in dts],
        out_shape=[jax.ShapeDtypeStruct((m, D_MODEL), dt) for dt in dts]
                  + [jax.ShapeDtypeStruct((ms, D_MODEL), dt) for dt in dts],
        scratch_shapes=[pltpu.VMEM((D_FF, D_MODEL), BF)],
        compiler_params=_params("arbitrary"),
        name="down_proj",
    )(x, g, xs, gs, w_down, gain)


def _layer(l, last, next_gain, xp, xs, xn, xns, mem, batch, seq, kt, vt, mk4, mv4, state_conv,
           prev_nk, prev_nv, slope_b, sink_b,
           norm_mix_g, w_in, gmlp_norm_g, gmlp_ws, gmlp_bs, attn_sinks, mem_norm_g, w_mem_kv,
           w_br_g, w_br_a, w_br_m, w_out, norm_ffn_g, w_up, conv_w, conv_b, w_down):
    dec = xs.shape[0]
    nreq = 8
    z, zs = _zsmall(xn, xns, w_in, l, 512)

    bs = gmlp_bs[l]
    bsb = jnp.repeat(bs.T, G_GDIM, axis=1)
    og, vg_last = _gmlp(z, gmlp_ws, gmlp_norm_g, bsb, l, batch, seq, 512)
    coef = jnp.stack([jnp.repeat(gmlp_ws[l, :, 0, 0], G_GDIM), jnp.repeat(bs[:, 0], G_GDIM)])
    ogs, vgs = _gmlp_sample(zs, gmlp_norm_g, coef, l)

    oa = _swa(attn_sinks, z, l, batch, seq)
    q3 = zs[:, C_Q:C_K].reshape(dec, SWA_HEADS, SWA_HD)
    newcol = lambda a, b: zs[:, a:b].reshape(dec // nreq, nreq, SWA_KV_WIDTH).transpose(0, 2, 1)
    oas, nk, nv = _swa_sample(q3, kt, vt, zs, newcol(C_K, C_VV), newcol(C_VV, C_M), slope_b, sink_b,
                              prev_nk, prev_nv, l, nreq)

    mkv = _memkv(mem, mem_norm_g, w_mem_kv, l, 256)
    om = _mem_attn(z, mkv, batch, seq, 512)
    oms = _mem_sample(zs[:, C_M:C_G].reshape(dec, MEM_HEADS, MEM_HD), mk4, mv4, l, nreq)

    merged, mergeds = _merge(xn, og, oa, om, xns, ogs, oas.reshape(dec, SWA_WIDTH).astype(BF),
                             oms.reshape(dec, MEM_WIDTH).astype(BF), w_in, w_br_g, w_br_a, w_br_m, l, 1024, 256)
    xp, xn2, xs, xn2s = _out_proj(xp, merged, xs, mergeds, w_out, norm_ffn_g, l, 256)
    g, ca, cb, gs, has, hbs = _up_conv(xn2, xn2s, w_up, conv_w, conv_b, state_conv, l, batch, seq, 1024, 512)
    outs = _down_proj(xp, g, xs, gs, w_down, next_gain, l, 256, 256, F32 if last else BF, not last)

    tail = lambda t: t.reshape(batch, SUBLANES, D_FF)[:, SUBLANES - (CONV_W - 1):]
    zb = z.reshape(batch, seq, C_G)[:, seq - WINDOW:]
    st = dict(
        pk=zb[:, :, C_K:C_VV].reshape(batch, WINDOW, SWA_KV, SWA_HD),
        pv=zb[:, :, C_VV:C_M].reshape(batch, WINDOW, SWA_KV, SWA_HD),
        mk=mkv[:, :MEM_WIDTH].reshape(batch, MEM_LEN, MEM_HEADS, MEM_HD),
        mv=mkv[:, MEM_WIDTH:].reshape(batch, MEM_LEN, MEM_HEADS, MEM_HD),
        gvp=vg_last.reshape(batch, CHUNK, G_WIDTH),
        gvs=vgs.reshape(dec, 1, G_WIDTH),
        cvp=jnp.concatenate([tail(ca), tail(cb)], axis=-1),
        cvs=jnp.concatenate([state_conv[l, :, 1:], jnp.concatenate([has, hbs], axis=-1)[:, None]], axis=1),
    )
    return outs, nk, nv, st


def kernel(x_prompt, x_sample, cache_swa_k, cache_swa_v, cache_mem_k, cache_mem_v, state_conv, mem_prompt, norm_mix_g, w_in, gmlp_norm_g, gmlp_ws, gmlp_bs, attn_sinks, mem_norm_g, w_mem_kv, w_br_g, w_br_a, w_br_m, w_out, norm_ffn_g, w_up, conv_w, conv_b, w_down, final_norm_g):
    batch, seq, _ = x_prompt.shape
    dec = x_sample.shape[0]
    assert x_sample.shape[1] == 1 and PAST_LEN % CHUNK == 0 and PAST_LEN >= WINDOW
    assert seq % 2048 == 0 and seq >= WINDOW and (seq - 1) // CHUNK * CHUNK == seq - CHUNK
    xp = x_prompt.reshape(batch * seq, D_MODEL)
    xs = x_sample.reshape(dec, D_MODEL)
    mem = mem_prompt.reshape(batch * MEM_LEN, D_MODEL)
    kt = cache_swa_k.transpose(0, 1, 3, 4, 2)
    vt = cache_swa_v.transpose(0, 1, 3, 4, 2)
    mk4 = cache_mem_k.reshape(DEPTH, dec, MEM_LEN * MEM_HEADS, MEM_HD)
    mv4 = cache_mem_v.reshape(DEPTH, dec, MEM_LEN * MEM_HEADS, MEM_HD)
    slope_b = jnp.broadcast_to(jnp.asarray(SLOPES, F32)[:, None], (SWA_HEADS, LANES))
    sink_b = jnp.broadcast_to(attn_sinks[:, :, None], (DEPTH, SWA_HEADS, LANES))
    nk = nv = None
    sts = []
    xn, xns = _norm(xp, xs, norm_mix_g, 0, BF, 512)
    for l in range(DEPTH):
        last = l == DEPTH - 1
        next_gain = (final_norm_g if last else norm_mix_g[l + 1]).reshape(1, D_MODEL)
        outs, nk, nv, st = _layer(l, last, next_gain, xp, xs, xn, xns, mem, batch, seq, kt, vt, mk4, mv4, state_conv,
                                  nk, nv, slope_b, sink_b,
                                  norm_mix_g, w_in, gmlp_norm_g, gmlp_ws, gmlp_bs, attn_sinks, mem_norm_g,
                                  w_mem_kv, w_br_g, w_br_a, w_br_m, w_out, norm_ffn_g, w_up, conv_w, conv_b, w_down)
        sts.append(st)
        if last:
            y_prompt, y_sample = outs
        else:
            xp, xn, xs, xns = outs
    stack = lambda key: jnp.stack([s[key] for s in sts])
    return (y_prompt.reshape(batch, seq, D_MODEL), y_sample.reshape(dec, 1, D_MODEL), stack("pk"), stack("pv"),
            nk.transpose(0, 1, 4, 2, 3), nv.transpose(0, 1, 4, 2, 3),
            stack("mk"), stack("mv"), stack("gvp"), stack("gvs"), stack("cvp"), stack("cvs"))
```

```python
import functools
import math

import jax
import jax.numpy as jnp
from jax import lax
from jax.experimental import pallas as pl
from jax.experimental.pallas import tpu as pltpu

D_MODEL = 2048
DEPTH = 2
PAST_LEN = 8192
MEM_LEN = 256
CHUNK = 128
G_GROUPS = 8
G_WIDTH = 768
G_GDIM = G_WIDTH // G_GROUPS
SWA_HEADS = 12
SWA_KV = 4
SWA_GROUP = SWA_HEADS // SWA_KV
SWA_HD = 64
SWA_WIDTH = SWA_HEADS * SWA_HD
SWA_KV_WIDTH = SWA_KV * SWA_HD
WINDOW = 128
MEM_HEADS = 4
MEM_HD = 128
MEM_WIDTH = MEM_HEADS * MEM_HD
N_BRANCH = 3
D_FF = 5632
CONV_W = 3
EPS = 1e-6

LANES = 128
SUBLANES = 8
VMEM_LIMIT = 56 * 1024 * 1024

C_U, C_V, C_Q, C_K, C_VV, C_M, C_G = 0, 768, 1536, 2304, 2560, 2816, 3328
ZT = 256
N_ACT = C_Q // ZT
assert all(c % ZT == 0 for c in (C_V, C_Q, C_K, C_VV, C_M, C_G))

BF = jnp.bfloat16
F32 = jnp.float32
NT_DIMS = (((1,), (1,)), ((), ()))


def _alibi_slopes(n):
    p = 2 ** int(math.floor(math.log2(n)))
    base = [2.0 ** (-8.0 * (i + 1) / p) for i in range(p)]
    extra = [2.0 ** (-8.0 * (2 * i + 1) / (2 * p)) for i in range(n - p)]
    return base + extra


SLOPES = _alibi_slopes(SWA_HEADS)


def _params(*sem):
    return pltpu.CompilerParams(dimension_semantics=sem, vmem_limit_bytes=VMEM_LIMIT)


def _resident(shape):
    return pl.BlockSpec(shape, lambda *_: (0,) * len(shape), pipeline_mode=pl.Buffered(1))


def _rms(x, g):
    ms = jnp.mean(x * x, axis=-1, keepdims=True)
    return x * lax.rsqrt(ms + EPS) * g


def _dot(a, b):
    return jnp.dot(a, b, preferred_element_type=F32)


def _first_inner():
    return pl.program_id(1) == 0


def _last_inner():
    return pl.program_id(1) == pl.num_programs(1) - 1


def _norm_kernel(x_ref, xs_ref, g_ref, o_ref, os_ref, *, l):
    g = g_ref[l:l + 1, :]
    o_ref[...] = _rms(x_ref[...], g).astype(o_ref.dtype)

    @pl.when(pl.program_id(0) == pl.num_programs(0) - 1)
    def _():
        os_ref[...] = _rms(xs_ref[...], g).astype(os_ref.dtype)


def _norm(x, xs, g, l, dtype, tm):
    m, ms = x.shape[0], xs.shape[0]
    row = pl.BlockSpec((tm, D_MODEL), lambda i: (i, 0))
    return pl.pallas_call(
        functools.partial(_norm_kernel, l=l),
        grid=(m // tm,),
        in_specs=[row, _resident((ms, D_MODEL)), _resident(g.shape)],
        out_specs=[row, pl.BlockSpec((ms, D_MODEL), lambda i: (0, 0))],
        out_shape=[jax.ShapeDtypeStruct((m, D_MODEL), dtype), jax.ShapeDtypeStruct((ms, D_MODEL), dtype)],
        compiler_params=_params("arbitrary"),
        name="rmsnorm",
    )(x, xs, g)


NZ = C_G // ZT


def _zsmall_kernel(xn_ref, xs_ref, w_ref, z_ref, zs_ref, wbf_ref):
    s = pl.program_id(0)

    @pl.when(s < NZ)
    def _():
        wbf_ref[s] = w_ref[...].astype(BF)

    def project(x, o_ref):
        for c in range(NZ):
            h = _dot(x, wbf_ref[c])
            o_ref[:, c * ZT:(c + 1) * ZT] = jax.nn.gelu(h) if c < N_ACT else h

    @pl.when(s >= NZ)
    def _():
        project(xn_ref[...], z_ref)

    @pl.when(s == pl.num_programs(0) - 1)
    def _():
        project(xs_ref[...], zs_ref)


def _zsmall(xn, xs, w_in, l, tm):
    m, ms = xn.shape[0], xs.shape[0]
    rows = lambda s: (jnp.maximum(s - NZ, 0), 0)
    return pl.pallas_call(
        _zsmall_kernel,
        grid=(NZ + m // tm,),
        in_specs=[pl.BlockSpec((tm, D_MODEL), rows), _resident((ms, D_MODEL)),
                  pl.BlockSpec((None, D_MODEL, ZT), lambda s: (l, 0, jnp.minimum(s, NZ - 1)))],
        out_specs=[pl.BlockSpec((tm, C_G), rows), pl.BlockSpec((ms, C_G), lambda s: (0, 0))],
        out_shape=[jax.ShapeDtypeStruct((m, C_G), F32), jax.ShapeDtypeStruct((ms, C_G), F32)],
        scratch_shapes=[pltpu.VMEM((NZ, D_MODEL, ZT), BF)],
        compiler_params=_params("arbitrary"),
        name="in_proj",
    )(xn, xs, w_in)


G_TILES = G_WIDTH // LANES
GA = [(t * LANES) // G_GDIM for t in range(G_TILES)]
BND = [(GA[t] + 1) * G_GDIM - t * LANES for t in range(G_TILES)]
assert all(0 < b < LANES and (GA[t] + 2) * G_GDIM >= (t + 1) * LANES for t, b in enumerate(BND))


def _gmlp_kernel(u_ref, gv_ref, ws_ref, gg_ref, bsb_ref, og_ref, vg_ref, *, l, nchunk):
    r = lax.broadcasted_iota(jnp.int32, (CHUNK, CHUNK), 0)
    c = lax.broadcasted_iota(jnp.int32, (CHUNK, CHUNK), 1)
    tri = r >= c
    wtri = [jnp.where(tri, ws_ref[g], 0.0).astype(BF) for g in range(G_GROUPS)]
    wst = [jnp.concatenate([wtri[GA[t]], wtri[GA[t] + 1]], axis=0) for t in range(G_TILES)]
    lane = lax.broadcasted_iota(jnp.int32, (CHUNK, LANES), 1)
    g = gg_ref[l:l + 1, :]
    for n in range(nchunk):
        rows = slice(n * CHUNK, (n + 1) * CHUNK)
        vg = _rms(gv_ref[rows, :], g)
        if n == nchunk - 1:
            vg_ref[...] = vg
        for t in range(G_TILES):
            cols = slice(t * LANES, (t + 1) * LANES)
            both = _dot(wst[t], vg[:, cols].astype(BF))
            mix = jnp.where(lane < BND[t], both[:CHUNK], both[CHUNK:]) + bsb_ref[:, cols]
            og_ref[rows, cols] = (u_ref[rows, cols] * mix).astype(BF)


def _gmlp(z, ws, gg, bsb, l, batch, seq, tr):
    nr = seq // tr
    zcol = lambda cb: pl.BlockSpec((tr, G_WIDTH), lambda b, i: (b * nr + i, cb))
    return pl.pallas_call(
        functools.partial(_gmlp_kernel, l=l, nchunk=tr // CHUNK),
        grid=(batch, nr),
        in_specs=[zcol(C_U // G_WIDTH), zcol(C_V // G_WIDTH),
                  pl.BlockSpec((None, G_GROUPS, CHUNK, CHUNK), lambda b, i: (l, 0, 0, 0)),
                  _resident(gg.shape), _resident((CHUNK, G_WIDTH))],
        out_specs=[pl.BlockSpec((tr, G_WIDTH), lambda b, i: (b * nr + i, 0)),
                   pl.BlockSpec((CHUNK, G_WIDTH), lambda b, i: (b, 0))],
        out_shape=[jax.ShapeDtypeStruct((batch * seq, G_WIDTH), BF),
                   jax.ShapeDtypeStruct((batch * CHUNK, G_WIDTH), F32)],
        compiler_params=_params("arbitrary", "arbitrary"),
        name="gmlp_spatial",
    )(z, z, ws, gg, bsb)


def _gmlp_sample_kernel(zs_ref, gg_ref, coef_ref, og_ref, vg_ref, *, l):
    vg = _rms(zs_ref[:, C_V:C_Q], gg_ref[l:l + 1, :])
    vg_ref[...] = vg
    og_ref[...] = (zs_ref[:, C_U:C_V] * (coef_ref[0:1, :] * vg + coef_ref[1:2, :])).astype(BF)


def _gmlp_sample(zs, gg, coef, l):
    m = zs.shape[0]
    return pl.pallas_call(
        functools.partial(_gmlp_sample_kernel, l=l),
        out_shape=[jax.ShapeDtypeStruct((m, G_WIDTH), BF), jax.ShapeDtypeStruct((m, G_WIDTH), F32)],
        name="gmlp_sample",
    )(zs, gg, coef)


def _swa_kernel(sink_ref, q_ref, kp_ref, kc_ref, vp_ref, vc_ref, o_ref, bias_ref, *, l):
    n = pl.program_id(1)
    rows = SWA_GROUP * WINDOW

    @pl.when((pl.program_id(0) == 0) & (n == 0))
    def _():
        r = lax.broadcasted_iota(jnp.int32, (WINDOW, 2 * WINDOW), 0)
        c = lax.broadcasted_iota(jnp.int32, (WINDOW, 2 * WINDOW), 1)
        dist = r + WINDOW - c
        valid = (dist >= 0) & (dist <= WINDOW)
        distf = dist.astype(F32)
        for h in range(SWA_HEADS):
            j, g = divmod(h, SWA_GROUP)
            pen = -SLOPES[h] * distf
            bias_ref[j, g * WINDOW:(g + 1) * WINDOW, :] = jnp.where(valid, pen, -jnp.inf)
            bias_ref[SWA_KV + j, g * WINDOW:(g + 1) * WINDOW, :] = jnp.where(valid & (c >= WINDOW), pen, -jnp.inf)

    first = (n == 0).astype(jnp.int32) * SWA_KV
    kcat = jnp.concatenate([kp_ref[...], kc_ref[...]], axis=0).astype(BF)
    vcat = jnp.concatenate([vp_ref[...], vc_ref[...]], axis=0).astype(BF)
    head = lax.broadcasted_iota(jnp.int32, (rows, 1), 0) // WINDOW
    for j in range(SWA_KV):
        kj = kcat[:, j * SWA_HD:(j + 1) * SWA_HD]
        vj = vcat[:, j * SWA_HD:(j + 1) * SWA_HD]
        hs = range(j * SWA_GROUP, (j + 1) * SWA_GROUP)
        q = jnp.concatenate([q_ref[:, h * SWA_HD:(h + 1) * SWA_HD] for h in hs], axis=0).astype(BF)
        sink = jnp.full((rows, 1), sink_ref[l, hs[-1]], F32)
        for g in range(SWA_GROUP - 1):
            sink = jnp.where(head == g, sink_ref[l, hs[g]], sink)
        s = lax.dot_general(q, kj, NT_DIMS, preferred_element_type=F32) * (SWA_HD ** -0.5) + bias_ref[first + j]
        mx = jnp.maximum(jnp.max(s, axis=-1, keepdims=True), sink)
        p = jnp.exp(s - mx)
        den = jnp.sum(p, axis=-1, keepdims=True) + jnp.exp(sink - mx)
        o = _dot(p.astype(BF), vj) / den
        for g, h in enumerate(hs):
            o_ref[:, h * SWA_HD:(h + 1) * SWA_HD] = o[g * WINDOW:(g + 1) * WINDOW].astype(BF)


def _swa(sinks, z, l, batch, seq):
    nb = seq // WINDOW
    cur = lambda n, cb: pl.BlockSpec((WINDOW, n), lambda b, i: (b * nb + i, cb))
    prev = lambda n, cb: pl.BlockSpec((WINDOW, n), lambda b, i: (b * nb + jnp.maximum(i - 1, 0), cb))
    kcb, vcb = C_K // SWA_KV_WIDTH, C_VV // SWA_KV_WIDTH
    return pl.pallas_call(
        functools.partial(_swa_kernel, l=l),
        grid=(batch, nb),
        in_specs=[pl.BlockSpec(memory_space=pltpu.SMEM), cur(SWA_WIDTH, C_Q // SWA_WIDTH),
                  prev(SWA_KV_WIDTH, kcb), cur(SWA_KV_WIDTH, kcb), prev(SWA_KV_WIDTH, vcb), cur(SWA_KV_WIDTH, vcb)],
        out_specs=cur(SWA_WIDTH, 0),
        out_shape=jax.ShapeDtypeStruct((batch * seq, SWA_WIDTH), BF),
        scratch_shapes=[pltpu.VMEM((2 * SWA_KV, SWA_GROUP * WINDOW, 2 * WINDOW), F32)],
        compiler_params=_params("arbitrary", "arbitrary"),
        name="swa_prompt",
    )(sinks, z, z, z, z, z)


def _swa_sample_kernel(q_ref, kt_ref, vt_ref, kn_ref, vn_ref, knc_ref, vnc_ref, slope_ref, sink_ref, *rest, l, nreq):
    o_ref, nk_ref, nv_ref = rest[-3:]
    kvw = SWA_KV_WIDTH
    hrow = lax.broadcasted_iota(jnp.int32, (SWA_HEADS, kvw), 0) // SWA_GROUP
    lblk = lax.broadcasted_iota(jnp.int32, (SWA_HEADS, kvw), 1) // SWA_HD
    own = hrow == lblk
    kvh = lax.broadcasted_iota(jnp.int32, (SWA_HEADS, SWA_HD), 0) // SWA_GROUP
    c = lax.broadcasted_iota(jnp.int32, (SWA_HEADS, WINDOW), 1)
    bias = slope_ref[...] * (WINDOW - c).astype(F32)
    sink = sink_ref[:, 0:1]
    pos = lax.broadcasted_iota(jnp.int32, (kvw, WINDOW), 1)
    scale = SWA_HD ** -0.5
    q = q_ref[...]
    qm = jnp.where(own, jnp.concatenate([q] * SWA_KV, axis=2), 0.0).astype(BF)
    kt = kt_ref[...].reshape(nreq, kvw, WINDOW)
    vt = vt_ref[...].reshape(nreq, kvw, WINDOW)
    rows = lambda ref: jnp.stack([ref[r:r + 1, :] for r in range(nreq)]).astype(BF).astype(F32)
    kn, vn = rows(kn_ref), rows(vn_ref)
    s_c = jnp.einsum("rhk,rkc->rhc", qm, kt.astype(BF), preferred_element_type=F32) * scale - bias
    s_n = jnp.sum(qm.astype(F32) * kn, axis=-1, keepdims=True) * scale
    mx = jnp.maximum(jnp.maximum(jnp.max(s_c, axis=-1, keepdims=True), s_n), sink)
    p_c = jnp.exp(s_c - mx)
    p_n = jnp.exp(s_n - mx)
    den = jnp.sum(p_c, axis=-1, keepdims=True) + p_n + jnp.exp(sink - mx)
    p_c = (p_c / den).astype(BF)
    p_n = (p_n / den).astype(BF).astype(F32)
    o_all = jnp.einsum("rhc,rkc->rhk", p_c, vt.astype(BF), preferred_element_type=F32) + p_n * vn
    o = jnp.zeros((nreq, SWA_HEADS, SWA_HD), F32)
    for j in range(SWA_KV):
        o = jnp.where(kvh == j, o_all[:, :, j * SWA_HD:(j + 1) * SWA_HD], o)
    o_ref[...] = o
    last = pos == WINDOW - 1
    for r in range(nreq):
        nk_ref[r] = jnp.where(last, knc_ref[:, r:r + 1], pltpu.roll(kt[r], WINDOW - 1, axis=1)).reshape(SWA_KV, SWA_HD, WINDOW)
        nv_ref[r] = jnp.where(last, vnc_ref[:, r:r + 1], pltpu.roll(vt[r], WINDOW - 1, axis=1)).reshape(SWA_KV, SWA_HD, WINDOW)


def _swa_sample(q3, kt, vt, zs, knc, vnc, slope_b, sink_b, prev_nk, prev_nv, l, nreq):
    m = q3.shape[0]
    cache = pl.BlockSpec((None, nreq, SWA_KV, SWA_HD, WINDOW), lambda i: (l, i, 0, 0, 0))
    newcol = pl.BlockSpec((None, SWA_KV_WIDTH, nreq), lambda i: (i, 0, 0))
    in_specs = [pl.BlockSpec((nreq, SWA_HEADS, SWA_HD), lambda i: (i, 0, 0)), cache, cache,
                pl.BlockSpec((nreq, SWA_KV_WIDTH), lambda i: (i, C_K // SWA_KV_WIDTH)),
                pl.BlockSpec((nreq, SWA_KV_WIDTH), lambda i: (i, C_VV // SWA_KV_WIDTH)),
                newcol, newcol, _resident((SWA_HEADS, LANES)),
                pl.BlockSpec((None, SWA_HEADS, LANES), lambda i: (l, 0, 0))]
    args = [q3, kt, vt, zs, zs, knc, vnc, slope_b, sink_b]
    aliases = {}
    if prev_nk is not None:
        in_specs += [pl.BlockSpec(memory_space=pl.ANY)] * 2
        aliases = {len(args): 1, len(args) + 1: 2}
        args += [prev_nk, prev_nv]
    return pl.pallas_call(
        functools.partial(_swa_sample_kernel, l=l, nreq=nreq),
        grid=(m // nreq,),
        in_specs=in_specs,
        out_specs=[pl.BlockSpec((nreq, SWA_HEADS, SWA_HD), lambda i: (i, 0, 0)), cache, cache],
        out_shape=[jax.ShapeDtypeStruct((m, SWA_HEADS, SWA_HD), F32),
                   jax.ShapeDtypeStruct(kt.shape, F32), jax.ShapeDtypeStruct(vt.shape, F32)],
        input_output_aliases=aliases,
        compiler_params=_params("arbitrary"),
        name="swa_sample",
    )(*args)


def _memkv_kernel(x_ref, g_ref, w_ref, o_ref, xn_ref, *, l):
    @pl.when(pl.program_id(0) == 0)
    def _():
        xn_ref[...] = _rms(x_ref[...], g_ref[l:l + 1, :]).astype(BF)

    o_ref[...] = _dot(xn_ref[...], w_ref[...].astype(BF))


def _memkv(mem, g, w, l, tn):
    m = mem.shape[0]
    return pl.pallas_call(
        functools.partial(_memkv_kernel, l=l),
        grid=(2 * MEM_WIDTH // tn,),
        in_specs=[_resident((m, D_MODEL)), _resident(g.shape),
                  pl.BlockSpec((None, D_MODEL, tn), lambda j: (l, 0, j))],
        out_specs=pl.BlockSpec((m, tn), lambda j: (0, j)),
        out_shape=jax.ShapeDtypeStruct((m, 2 * MEM_WIDTH), F32),
        scratch_shapes=[pltpu.VMEM((m, D_MODEL), BF)],
        compiler_params=_params("arbitrary"),
        name="mem_kv",
    )(mem, g, w)


def _mem_attn_kernel(qa_ref, qb_ref, mk_ref, mv_ref, o_ref):
    for h in range(MEM_HEADS):
        q_ref = qa_ref if h < MEM_HEADS // 2 else qb_ref
        qcols = slice((h % (MEM_HEADS // 2)) * MEM_HD, (h % (MEM_HEADS // 2) + 1) * MEM_HD)
        cols = slice(h * MEM_HD, (h + 1) * MEM_HD)
        s = lax.dot_general(q_ref[:, qcols].astype(BF), mk_ref[:, cols].astype(BF), NT_DIMS,
                            preferred_element_type=F32) * (MEM_HD ** -0.5)
        e = jnp.exp(s - jnp.max(s, axis=-1, keepdims=True))
        p = (e / jnp.sum(e, axis=-1, keepdims=True)).astype(BF)
        o_ref[:, cols] = _dot(p, mv_ref[:, cols].astype(BF)).astype(BF)


def _mem_attn(z, mkv, batch, seq, tq):
    nq = seq // tq
    half = MEM_WIDTH // 2
    qspec = lambda cb: pl.BlockSpec((tq, half), lambda b, i: (b * nq + i, cb))
    kvspec = lambda cb: pl.BlockSpec((MEM_LEN, MEM_WIDTH), lambda b, i: (b, cb))
    return pl.pallas_call(
        _mem_attn_kernel,
        grid=(batch, nq),
        in_specs=[qspec(C_M // half), qspec(C_M // half + 1), kvspec(0), kvspec(1)],
        out_specs=pl.BlockSpec((tq, MEM_WIDTH), lambda b, i: (b * nq + i, 0)),
        out_shape=jax.ShapeDtypeStruct((batch * seq, MEM_WIDTH), BF),
        compiler_params=_params("arbitrary", "arbitrary"),
        name="mem_attn_prompt",
    )(z, z, mkv, mkv)


def _mem_sample_kernel(q_ref, mk_ref, mv_ref, o_ref, *, nreq):
    nrow = MEM_LEN * MEM_HEADS
    own = (lax.broadcasted_iota(jnp.int32, (nreq, MEM_HEADS, nrow), 2) % MEM_HEADS
           == lax.broadcasted_iota(jnp.int32, (nreq, MEM_HEADS, nrow), 1))
    s = jnp.einsum("rhd,rkd->rhk", q_ref[...].astype(BF), mk_ref[...].astype(BF),
                   preferred_element_type=F32) * (MEM_HD ** -0.5)
    s = jnp.where(own, s, -jnp.inf)
    e = jnp.exp(s - jnp.max(s, axis=-1, keepdims=True))
    p = (e / jnp.sum(e, axis=-1, keepdims=True)).astype(BF)
    o_ref[...] = jnp.einsum("rhk,rkd->rhd", p, mv_ref[...].astype(BF), preferred_element_type=F32)


def _mem_sample(q3, mk, mv, l, nreq):
    m = q3.shape[0]
    qspec = pl.BlockSpec((nreq, MEM_HEADS, MEM_HD), lambda i: (i, 0, 0))
    kvspec = pl.BlockSpec((None, nreq, MEM_LEN * MEM_HEADS, MEM_HD), lambda i: (l, i, 0, 0))
    return pl.pallas_call(
        functools.partial(_mem_sample_kernel, nreq=nreq),
        grid=(m // nreq,),
        in_specs=[qspec, kvspec, kvspec],
        out_specs=qspec,
        out_shape=jax.ShapeDtypeStruct((m, MEM_HEADS, MEM_HD), F32),
        compiler_params=_params("arbitrary"),
        name="mem_attn_sample",
    )(q3, mk, mv)


def _merge_kernel(xn_ref, og_ref, oa_ref, om_ref, xns_ref, ogs_ref, oas_ref, oms_ref,
                  wg0_ref, wg1_ref, wg2_ref, wbg_ref, wba_ref, wbm_ref, out_ref, outs_ref,
                  bg0_ref, bg1_ref, bg2_ref, bbg_ref, bba_ref, bbm_ref):
    pairs = [(wg0_ref, bg0_ref), (wg1_ref, bg1_ref), (wg2_ref, bg2_ref),
             (wbg_ref, bbg_ref), (wba_ref, bba_ref), (wbm_ref, bbm_ref)]

    @pl.when(_first_inner())
    def _():
        for w_ref, b_ref in pairs:
            b_ref[...] = w_ref[...].astype(BF)

    def merged(xn, og, oa, om):
        def branch(bg_ref, o, bb_ref):
            return jax.nn.sigmoid(_dot(xn, bg_ref[...])) * _dot(o, bb_ref[...])
        return (branch(bg0_ref, og, bbg_ref) + branch(bg1_ref, oa, bba_ref) + branch(bg2_ref, om, bbm_ref)).astype(BF)

    out_ref[...] = merged(xn_ref[...], og_ref[...], oa_ref[...], om_ref[...])

    @pl.when(_last_inner())
    def _():
        outs_ref[...] = merged(xns_ref[...], ogs_ref[...], oas_ref[...], oms_ref[...])


def _merge(xn, og, oa, om, xns, ogs, oas, oms, w_in, wbg, wba, wbm, l, tm, tn):
    m, ms = xn.shape[0], xns.shape[0]
    nt = D_MODEL // tn
    row = lambda n: pl.BlockSpec((tm, n), lambda j, i: (i, 0))
    gate = lambda b: pl.BlockSpec((None, D_MODEL, tn), lambda j, i: (l, 0, C_G // tn + b * nt + j))
    col = lambda n: pl.BlockSpec((None, n, tn), lambda j, i: (l, 0, j))
    widths = (G_WIDTH, SWA_WIDTH, MEM_WIDTH)
    return pl.pallas_call(
        _merge_kernel,
        grid=(nt, m // tm),
        in_specs=[row(D_MODEL)] + [row(n) for n in widths]
                 + [_resident((ms, D_MODEL))] + [_resident((ms, n)) for n in widths]
                 + [gate(0), gate(1), gate(2)] + [col(n) for n in widths],
        out_specs=[pl.BlockSpec((tm, tn), lambda j, i: (i, j)), pl.BlockSpec((ms, tn), lambda j, i: (0, j))],
        out_shape=[jax.ShapeDtypeStruct((m, D_MODEL), BF), jax.ShapeDtypeStruct((ms, D_MODEL), BF)],
        scratch_shapes=[pltpu.VMEM((D_MODEL, tn), BF)] * 3 + [pltpu.VMEM((n, tn), BF) for n in widths],
        compiler_params=_params("arbitrary", "arbitrary"),
        name="merge",
    )(xn, og, oa, om, xns, ogs, oas, oms, w_in, w_in, w_in, wbg, wba, wbm)


def _out_kernel(x_ref, mg_ref, xs_ref, mgs_ref, w_ref, g_ref, xo_ref, xn_ref, xos_ref, xns_ref, wbf_ref, *, l):
    @pl.when(pl.program_id(0) == 0)
    def _():
        wbf_ref[...] = w_ref[...].astype(BF)

    g = g_ref[l:l + 1, :]
    x = x_ref[...] + _dot(mg_ref[...], wbf_ref[...])
    xo_ref[...] = x
    xn_ref[...] = _rms(x, g).astype(BF)

    @pl.when(pl.program_id(0) == pl.num_programs(0) - 1)
    def _():
        xs = xs_ref[...] + _dot(mgs_ref[...], wbf_ref[...])
        xos_ref[...] = xs
        xns_ref[...] = _rms(xs, g).astype(BF)


def _out_proj(x, merged, xs, mergeds, w_out, g, l, tm):
    m, ms = x.shape[0], xs.shape[0]
    row = pl.BlockSpec((tm, D_MODEL), lambda i: (i, 0))
    srow = pl.BlockSpec((ms, D_MODEL), lambda i: (0, 0))
    return pl.pallas_call(
        functools.partial(_out_kernel, l=l),
        grid=(m // tm,),
        in_specs=[row, row, _resident((ms, D_MODEL)), _resident((ms, D_MODEL)),
                  pl.BlockSpec((None, D_MODEL, D_MODEL), lambda i: (l, 0, 0), pipeline_mode=pl.Buffered(1)),
                  _resident(g.shape)],
        out_specs=[row, row, srow, srow],
        out_shape=[jax.ShapeDtypeStruct((m, D_MODEL), F32), jax.ShapeDtypeStruct((m, D_MODEL), BF),
                   jax.ShapeDtypeStruct((ms, D_MODEL), F32), jax.ShapeDtypeStruct((ms, D_MODEL), BF)],
        scratch_shapes=[pltpu.VMEM((D_MODEL, D_MODEL), BF)],
        compiler_params=_params("arbitrary"),
        name="out_proj",
    )(x, merged, xs, mergeds, w_out, g)


def _up_kernel(xn_ref, xns_ref, wa_ref, wb_ref, cwa_ref, cwb_ref, cba_ref, cbb_ref, sta_ref, stb_ref,
               g_ref, ca_ref, cb_ref, gs_ref, has_ref, hbs_ref, ha_ref, hb_ref, ba_ref, bb_ref,
               *, l, tm, tiles_per_seq):
    @pl.when(_first_inner())
    def _():
        ba_ref[...] = wa_ref[...].astype(BF)
        bb_ref[...] = wb_ref[...].astype(BF)

    @pl.when(pl.program_id(1) % tiles_per_seq == 0)
    def _():
        ha_ref[...] = jnp.zeros(ha_ref.shape, F32)
        hb_ref[...] = jnp.zeros(hb_ref.shape, F32)

    top = lax.broadcasted_iota(jnp.int32, (SUBLANES, ha_ref.shape[1]), 0)
    xn = xn_ref[...]

    def conv(h, prev_ref, cw_ref, bias_ref, tail_ref):
        prev = prev_ref[...]
        acc = bias_ref[l:l + 1, :]
        for j in range(CONV_W - 1):
            back = CONV_W - 1 - j
            rolled = pltpu.roll(h, back, axis=0)
            head = jnp.where(top < back, pltpu.roll(prev, back, axis=0), rolled[0:SUBLANES])
            acc = acc + jnp.concatenate([head, rolled[SUBLANES:]], axis=0) * cw_ref[j:j + 1, :]
        tail = h[tm - SUBLANES:]
        tail_ref[...] = tail
        prev_ref[...] = tail
        return acc + h * cw_ref[CONV_W - 1:CONV_W, :]

    a = conv(_dot(xn, ba_ref[...]), ha_ref, cwa_ref, cba_ref, ca_ref)
    b = conv(_dot(xn, bb_ref[...]), hb_ref, cwb_ref, cbb_ref, cb_ref)
    g_ref[...] = (jax.nn.gelu(a) * b).astype(BF)

    @pl.when(_last_inner())
    def _():
        xns = xns_ref[...]

        def conv_s(bw_ref, cw_ref, bias_ref, st_ref, h_out_ref):
            h = _dot(xns, bw_ref[...])
            h_out_ref[...] = h
            return (bias_ref[l:l + 1, :] + st_ref[:, 0, :] * cw_ref[0:1, :] + st_ref[:, 1, :] * cw_ref[1:2, :]
                    + h * cw_ref[2:3, :])

        a_s = conv_s(ba_ref, cwa_ref, cba_ref, sta_ref, has_ref)
        b_s = conv_s(bb_ref, cwb_ref, cbb_ref, stb_ref, hbs_ref)
        gs_ref[...] = (jax.nn.gelu(a_s) * b_s).astype(BF)


def _up_conv(xn, xns, w_up, conv_w, conv_b, state, l, batch, seq, tm, tn):
    m, ms = xn.shape[0], xns.shape[0]
    nt = D_FF // tn
    nrow = m // tm
    tiles_per_seq = seq // tm
    half = lambda off: (lambda j, i: (l, 0, off * nt + j))
    wspec = lambda off: pl.BlockSpec((None, D_MODEL, tn), half(off))
    cwspec = lambda off: pl.BlockSpec((None, CONV_W, tn), half(off))
    cbspec = lambda off: pl.BlockSpec((DEPTH, tn), lambda j, i: (0, off * nt + j))
    stspec = lambda off: pl.BlockSpec((None, ms, CONV_W - 1, tn), lambda j, i: (l, 0, 0, off * nt + j))
    tail = pl.BlockSpec((SUBLANES, tn), lambda j, i: (i // tiles_per_seq, j))
    scol = pl.BlockSpec((ms, tn), lambda j, i: (0, j))
    return pl.pallas_call(
        functools.partial(_up_kernel, l=l, tm=tm, tiles_per_seq=tiles_per_seq),
        grid=(nt, nrow),
        in_specs=[pl.BlockSpec((tm, D_MODEL), lambda j, i: (i, 0)), _resident((ms, D_MODEL)),
                  wspec(0), wspec(1), cwspec(0), cwspec(1), cbspec(0), cbspec(1), stspec(0), stspec(1)],
        out_specs=[pl.BlockSpec((tm, tn), lambda j, i: (i, j)), tail, tail, scol, scol, scol],
        out_shape=[jax.ShapeDtypeStruct((m, D_FF), BF),
                   jax.ShapeDtypeStruct((batch * SUBLANES, D_FF), F32),
                   jax.ShapeDtypeStruct((batch * SUBLANES, D_FF), F32),
                   jax.ShapeDtypeStruct((ms, D_FF), BF),
                   jax.ShapeDtypeStruct((ms, D_FF), F32), jax.ShapeDtypeStruct((ms, D_FF), F32)],
        scratch_shapes=[pltpu.VMEM((SUBLANES, tn), F32)] * 2 + [pltpu.VMEM((D_MODEL, tn), BF)] * 2,
        compiler_params=_params("arbitrary", "arbitrary"),
        name="up_conv_glu",
    )(xn, xns, w_up, w_up, conv_w, conv_w, conv_b, conv_b, state, state)


def _down_kernel(x_ref, g_ref, xs_ref, gs_ref, w_ref, gain_ref, *rest, tk, ncast, emit_x):
    wbf_ref = rest[-1]
    outs = rest[:-1]
    s = pl.program_id(0)

    @pl.when(s < ncast)
    def _():
        wbf_ref[pl.ds(pl.multiple_of(s * tk, tk), tk), :] = w_ref[...].astype(BF)

    def tile(x, g, o_refs):
        x = x + _dot(g, wbf_ref[...])
        if emit_x:
            o_refs[0][...] = x
        o_refs[-1][...] = _rms(x, gain_ref[...]).astype(o_refs[-1].dtype)

    nout = len(outs) // 2

    @pl.when(s >= ncast)
    def _():
        tile(x_ref[...], g_ref[...], outs[:nout])

    @pl.when(s == pl.num_programs(0) - 1)
    def _():
        tile(xs_ref[...], gs_ref[...], outs[nout:])


def _down_proj(x, g, xs, gs, w_down, gain, l, tm, tk, norm_dtype, emit_x):
    m, ms = x.shape[0], xs.shape[0]
    ncast = D_FF // tk
    rows = lambda n: pl.BlockSpec((tm, n), lambda s: (jnp.maximum(s - ncast, 0), 0))
    fixed = lambda n: pl.BlockSpec((ms, n), lambda s: (0, 0))
    dts = ([F32] if emit_x else []) + [norm_dtype]
    return pl.pallas_call(
        functools.partial(_down_kernel, tk=tk, ncast=ncast, emit_x=emit_x),
        grid=(ncast + m // tm,),
        in_specs=[rows(D_MODEL), rows(D_FF), _resident((ms, D_MODEL)), _resident((ms, D_FF)),
                  pl.BlockSpec((None, tk, D_MODEL), lambda s: (l, jnp.minimum(s, ncast - 1), 0)),
                  _resident((1, D_MODEL))],
        out_specs=[rows(D_MODEL) for _ in dts] + [fixed(D_MODEL) for _ in dts],
        out_shape=[jax.ShapeDtypeStruct((m, D_MODEL), dt) for dt in dts]
                  + [jax.ShapeDtypeStruct((ms, D_MODEL), dt) for dt in dts],
        scratch_shapes=[pltpu.VMEM((D_FF, D_MODEL), BF)],
        compiler_params=_params("arbitrary"),
        name="down_proj",
    )(x, g, xs, gs, w_down, gain)


def _conv_state_kernel(st_ref, *rest, nt):
    o_ref = rest[-1]
    first_half = pl.program_id(0) < nt
    for l in range(DEPTH):
        ha_ref, hb_ref = rest[2 * l], rest[2 * l + 1]
        o_ref[l, :, 0, :] = st_ref[l, :, CONV_W - 2, :]
        o_ref[l, :, 1, :] = jnp.where(first_half, ha_ref[...], hb_ref[...])


def _conv_state(state, halves, tn):
    depth, ms = state.shape[:2]
    nt = D_FF // tn
    blk = pl.BlockSpec((depth, ms, CONV_W - 1, tn), lambda j: (0, 0, 0, j))
    a_spec = pl.BlockSpec((ms, tn), lambda j: (0, jnp.minimum(j, nt - 1)))
    b_spec = pl.BlockSpec((ms, tn), lambda j: (0, jnp.maximum(j - nt, 0)))
    return pl.pallas_call(
        functools.partial(_conv_state_kernel, nt=nt),
        grid=(2 * nt,),
        in_specs=[blk] + [a_spec, b_spec] * depth,
        out_specs=blk,
        out_shape=jax.ShapeDtypeStruct(state.shape, F32),
        compiler_params=_params("arbitrary"),
        name="conv_state_sample",
    )(state, *halves)


def _layer(l, last, next_gain, xp, xs, xn, xns, mem, batch, seq, kt, vt, mk4, mv4, state_conv,
           prev_nk, prev_nv, slope_b, sink_b,
           norm_mix_g, w_in, gmlp_norm_g, gmlp_ws, gmlp_bs, attn_sinks, mem_norm_g, w_mem_kv,
           w_br_g, w_br_a, w_br_m, w_out, norm_ffn_g, w_up, conv_w, conv_b, w_down):
    dec = xs.shape[0]
    nreq = 8
    z, zs = _zsmall(xn, xns, w_in, l, 512)

    bs = gmlp_bs[l]
    bsb = jnp.repeat(bs.T, G_GDIM, axis=1)
    og, vg_last = _gmlp(z, gmlp_ws, gmlp_norm_g, bsb, l, batch, seq, 512)
    coef = jnp.stack([jnp.repeat(gmlp_ws[l, :, 0, 0], G_GDIM), jnp.repeat(bs[:, 0], G_GDIM)])
    ogs, vgs = _gmlp_sample(zs, gmlp_norm_g, coef, l)

    oa = _swa(attn_sinks, z, l, batch, seq)
    q3 = zs[:, C_Q:C_K].reshape(dec, SWA_HEADS, SWA_HD)
    newcol = lambda a, b: zs[:, a:b].reshape(dec // nreq, nreq, SWA_KV_WIDTH).transpose(0, 2, 1)
    oas, nk, nv = _swa_sample(q3, kt, vt, zs, newcol(C_K, C_VV), newcol(C_VV, C_M), slope_b, sink_b,
                              prev_nk, prev_nv, l, nreq)

    mkv = _memkv(mem, mem_norm_g, w_mem_kv, l, 256)
    om = _mem_attn(z, mkv, batch, seq, 512)
    oms = _mem_sample(zs[:, C_M:C_G].reshape(dec, MEM_HEADS, MEM_HD), mk4, mv4, l, nreq)

    merged, mergeds = _merge(xn, og, oa, om, xns, ogs, oas.reshape(dec, SWA_WIDTH).astype(BF),
                             oms.reshape(dec, MEM_WIDTH).astype(BF), w_in, w_br_g, w_br_a, w_br_m, l, 1024, 256)
    xp, xn2, xs, xn2s = _out_proj(xp, merged, xs, mergeds, w_out, norm_ffn_g, l, 256)
    g, ca, cb, gs, has, hbs = _up_conv(xn2, xn2s, w_up, conv_w, conv_b, state_conv, l, batch, seq, 1024, 512)
    outs = _down_proj(xp, g, xs, gs, w_down, next_gain, l, 256, 256, F32 if last else BF, not last)

    tail = lambda t: t.reshape(batch, SUBLANES, D_FF)[:, SUBLANES - (CONV_W - 1):]
    zb = z.reshape(batch, seq, C_G)[:, seq - WINDOW:]
    st = dict(
        pk=zb[:, :, C_K:C_VV].reshape(batch, WINDOW, SWA_KV, SWA_HD),
        pv=zb[:, :, C_VV:C_M].reshape(batch, WINDOW, SWA_KV, SWA_HD),
        mk=mkv[:, :MEM_WIDTH].reshape(batch, MEM_LEN, MEM_HEADS, MEM_HD),
        mv=mkv[:, MEM_WIDTH:].reshape(batch, MEM_LEN, MEM_HEADS, MEM_HD),
        gvp=vg_last.reshape(batch, CHUNK, G_WIDTH),
        gvs=vgs.reshape(dec, 1, G_WIDTH),
        cvp=jnp.concatenate([tail(ca), tail(cb)], axis=-1),
        cvs=(has, hbs),
    )
    return outs, nk, nv, st


def kernel(x_prompt, x_sample, cache_swa_k, cache_swa_v, cache_mem_k, cache_mem_v, state_conv, mem_prompt, norm_mix_g, w_in, gmlp_norm_g, gmlp_ws, gmlp_bs, attn_sinks, mem_norm_g, w_mem_kv, w_br_g, w_br_a, w_br_m, w_out, norm_ffn_g, w_up, conv_w, conv_b, w_down, final_norm_g):
    batch, seq, _ = x_prompt.shape
    dec = x_sample.shape[0]
    assert x_sample.shape[1] == 1 and PAST_LEN % CHUNK == 0 and PAST_LEN >= WINDOW
    assert seq % 2048 == 0 and seq >= WINDOW and (seq - 1) // CHUNK * CHUNK == seq - CHUNK
    xp = x_prompt.reshape(batch * seq, D_MODEL)
    xs = x_sample.reshape(dec, D_MODEL)
    mem = mem_prompt.reshape(batch * MEM_LEN, D_MODEL)
    kt = cache_swa_k.transpose(0, 1, 3, 4, 2)
    vt = cache_swa_v.transpose(0, 1, 3, 4, 2)
    mk4 = cache_mem_k.reshape(DEPTH, dec, MEM_LEN * MEM_HEADS, MEM_HD)
    mv4 = cache_mem_v.reshape(DEPTH, dec, MEM_LEN * MEM_HEADS, MEM_HD)
    slope_b = jnp.broadcast_to(jnp.asarray(SLOPES, F32)[:, None], (SWA_HEADS, LANES))
    sink_b = jnp.broadcast_to(attn_sinks[:, :, None], (DEPTH, SWA_HEADS, LANES))
    nk = nv = None
    sts = []
    xn, xns = _norm(xp, xs, norm_mix_g, 0, BF, 512)
    for l in range(DEPTH):
        last = l == DEPTH - 1
        next_gain = (final_norm_g if last else norm_mix_g[l + 1]).reshape(1, D_MODEL)
        outs, nk, nv, st = _layer(l, last, next_gain, xp, xs, xn, xns, mem, batch, seq, kt, vt, mk4, mv4, state_conv,
                                  nk, nv, slope_b, sink_b,
                                  norm_mix_g, w_in, gmlp_norm_g, gmlp_ws, gmlp_bs, attn_sinks, mem_norm_g,
                                  w_mem_kv, w_br_g, w_br_a, w_br_m, w_out, norm_ffn_g, w_up, conv_w, conv_b, w_down)
        sts.append(st)
        if last:
            y_prompt, y_sample = outs
        else:
            xp, xn, xs, xns = outs
    stack = lambda key: jnp.stack([s[key] for s in sts])
    return (y_prompt.reshape(batch, seq, D_MODEL), y_sample.reshape(dec, 1, D_MODEL), stack("pk"), stack("pv"),
            nk.transpose(0, 1, 4, 2, 3), nv.transpose(0, 1, 4, 2, 3),
            stack("mk"), stack("mv"), stack("gvp"), stack("gvs"), stack("cvp"),
            _conv_state(state_conv, [h for s in sts for h in s["cvs"]], 512))
```

```python
import functools
import math

import jax
import jax.numpy as jnp
from jax import lax
from jax.experimental import pallas as pl
from jax.experimental.pallas import tpu as pltpu

D_MODEL = 2048
DEPTH = 2
PAST_LEN = 8192
MEM_LEN = 256
CHUNK = 128
G_GROUPS = 8
G_WIDTH = 768
G_GDIM = G_WIDTH // G_GROUPS
SWA_HEADS = 12
SWA_KV = 4
SWA_GROUP = SWA_HEADS // SWA_KV
SWA_HD = 64
SWA_WIDTH = SWA_HEADS * SWA_HD
SWA_KV_WIDTH = SWA_KV * SWA_HD
WINDOW = 128
MEM_HEADS = 4
MEM_HD = 128
MEM_WIDTH = MEM_HEADS * MEM_HD
N_BRANCH = 3
D_FF = 5632
CONV_W = 3
EPS = 1e-6

LANES = 128
SUBLANES = 8
VMEM_LIMIT = 56 * 1024 * 1024

C_U, C_V, C_Q, C_K, C_VV, C_M, C_G = 0, 768, 1536, 2304, 2560, 2816, 3328
ZT = 256
N_ACT = C_Q // ZT
assert all(c % ZT == 0 for c in (C_V, C_Q, C_K, C_VV, C_M, C_G))

BF = jnp.bfloat16
F32 = jnp.float32
NT_DIMS = (((1,), (1,)), ((), ()))


def _alibi_slopes(n):
    p = 2 ** int(math.floor(math.log2(n)))
    base = [2.0 ** (-8.0 * (i + 1) / p) for i in range(p)]
    extra = [2.0 ** (-8.0 * (2 * i + 1) / (2 * p)) for i in range(n - p)]
    return base + extra


SLOPES = _alibi_slopes(SWA_HEADS)


def _params(*sem):
    return pltpu.CompilerParams(dimension_semantics=sem, vmem_limit_bytes=VMEM_LIMIT)


def _resident(shape):
    return pl.BlockSpec(shape, lambda *_: (0,) * len(shape), pipeline_mode=pl.Buffered(1))


def _rms(x, g):
    ms = jnp.mean(x * x, axis=-1, keepdims=True)
    return x * lax.rsqrt(ms + EPS) * g


def _dot(a, b):
    return jnp.dot(a, b, preferred_element_type=F32)


def _first_inner():
    return pl.program_id(1) == 0


def _last_inner():
    return pl.program_id(1) == pl.num_programs(1) - 1


def _norm_kernel(x_ref, xs_ref, g_ref, o_ref, os_ref, *, l):
    g = g_ref[l:l + 1, :]
    o_ref[...] = _rms(x_ref[...], g).astype(o_ref.dtype)

    @pl.when(pl.program_id(0) == pl.num_programs(0) - 1)
    def _():
        os_ref[...] = _rms(xs_ref[...], g).astype(os_ref.dtype)


def _norm(x, xs, g, l, dtype, tm):
    m, ms = x.shape[0], xs.shape[0]
    row = pl.BlockSpec((tm, D_MODEL), lambda i: (i, 0))
    return pl.pallas_call(
        functools.partial(_norm_kernel, l=l),
        grid=(m // tm,),
        in_specs=[row, _resident((ms, D_MODEL)), _resident(g.shape)],
        out_specs=[row, pl.BlockSpec((ms, D_MODEL), lambda i: (0, 0))],
        out_shape=[jax.ShapeDtypeStruct((m, D_MODEL), dtype), jax.ShapeDtypeStruct((ms, D_MODEL), dtype)],
        compiler_params=_params("arbitrary"),
        name="rmsnorm",
    )(x, xs, g)


NZ = C_G // ZT


QKV_TILES = range(C_Q // ZT, C_M // ZT)
QKV_W = C_M - C_Q


def _swa_bias_table(bias_ref):
    r = lax.broadcasted_iota(jnp.int32, (WINDOW, 2 * WINDOW), 0)
    c = lax.broadcasted_iota(jnp.int32, (WINDOW, 2 * WINDOW), 1)
    dist = r + WINDOW - c
    valid = (dist >= 0) & (dist <= WINDOW)
    distf = dist.astype(F32)
    for h in range(SWA_HEADS):
        j, g = divmod(h, SWA_GROUP)
        pen = -SLOPES[h] * distf
        bias_ref[j, g * WINDOW:(g + 1) * WINDOW, :] = jnp.where(valid, pen, -jnp.inf)
        bias_ref[SWA_KV + j, g * WINDOW:(g + 1) * WINDOW, :] = jnp.where(valid & (c >= WINDOW), pen, -jnp.inf)


def _front_kernel(sink_ref, xn_ref, xs_ref, w_ref, z_ref, oa_ref, zs_ref, wbf_ref, qkv_ref, carry_ref, bias_ref,
                  *, l, tm, tiles_per_seq):
    s = pl.program_id(0)
    rows = SWA_GROUP * WINDOW
    kcol, vcol = C_K - C_Q, C_VV - C_Q

    @pl.when(s < NZ)
    def _():
        wbf_ref[s] = w_ref[...].astype(BF)

    @pl.when(s == NZ)
    def _():
        _swa_bias_table(bias_ref)
        carry_ref[...] = jnp.zeros(carry_ref.shape, F32)

    def column(x, c):
        h = _dot(x, wbf_ref[c])
        return jax.nn.gelu(h) if c < N_ACT else h

    @pl.when(s >= NZ)
    def _():
        x = xn_ref[...]
        for c in QKV_TILES:
            h = column(x, c)
            z_ref[:, c * ZT:(c + 1) * ZT] = h
            qkv_ref[:, c * ZT - C_Q:(c + 1) * ZT - C_Q] = h

        others = [c for c in range(NZ) if c not in QKV_TILES]
        units = [(b, j) for b in range(tm // WINDOW) for j in range(SWA_KV)]
        head = lax.broadcasted_iota(jnp.int32, (rows, 1), 0) // WINDOW
        fresh = ((s - NZ) % tiles_per_seq == 0).astype(jnp.int32) * SWA_KV
        for u, (b, j) in enumerate(units):
            if u * len(others) // len(units) != (u + 1) * len(others) // len(units):
                c = others[u * len(others) // len(units)]
                z_ref[:, c * ZT:(c + 1) * ZT] = column(x, c)
            own = slice(b * WINDOW, (b + 1) * WINDOW)
            kvh = lambda base: slice(base + j * SWA_HD, base + (j + 1) * SWA_HD)
            if b == 0:
                kprev, vprev = carry_ref[:, kvh(0)], carry_ref[:, kvh(SWA_KV_WIDTH)]
            else:
                prev = slice((b - 1) * WINDOW, b * WINDOW)
                kprev, vprev = qkv_ref[prev, kvh(kcol)], qkv_ref[prev, kvh(vcol)]
            kj = jnp.concatenate([kprev, qkv_ref[own, kvh(kcol)]], axis=0).astype(BF)
            vj = jnp.concatenate([vprev, qkv_ref[own, kvh(vcol)]], axis=0).astype(BF)
            hs = range(j * SWA_GROUP, (j + 1) * SWA_GROUP)
            q = jnp.concatenate([qkv_ref[own, h * SWA_HD:(h + 1) * SWA_HD] for h in hs], axis=0).astype(BF)
            sink = jnp.full((rows, 1), sink_ref[l, hs[-1]], F32)
            for g in range(SWA_GROUP - 1):
                sink = jnp.where(head == g, sink_ref[l, hs[g]], sink)
            bias = bias_ref[(fresh if b == 0 else 0) + j]
            sc = lax.dot_general(q, kj, NT_DIMS, preferred_element_type=F32) * (SWA_HD ** -0.5) + bias
            mx = jnp.maximum(jnp.max(sc, axis=-1, keepdims=True), sink)
            p = jnp.exp(sc - mx)
            den = jnp.sum(p, axis=-1, keepdims=True) + jnp.exp(sink - mx)
            o = _dot(p.astype(BF), vj) / den
            for g, h in enumerate(hs):
                oa_ref[own, h * SWA_HD:(h + 1) * SWA_HD] = o[g * WINDOW:(g + 1) * WINDOW].astype(BF)
        last = slice(tm - WINDOW, tm)
        carry_ref[...] = jnp.concatenate(
            [qkv_ref[last, kcol:kcol + SWA_KV_WIDTH], qkv_ref[last, vcol:vcol + SWA_KV_WIDTH]], axis=1)

    @pl.when(s == pl.num_programs(0) - 1)
    def _():
        xs = xs_ref[...]
        for c in range(NZ):
            zs_ref[:, c * ZT:(c + 1) * ZT] = column(xs, c)


def _front(sinks, xn, xs, w_in, l, seq, tm):
    m, ms = xn.shape[0], xs.shape[0]
    rows = lambda s: (jnp.maximum(s - NZ, 0), 0)
    return pl.pallas_call(
        functools.partial(_front_kernel, l=l, tm=tm, tiles_per_seq=seq // tm),
        grid=(NZ + m // tm,),
        in_specs=[pl.BlockSpec(memory_space=pltpu.SMEM), pl.BlockSpec((tm, D_MODEL), rows), _resident((ms, D_MODEL)),
                  pl.BlockSpec((None, D_MODEL, ZT), lambda s: (l, 0, jnp.minimum(s, NZ - 1)))],
        out_specs=[pl.BlockSpec((tm, C_G), rows), pl.BlockSpec((tm, SWA_WIDTH), rows),
                   pl.BlockSpec((ms, C_G), lambda s: (0, 0))],
        out_shape=[jax.ShapeDtypeStruct((m, C_G), F32), jax.ShapeDtypeStruct((m, SWA_WIDTH), BF),
                   jax.ShapeDtypeStruct((ms, C_G), F32)],
        scratch_shapes=[pltpu.VMEM((NZ, D_MODEL, ZT), BF), pltpu.VMEM((tm, QKV_W), F32),
                        pltpu.VMEM((WINDOW, 2 * SWA_KV_WIDTH), F32),
                        pltpu.VMEM((2 * SWA_KV, SWA_GROUP * WINDOW, 2 * WINDOW), F32)],
        compiler_params=_params("arbitrary"),
        name="in_proj_swa",
    )(sinks, xn, xs, w_in)


G_TILES = G_WIDTH // LANES
GA = [(t * LANES) // G_GDIM for t in range(G_TILES)]
BND = [(GA[t] + 1) * G_GDIM - t * LANES for t in range(G_TILES)]
assert all(0 < b < LANES and (GA[t] + 2) * G_GDIM >= (t + 1) * LANES for t, b in enumerate(BND))


def _gmlp_kernel(u_ref, gv_ref, ws_ref, gg_ref, bsb_ref, og_ref, vg_ref, *, l, nchunk):
    r = lax.broadcasted_iota(jnp.int32, (CHUNK, CHUNK), 0)
    c = lax.broadcasted_iota(jnp.int32, (CHUNK, CHUNK), 1)
    tri = r >= c
    wtri = [jnp.where(tri, ws_ref[g], 0.0).astype(BF) for g in range(G_GROUPS)]
    wst = [jnp.concatenate([wtri[GA[t]], wtri[GA[t] + 1]], axis=0) for t in range(G_TILES)]
    lane = lax.broadcasted_iota(jnp.int32, (CHUNK, LANES), 1)
    g = gg_ref[l:l + 1, :]
    for n in range(nchunk):
        rows = slice(n * CHUNK, (n + 1) * CHUNK)
        vg = _rms(gv_ref[rows, :], g)
        if n == nchunk - 1:
            vg_ref[...] = vg
        for t in range(G_TILES):
            cols = slice(t * LANES, (t + 1) * LANES)
            both = _dot(wst[t], vg[:, cols].astype(BF))
            mix = jnp.where(lane < BND[t], both[:CHUNK], both[CHUNK:]) + bsb_ref[:, cols]
            og_ref[rows, cols] = (u_ref[rows, cols] * mix).astype(BF)


def _gmlp(z, ws, gg, bsb, l, batch, seq, tr):
    nr = seq // tr
    zcol = lambda cb: pl.BlockSpec((tr, G_WIDTH), lambda b, i: (b * nr + i, cb))
    return pl.pallas_call(
        functools.partial(_gmlp_kernel, l=l, nchunk=tr // CHUNK),
        grid=(batch, nr),
        in_specs=[zcol(C_U // G_WIDTH), zcol(C_V // G_WIDTH),
                  pl.BlockSpec((None, G_GROUPS, CHUNK, CHUNK), lambda b, i: (l, 0, 0, 0)),
                  _resident(gg.shape), _resident((CHUNK, G_WIDTH))],
        out_specs=[pl.BlockSpec((tr, G_WIDTH), lambda b, i: (b * nr + i, 0)),
                   pl.BlockSpec((CHUNK, G_WIDTH), lambda b, i: (b, 0))],
        out_shape=[jax.ShapeDtypeStruct((batch * seq, G_WIDTH), BF),
                   jax.ShapeDtypeStruct((batch * CHUNK, G_WIDTH), F32)],
        compiler_params=_params("arbitrary", "arbitrary"),
        name="gmlp_spatial",
    )(z, z, ws, gg, bsb)


def _gmlp_sample_kernel(zs_ref, gg_ref, coef_ref, og_ref, vg_ref, *, l):
    vg = _rms(zs_ref[:, C_V:C_Q], gg_ref[l:l + 1, :])
    vg_ref[...] = vg
    og_ref[...] = (zs_ref[:, C_U:C_V] * (coef_ref[0:1, :] * vg + coef_ref[1:2, :])).astype(BF)


def _gmlp_sample(zs, gg, coef, l):
    m = zs.shape[0]
    return pl.pallas_call(
        functools.partial(_gmlp_sample_kernel, l=l),
        out_shape=[jax.ShapeDtypeStruct((m, G_WIDTH), BF), jax.ShapeDtypeStruct((m, G_WIDTH), F32)],
        name="gmlp_sample",
    )(zs, gg, coef)


def _swa_sample_kernel(q_ref, kt_ref, vt_ref, kn_ref, vn_ref, knc_ref, vnc_ref, slope_ref, sink_ref, *rest, l, nreq):
    o_ref, nk_ref, nv_ref = rest[-3:]
    kvw = SWA_KV_WIDTH
    hrow = lax.broadcasted_iota(jnp.int32, (SWA_HEADS, kvw), 0) // SWA_GROUP
    lblk = lax.broadcasted_iota(jnp.int32, (SWA_HEADS, kvw), 1) // SWA_HD
    own = hrow == lblk
    kvh = lax.broadcasted_iota(jnp.int32, (SWA_HEADS, SWA_HD), 0) // SWA_GROUP
    c = lax.broadcasted_iota(jnp.int32, (SWA_HEADS, WINDOW), 1)
    bias = slope_ref[...] * (WINDOW - c).astype(F32)
    sink = sink_ref[:, 0:1]
    pos = lax.broadcasted_iota(jnp.int32, (kvw, WINDOW), 1)
    scale = SWA_HD ** -0.5
    q = q_ref[...]
    qm = jnp.where(own, jnp.concatenate([q] * SWA_KV, axis=2), 0.0).astype(BF)
    kt = kt_ref[...].reshape(nreq, kvw, WINDOW)
    vt = vt_ref[...].reshape(nreq, kvw, WINDOW)
    rows = lambda ref: jnp.stack([ref[r:r + 1, :] for r in range(nreq)]).astype(BF).astype(F32)
    kn, vn = rows(kn_ref), rows(vn_ref)
    s_c = jnp.einsum("rhk,rkc->rhc", qm, kt.astype(BF), preferred_element_type=F32) * scale - bias
    s_n = jnp.sum(qm.astype(F32) * kn, axis=-1, keepdims=True) * scale
    mx = jnp.maximum(jnp.maximum(jnp.max(s_c, axis=-1, keepdims=True), s_n), sink)
    p_c = jnp.exp(s_c - mx)
    p_n = jnp.exp(s_n - mx)
    den = jnp.sum(p_c, axis=-1, keepdims=True) + p_n + jnp.exp(sink - mx)
    p_c = (p_c / den).astype(BF)
    p_n = (p_n / den).astype(BF).astype(F32)
    o_all = jnp.einsum("rhc,rkc->rhk", p_c, vt.astype(BF), preferred_element_type=F32) + p_n * vn
    o = jnp.zeros((nreq, SWA_HEADS, SWA_HD), F32)
    for j in range(SWA_KV):
        o = jnp.where(kvh == j, o_all[:, :, j * SWA_HD:(j + 1) * SWA_HD], o)
    o_ref[...] = o
    last = pos == WINDOW - 1
    for r in range(nreq):
        nk_ref[r] = jnp.where(last, knc_ref[:, r:r + 1], pltpu.roll(kt[r], WINDOW - 1, axis=1)).reshape(SWA_KV, SWA_HD, WINDOW)
        nv_ref[r] = jnp.where(last, vnc_ref[:, r:r + 1], pltpu.roll(vt[r], WINDOW - 1, axis=1)).reshape(SWA_KV, SWA_HD, WINDOW)


def _swa_sample(q3, kt, vt, zs, knc, vnc, slope_b, sink_b, prev_nk, prev_nv, l, nreq):
    m = q3.shape[0]
    cache = pl.BlockSpec((None, nreq, SWA_KV, SWA_HD, WINDOW), lambda i: (l, i, 0, 0, 0))
    newcol = pl.BlockSpec((None, SWA_KV_WIDTH, nreq), lambda i: (i, 0, 0))
    in_specs = [pl.BlockSpec((nreq, SWA_HEADS, SWA_HD), lambda i: (i, 0, 0)), cache, cache,
                pl.BlockSpec((nreq, SWA_KV_WIDTH), lambda i: (i, C_K // SWA_KV_WIDTH)),
                pl.BlockSpec((nreq, SWA_KV_WIDTH), lambda i: (i, C_VV // SWA_KV_WIDTH)),
                newcol, newcol, _resident((SWA_HEADS, LANES)),
                pl.BlockSpec((None, SWA_HEADS, LANES), lambda i: (l, 0, 0))]
    args = [q3, kt, vt, zs, zs, knc, vnc, slope_b, sink_b]
    aliases = {}
    if prev_nk is not None:
        in_specs += [pl.BlockSpec(memory_space=pl.ANY)] * 2
        aliases = {len(args): 1, len(args) + 1: 2}
        args += [prev_nk, prev_nv]
    return pl.pallas_call(
        functools.partial(_swa_sample_kernel, l=l, nreq=nreq),
        grid=(m // nreq,),
        in_specs=in_specs,
        out_specs=[pl.BlockSpec((nreq, SWA_HEADS, SWA_HD), lambda i: (i, 0, 0)), cache, cache],
        out_shape=[jax.ShapeDtypeStruct((m, SWA_HEADS, SWA_HD), F32),
                   jax.ShapeDtypeStruct(kt.shape, F32), jax.ShapeDtypeStruct(vt.shape, F32)],
        input_output_aliases=aliases,
        compiler_params=_params("arbitrary"),
        name="swa_sample",
    )(*args)


def _memkv_kernel(x_ref, g_ref, w_ref, o_ref, xn_ref, *, l):
    @pl.when(pl.program_id(0) == 0)
    def _():
        xn_ref[...] = _rms(x_ref[...], g_ref[l:l + 1, :]).astype(BF)

    o_ref[...] = _dot(xn_ref[...], w_ref[...].astype(BF))


def _memkv(mem, g, w, l, tn):
    m = mem.shape[0]
    return pl.pallas_call(
        functools.partial(_memkv_kernel, l=l),
        grid=(2 * MEM_WIDTH // tn,),
        in_specs=[_resident((m, D_MODEL)), _resident(g.shape),
                  pl.BlockSpec((None, D_MODEL, tn), lambda j: (l, 0, j))],
        out_specs=pl.BlockSpec((m, tn), lambda j: (0, j)),
        out_shape=jax.ShapeDtypeStruct((m, 2 * MEM_WIDTH), F32),
        scratch_shapes=[pltpu.VMEM((m, D_MODEL), BF)],
        compiler_params=_params("arbitrary"),
        name="mem_kv",
    )(mem, g, w)


def _mem_attn_kernel(qa_ref, qb_ref, mk_ref, mv_ref, o_ref):
    for h in range(MEM_HEADS):
        q_ref = qa_ref if h < MEM_HEADS // 2 else qb_ref
        qcols = slice((h % (MEM_HEADS // 2)) * MEM_HD, (h % (MEM_HEADS // 2) + 1) * MEM_HD)
        cols = slice(h * MEM_HD, (h + 1) * MEM_HD)
        s = lax.dot_general(q_ref[:, qcols].astype(BF), mk_ref[:, cols].astype(BF), NT_DIMS,
                            preferred_element_type=F32) * (MEM_HD ** -0.5)
        e = jnp.exp(s - jnp.max(s, axis=-1, keepdims=True))
        p = (e / jnp.sum(e, axis=-1, keepdims=True)).astype(BF)
        o_ref[:, cols] = _dot(p, mv_ref[:, cols].astype(BF)).astype(BF)


def _mem_attn(z, mkv, batch, seq, tq):
    nq = seq // tq
    half = MEM_WIDTH // 2
    qspec = lambda cb: pl.BlockSpec((tq, half), lambda b, i: (b * nq + i, cb))
    kvspec = lambda cb: pl.BlockSpec((MEM_LEN, MEM_WIDTH), lambda b, i: (b, cb))
    return pl.pallas_call(
        _mem_attn_kernel,
        grid=(batch, nq),
        in_specs=[qspec(C_M // half), qspec(C_M // half + 1), kvspec(0), kvspec(1)],
        out_specs=pl.BlockSpec((tq, MEM_WIDTH), lambda b, i: (b * nq + i, 0)),
        out_shape=jax.ShapeDtypeStruct((batch * seq, MEM_WIDTH), BF),
        compiler_params=_params("arbitrary", "arbitrary"),
        name="mem_attn_prompt",
    )(z, z, mkv, mkv)


def _mem_sample_kernel(q_ref, mk_ref, mv_ref, o_ref, *, nreq):
    nrow = MEM_LEN * MEM_HEADS
    own = (lax.broadcasted_iota(jnp.int32, (nreq, MEM_HEADS, nrow), 2) % MEM_HEADS
           == lax.broadcasted_iota(jnp.int32, (nreq, MEM_HEADS, nrow), 1))
    s = jnp.einsum("rhd,rkd->rhk", q_ref[...].astype(BF), mk_ref[...].astype(BF),
                   preferred_element_type=F32) * (MEM_HD ** -0.5)
    s = jnp.where(own, s, -jnp.inf)
    e = jnp.exp(s - jnp.max(s, axis=-1, keepdims=True))
    p = (e / jnp.sum(e, axis=-1, keepdims=True)).astype(BF)
    o_ref[...] = jnp.einsum("rhk,rkd->rhd", p, mv_ref[...].astype(BF), preferred_element_type=F32)


def _mem_sample(q3, mk, mv, l, nreq):
    m = q3.shape[0]
    qspec = pl.BlockSpec((nreq, MEM_HEADS, MEM_HD), lambda i: (i, 0, 0))
    kvspec = pl.BlockSpec((None, nreq, MEM_LEN * MEM_HEADS, MEM_HD), lambda i: (l, i, 0, 0))
    return pl.pallas_call(
        functools.partial(_mem_sample_kernel, nreq=nreq),
        grid=(m // nreq,),
        in_specs=[qspec, kvspec, kvspec],
        out_specs=qspec,
        out_shape=jax.ShapeDtypeStruct((m, MEM_HEADS, MEM_HD), F32),
        compiler_params=_params("arbitrary"),
        name="mem_attn_sample",
    )(q3, mk, mv)


def _merge_kernel(xn_ref, og_ref, oa_ref, om_ref, xns_ref, ogs_ref, oas_ref, oms_ref,
                  wg0_ref, wg1_ref, wg2_ref, wbg_ref, wba_ref, wbm_ref, out_ref, outs_ref,
                  bg0_ref, bg1_ref, bg2_ref, bbg_ref, bba_ref, bbm_ref):
    pairs = [(wg0_ref, bg0_ref), (wg1_ref, bg1_ref), (wg2_ref, bg2_ref),
             (wbg_ref, bbg_ref), (wba_ref, bba_ref), (wbm_ref, bbm_ref)]

    @pl.when(_first_inner())
    def _():
        for w_ref, b_ref in pairs:
            b_ref[...] = w_ref[...].astype(BF)

    def merged(xn, og, oa, om):
        def branch(bg_ref, o, bb_ref):
            return jax.nn.sigmoid(_dot(xn, bg_ref[...])) * _dot(o, bb_ref[...])
        return (branch(bg0_ref, og, bbg_ref) + branch(bg1_ref, oa, bba_ref) + branch(bg2_ref, om, bbm_ref)).astype(BF)

    out_ref[...] = merged(xn_ref[...], og_ref[...], oa_ref[...], om_ref[...])

    @pl.when(_last_inner())
    def _():
        outs_ref[...] = merged(xns_ref[...], ogs_ref[...], oas_ref[...], oms_ref[...])


def _merge(xn, og, oa, om, xns, ogs, oas, oms, w_in, wbg, wba, wbm, l, tm, tn):
    m, ms = xn.shape[0], xns.shape[0]
    nt = D_MODEL // tn
    row = lambda n: pl.BlockSpec((tm, n), lambda j, i: (i, 0))
    gate = lambda b: pl.BlockSpec((None, D_MODEL, tn), lambda j, i: (l, 0, C_G // tn + b * nt + j))
    col = lambda n: pl.BlockSpec((None, n, tn), lambda j, i: (l, 0, j))
    widths = (G_WIDTH, SWA_WIDTH, MEM_WIDTH)
    return pl.pallas_call(
        _merge_kernel,
        grid=(nt, m // tm),
        in_specs=[row(D_MODEL)] + [row(n) for n in widths]
                 + [_resident((ms, D_MODEL))] + [_resident((ms, n)) for n in widths]
                 + [gate(0), gate(1), gate(2)] + [col(n) for n in widths],
        out_specs=[pl.BlockSpec((tm, tn), lambda j, i: (i, j)), pl.BlockSpec((ms, tn), lambda j, i: (0, j))],
        out_shape=[jax.ShapeDtypeStruct((m, D_MODEL), BF), jax.ShapeDtypeStruct((ms, D_MODEL), BF)],
        scratch_shapes=[pltpu.VMEM((D_MODEL, tn), BF)] * 3 + [pltpu.VMEM((n, tn), BF) for n in widths],
        compiler_params=_params("arbitrary", "arbitrary"),
        name="merge",
    )(xn, og, oa, om, xns, ogs, oas, oms, w_in, w_in, w_in, wbg, wba, wbm)


def _out_kernel(x_ref, mg_ref, xs_ref, mgs_ref, w_ref, g_ref, xo_ref, xn_ref, xos_ref, xns_ref, wbf_ref, *, l):
    @pl.when(pl.program_id(0) == 0)
    def _():
        wbf_ref[...] = w_ref[...].astype(BF)

    g = g_ref[l:l + 1, :]
    x = x_ref[...] + _dot(mg_ref[...], wbf_ref[...])
    xo_ref[...] = x
    xn_ref[...] = _rms(x, g).astype(BF)

    @pl.when(pl.program_id(0) == pl.num_programs(0) - 1)
    def _():
        xs = xs_ref[...] + _dot(mgs_ref[...], wbf_ref[...])
        xos_ref[...] = xs
        xns_ref[...] = _rms(xs, g).astype(BF)


def _out_proj(x, merged, xs, mergeds, w_out, g, l, tm):
    m, ms = x.shape[0], xs.shape[0]
    row = pl.BlockSpec((tm, D_MODEL), lambda i: (i, 0))
    srow = pl.BlockSpec((ms, D_MODEL), lambda i: (0, 0))
    return pl.pallas_call(
        functools.partial(_out_kernel, l=l),
        grid=(m // tm,),
        in_specs=[row, row, _resident((ms, D_MODEL)), _resident((ms, D_MODEL)),
                  pl.BlockSpec((None, D_MODEL, D_MODEL), lambda i: (l, 0, 0), pipeline_mode=pl.Buffered(1)),
                  _resident(g.shape)],
        out_specs=[row, row, srow, srow],
        out_shape=[jax.ShapeDtypeStruct((m, D_MODEL), F32), jax.ShapeDtypeStruct((m, D_MODEL), BF),
                   jax.ShapeDtypeStruct((ms, D_MODEL), F32), jax.ShapeDtypeStruct((ms, D_MODEL), BF)],
        scratch_shapes=[pltpu.VMEM((D_MODEL, D_MODEL), BF)],
        compiler_params=_params("arbitrary"),
        name="out_proj",
    )(x, merged, xs, mergeds, w_out, g)


def _up_kernel(xn_ref, xns_ref, wa_ref, wb_ref, cwa_ref, cwb_ref, cba_ref, cbb_ref, sta_ref, stb_ref,
               g_ref, ca_ref, cb_ref, gs_ref, has_ref, hbs_ref, ha_ref, hb_ref, ba_ref, bb_ref,
               *, l, tm, tiles_per_seq):
    @pl.when(_first_inner())
    def _():
        ba_ref[...] = wa_ref[...].astype(BF)
        bb_ref[...] = wb_ref[...].astype(BF)

    @pl.when(pl.program_id(1) % tiles_per_seq == 0)
    def _():
        ha_ref[...] = jnp.zeros(ha_ref.shape, F32)
        hb_ref[...] = jnp.zeros(hb_ref.shape, F32)

    top = lax.broadcasted_iota(jnp.int32, (SUBLANES, ha_ref.shape[1]), 0)
    xn = xn_ref[...]

    def conv(h, prev_ref, cw_ref, bias_ref, tail_ref):
        prev = prev_ref[...]
        acc = bias_ref[l:l + 1, :]
        for j in range(CONV_W - 1):
            back = CONV_W - 1 - j
            rolled = pltpu.roll(h, back, axis=0)
            head = jnp.where(top < back, pltpu.roll(prev, back, axis=0), rolled[0:SUBLANES])
            acc = acc + jnp.concatenate([head, rolled[SUBLANES:]], axis=0) * cw_ref[j:j + 1, :]
        tail = h[tm - SUBLANES:]
        tail_ref[...] = tail
        prev_ref[...] = tail
        return acc + h * cw_ref[CONV_W - 1:CONV_W, :]

    a = conv(_dot(xn, ba_ref[...]), ha_ref, cwa_ref, cba_ref, ca_ref)
    b = conv(_dot(xn, bb_ref[...]), hb_ref, cwb_ref, cbb_ref, cb_ref)
    g_ref[...] = (jax.nn.gelu(a) * b).astype(BF)

    @pl.when(_last_inner())
    def _():
        xns = xns_ref[...]

        def conv_s(bw_ref, cw_ref, bias_ref, st_ref, h_out_ref):
            h = _dot(xns, bw_ref[...])
            h_out_ref[...] = h
            return (bias_ref[l:l + 1, :] + st_ref[:, 0, :] * cw_ref[0:1, :] + st_ref[:, 1, :] * cw_ref[1:2, :]
                    + h * cw_ref[2:3, :])

        a_s = conv_s(ba_ref, cwa_ref, cba_ref, sta_ref, has_ref)
        b_s = conv_s(bb_ref, cwb_ref, cbb_ref, stb_ref, hbs_ref)
        gs_ref[...] = (jax.nn.gelu(a_s) * b_s).astype(BF)


def _up_conv(xn, xns, w_up, conv_w, conv_b, state, l, batch, seq, tm, tn):
    m, ms = xn.shape[0], xns.shape[0]
    nt = D_FF // tn
    nrow = m // tm
    tiles_per_seq = seq // tm
    half = lambda off: (lambda j, i: (l, 0, off * nt + j))
    wspec = lambda off: pl.BlockSpec((None, D_MODEL, tn), half(off))
    cwspec = lambda off: pl.BlockSpec((None, CONV_W, tn), half(off))
    cbspec = lambda off: pl.BlockSpec((DEPTH, tn), lambda j, i: (0, off * nt + j))
    stspec = lambda off: pl.BlockSpec((None, ms, CONV_W - 1, tn), lambda j, i: (l, 0, 0, off * nt + j))
    tail = pl.BlockSpec((SUBLANES, tn), lambda j, i: (i // tiles_per_seq, j))
    scol = pl.BlockSpec((ms, tn), lambda j, i: (0, j))
    return pl.pallas_call(
        functools.partial(_up_kernel, l=l, tm=tm, tiles_per_seq=tiles_per_seq),
        grid=(nt, nrow),
        in_specs=[pl.BlockSpec((tm, D_MODEL), lambda j, i: (i, 0)), _resident((ms, D_MODEL)),
                  wspec(0), wspec(1), cwspec(0), cwspec(1), cbspec(0), cbspec(1), stspec(0), stspec(1)],
        out_specs=[pl.BlockSpec((tm, tn), lambda j, i: (i, j)), tail, tail, scol, scol, scol],
        out_shape=[jax.ShapeDtypeStruct((m, D_FF), BF),
                   jax.ShapeDtypeStruct((batch * SUBLANES, D_FF), F32),
                   jax.ShapeDtypeStruct((batch * SUBLANES, D_FF), F32),
                   jax.ShapeDtypeStruct((ms, D_FF), BF),
                   jax.ShapeDtypeStruct((ms, D_FF), F32), jax.ShapeDtypeStruct((ms, D_FF), F32)],
        scratch_shapes=[pltpu.VMEM((SUBLANES, tn), F32)] * 2 + [pltpu.VMEM((D_MODEL, tn), BF)] * 2,
        compiler_params=_params("arbitrary", "arbitrary"),
        name="up_conv_glu",
    )(xn, xns, w_up, w_up, conv_w, conv_w, conv_b, conv_b, state, state)


def _down_kernel(x_ref, g_ref, xs_ref, gs_ref, w_ref, gain_ref, *rest, tk, ncast, emit_x):
    wbf_ref = rest[-1]
    outs = rest[:-1]
    s = pl.program_id(0)

    @pl.when(s < ncast)
    def _():
        wbf_ref[pl.ds(pl.multiple_of(s * tk, tk), tk), :] = w_ref[...].astype(BF)

    def tile(x, g, o_refs):
        x = x + _dot(g, wbf_ref[...])
        if emit_x:
            o_refs[0][...] = x
        o_refs[-1][...] = _rms(x, gain_ref[...]).astype(o_refs[-1].dtype)

    nout = len(outs) // 2

    @pl.when(s >= ncast)
    def _():
        tile(x_ref[...], g_ref[...], outs[:nout])

    @pl.when(s == pl.num_programs(0) - 1)
    def _():
        tile(xs_ref[...], gs_ref[...], outs[nout:])


def _down_proj(x, g, xs, gs, w_down, gain, l, tm, tk, norm_dtype, emit_x):
    m, ms = x.shape[0], xs.shape[0]
    ncast = D_FF // tk
    rows = lambda n: pl.BlockSpec((tm, n), lambda s: (jnp.maximum(s - ncast, 0), 0))
    fixed = lambda n: pl.BlockSpec((ms, n), lambda s: (0, 0))
    dts = ([F32] if emit_x else []) + [norm_dtype]
    return pl.pallas_call(
        functools.partial(_down_kernel, tk=tk, ncast=ncast, emit_x=emit_x),
        grid=(ncast + m // tm,),
        in_specs=[rows(D_MODEL), rows(D_FF), _resident((ms, D_MODEL)), _resident((ms, D_FF)),
                  pl.BlockSpec((None, tk, D_MODEL), lambda s: (l, jnp.minimum(s, ncast - 1), 0)),
                  _resident((1, D_MODEL))],
        out_specs=[rows(D_MODEL) for _ in dts] + [fixed(D_MODEL) for _ in dts],
        out_shape=[jax.ShapeDtypeStruct((m, D_MODEL), dt) for dt in dts]
                  + [jax.ShapeDtypeStruct((ms, D_MODEL), dt) for dt in dts],
        scratch_shapes=[pltpu.VMEM((D_FF, D_MODEL), BF)],
        compiler_params=_params("arbitrary"),
        name="down_proj",
    )(x, g, xs, gs, w_down, gain)


def _conv_state_kernel(st_ref, *rest, nt):
    o_ref = rest[-1]
    first_half = pl.program_id(0) < nt
    for l in range(DEPTH):
        ha_ref, hb_ref = rest[2 * l], rest[2 * l + 1]
        o_ref[l, :, 0, :] = st_ref[l, :, CONV_W - 2, :]
        o_ref[l, :, 1, :] = jnp.where(first_half, ha_ref[...], hb_ref[...])


def _conv_state(state, halves, tn):
    depth, ms = state.shape[:2]
    nt = D_FF // tn
    blk = pl.BlockSpec((depth, ms, CONV_W - 1, tn), lambda j: (0, 0, 0, j))
    a_spec = pl.BlockSpec((ms, tn), lambda j: (0, jnp.minimum(j, nt - 1)))
    b_spec = pl.BlockSpec((ms, tn), lambda j: (0, jnp.maximum(j - nt, 0)))
    return pl.pallas_call(
        functools.partial(_conv_state_kernel, nt=nt),
        grid=(2 * nt,),
        in_specs=[blk] + [a_spec, b_spec] * depth,
        out_specs=blk,
        out_shape=jax.ShapeDtypeStruct(state.shape, F32),
        compiler_params=_params("arbitrary"),
        name="conv_state_sample",
    )(state, *halves)


def _layer(l, last, next_gain, xp, xs, xn, xns, mem, batch, seq, kt, vt, mk4, mv4, state_conv,
           prev_nk, prev_nv, slope_b, sink_b,
           norm_mix_g, w_in, gmlp_norm_g, gmlp_ws, gmlp_bs, attn_sinks, mem_norm_g, w_mem_kv,
           w_br_g, w_br_a, w_br_m, w_out, norm_ffn_g, w_up, conv_w, conv_b, w_down):
    dec = xs.shape[0]
    nreq = 8
    z, oa, zs = _front(attn_sinks, xn, xns, w_in, l, seq, 512)

    bs = gmlp_bs[l]
    bsb = jnp.repeat(bs.T, G_GDIM, axis=1)
    og, vg_last = _gmlp(z, gmlp_ws, gmlp_norm_g, bsb, l, batch, seq, 512)
    coef = jnp.stack([jnp.repeat(gmlp_ws[l, :, 0, 0], G_GDIM), jnp.repeat(bs[:, 0], G_GDIM)])
    ogs, vgs = _gmlp_sample(zs, gmlp_norm_g, coef, l)

    q3 = zs[:, C_Q:C_K].reshape(dec, SWA_HEADS, SWA_HD)
    newcol = lambda a, b: zs[:, a:b].reshape(dec // nreq, nreq, SWA_KV_WIDTH).transpose(0, 2, 1)
    oas, nk, nv = _swa_sample(q3, kt, vt, zs, newcol(C_K, C_VV), newcol(C_VV, C_M), slope_b, sink_b,
                              prev_nk, prev_nv, l, nreq)

    mkv = _memkv(mem, mem_norm_g, w_mem_kv, l, 256)
    om = _mem_attn(z, mkv, batch, seq, 512)
    oms = _mem_sample(zs[:, C_M:C_G].reshape(dec, MEM_HEADS, MEM_HD), mk4, mv4, l, nreq)

    merged, mergeds = _merge(xn, og, oa, om, xns, ogs, oas.reshape(dec, SWA_WIDTH).astype(BF),
                             oms.reshape(dec, MEM_WIDTH).astype(BF), w_in, w_br_g, w_br_a, w_br_m, l, 1024, 256)
    xp, xn2, xs, xn2s = _out_proj(xp, merged, xs, mergeds, w_out, norm_ffn_g, l, 256)
    g, ca, cb, gs, has, hbs = _up_conv(xn2, xn2s, w_up, conv_w, conv_b, state_conv, l, batch, seq, 1024, 512)
    outs = _down_proj(xp, g, xs, gs, w_down, next_gain, l, 256, 256, F32 if last else BF, not last)

    tail = lambda t: t.reshape(batch, SUBLANES, D_FF)[:, SUBLANES - (CONV_W - 1):]
    zb = z.reshape(batch, seq, C_G)[:, seq - WINDOW:]
    st = dict(
        pk=zb[:, :, C_K:C_VV].reshape(batch, WINDOW, SWA_KV, SWA_HD),
        pv=zb[:, :, C_VV:C_M].reshape(batch, WINDOW, SWA_KV, SWA_HD),
        mk=mkv[:, :MEM_WIDTH].reshape(batch, MEM_LEN, MEM_HEADS, MEM_HD),
        mv=mkv[:, MEM_WIDTH:].reshape(batch, MEM_LEN, MEM_HEADS, MEM_HD),
        gvp=vg_last.reshape(batch, CHUNK, G_WIDTH),
        gvs=vgs.reshape(dec, 1, G_WIDTH),
        cvp=jnp.concatenate([tail(ca), tail(cb)], axis=-1),
        cvs=(has, hbs),
    )
    return outs, nk, nv, st


def kernel(x_prompt, x_sample, cache_swa_k, cache_swa_v, cache_mem_k, cache_mem_v, state_conv, mem_prompt, norm_mix_g, w_in, gmlp_norm_g, gmlp_ws, gmlp_bs, attn_sinks, mem_norm_g, w_mem_kv, w_br_g, w_br_a, w_br_m, w_out, norm_ffn_g, w_up, conv_w, conv_b, w_down, final_norm_g):
    batch, seq, _ = x_prompt.shape
    dec = x_sample.shape[0]
    assert x_sample.shape[1] == 1 and PAST_LEN % CHUNK == 0 and PAST_LEN >= WINDOW
    assert seq % 2048 == 0 and seq >= WINDOW and (seq - 1) // CHUNK * CHUNK == seq - CHUNK
    xp = x_prompt.reshape(batch * seq, D_MODEL)
    xs = x_sample.reshape(dec, D_MODEL)
    mem = mem_prompt.reshape(batch * MEM_LEN, D_MODEL)
    kt = cache_swa_k.transpose(0, 1, 3, 4, 2)
    vt = cache_swa_v.transpose(0, 1, 3, 4, 2)
    mk4 = cache_mem_k.reshape(DEPTH, dec, MEM_LEN * MEM_HEADS, MEM_HD)
    mv4 = cache_mem_v.reshape(DEPTH, dec, MEM_LEN * MEM_HEADS, MEM_HD)
    slope_b = jnp.broadcast_to(jnp.asarray(SLOPES, F32)[:, None], (SWA_HEADS, LANES))
    sink_b = jnp.broadcast_to(attn_sinks[:, :, None], (DEPTH, SWA_HEADS, LANES))
    nk = nv = None
    sts = []
    xn, xns = _norm(xp, xs, norm_mix_g, 0, BF, 512)
    for l in range(DEPTH):
        last = l == DEPTH - 1
        next_gain = (final_norm_g if last else norm_mix_g[l + 1]).reshape(1, D_MODEL)
        outs, nk, nv, st = _layer(l, last, next_gain, xp, xs, xn, xns, mem, batch, seq, kt, vt, mk4, mv4, state_conv,
                                  nk, nv, slope_b, sink_b,
                                  norm_mix_g, w_in, gmlp_norm_g, gmlp_ws, gmlp_bs, attn_sinks, mem_norm_g,
                                  w_mem_kv, w_br_g, w_br_a, w_br_m, w_out, norm_ffn_g, w_up, conv_w, conv_b, w_down)
        sts.append(st)
        if last:
            y_prompt, y_sample = outs
        else:
            xp, xn, xs, xns = outs
    stack = lambda key: jnp.stack([s[key] for s in sts])
    return (y_prompt.reshape(batch, seq, D_MODEL), y_sample.reshape(dec, 1, D_MODEL), stack("pk"), stack("pv"),
            nk.transpose(0, 1, 4, 2, 3), nv.transpose(0, 1, 4, 2, 3),
            stack("mk"), stack("mv"), stack("gvp"), stack("gvs"), stack("cvp"),
            _conv_state(state_conv, [h for s in sts for h in s["cvs"]], 512))
```

```python
import functools
import math

import jax
import jax.numpy as jnp
from jax import lax
from jax.experimental import pallas as pl
from jax.experimental.pallas import tpu as pltpu

D_MODEL = 2048
DEPTH = 2
PAST_LEN = 8192
MEM_LEN = 256
CHUNK = 128
G_GROUPS = 8
G_WIDTH = 768
G_GDIM = G_WIDTH // G_GROUPS
SWA_HEADS = 12
SWA_KV = 4
SWA_GROUP = SWA_HEADS // SWA_KV
SWA_HD = 64
SWA_WIDTH = SWA_HEADS * SWA_HD
SWA_KV_WIDTH = SWA_KV * SWA_HD
WINDOW = 128
MEM_HEADS = 4
MEM_HD = 128
MEM_WIDTH = MEM_HEADS * MEM_HD
N_BRANCH = 3
D_FF = 5632
CONV_W = 3
EPS = 1e-6

LANES = 128
SUBLANES = 8
VMEM_LIMIT = 56 * 1024 * 1024

C_U, C_V, C_Q, C_K, C_VV, C_M, C_G = 0, 768, 1536, 2304, 2560, 2816, 3328
ZT = 256
N_ACT = C_Q // ZT
assert all(c % ZT == 0 for c in (C_V, C_Q, C_K, C_VV, C_M, C_G))

BF = jnp.bfloat16
F32 = jnp.float32
NT_DIMS = (((1,), (1,)), ((), ()))


def _alibi_slopes(n):
    p = 2 ** int(math.floor(math.log2(n)))
    base = [2.0 ** (-8.0 * (i + 1) / p) for i in range(p)]
    extra = [2.0 ** (-8.0 * (2 * i + 1) / (2 * p)) for i in range(n - p)]
    return base + extra


SLOPES = _alibi_slopes(SWA_HEADS)


def _params(*sem):
    return pltpu.CompilerParams(dimension_semantics=sem, vmem_limit_bytes=VMEM_LIMIT)


def _resident(shape):
    return pl.BlockSpec(shape, lambda *_: (0,) * len(shape), pipeline_mode=pl.Buffered(1))


def _rms(x, g):
    ms = jnp.mean(x * x, axis=-1, keepdims=True)
    return x * lax.rsqrt(ms + EPS) * g


def _dot(a, b):
    return jnp.dot(a, b, preferred_element_type=F32)


def _first_inner():
    return pl.program_id(1) == 0


def _last_inner():
    return pl.program_id(1) == pl.num_programs(1) - 1


def _norm_kernel(x_ref, xs_ref, g_ref, o_ref, os_ref, *, l):
    g = g_ref[l:l + 1, :]
    o_ref[...] = _rms(x_ref[...], g).astype(o_ref.dtype)

    @pl.when(pl.program_id(0) == pl.num_programs(0) - 1)
    def _():
        os_ref[...] = _rms(xs_ref[...], g).astype(os_ref.dtype)


def _norm(x, xs, g, l, dtype, tm):
    m, ms = x.shape[0], xs.shape[0]
    row = pl.BlockSpec((tm, D_MODEL), lambda i: (i, 0))
    return pl.pallas_call(
        functools.partial(_norm_kernel, l=l),
        grid=(m // tm,),
        in_specs=[row, _resident((ms, D_MODEL)), _resident(g.shape)],
        out_specs=[row, pl.BlockSpec((ms, D_MODEL), lambda i: (0, 0))],
        out_shape=[jax.ShapeDtypeStruct((m, D_MODEL), dtype), jax.ShapeDtypeStruct((ms, D_MODEL), dtype)],
        compiler_params=_params("arbitrary"),
        name="rmsnorm",
    )(x, xs, g)


NZ = C_G // ZT


QKV_TILES = range(C_Q // ZT, C_M // ZT)
QKV_W = C_M - C_Q


def _swa_bias_table(bias_ref):
    r = lax.broadcasted_iota(jnp.int32, (WINDOW, 2 * WINDOW), 0)
    c = lax.broadcasted_iota(jnp.int32, (WINDOW, 2 * WINDOW), 1)
    dist = r + WINDOW - c
    valid = (dist >= 0) & (dist <= WINDOW)
    distf = dist.astype(F32)
    for h in range(SWA_HEADS):
        j, g = divmod(h, SWA_GROUP)
        pen = -SLOPES[h] * distf
        bias_ref[j, g * WINDOW:(g + 1) * WINDOW, :] = jnp.where(valid, pen, -jnp.inf)
        bias_ref[SWA_KV + j, g * WINDOW:(g + 1) * WINDOW, :] = jnp.where(valid & (c >= WINDOW), pen, -jnp.inf)


def _front_kernel(sink_ref, xn_ref, xs_ref, w_ref, ws_ref, gg_ref, bsb_ref, mkv_ref,
                  kv_ref, oa_ref, og_ref, vg_ref, om_ref, zs_ref,
                  wbf_ref, qkv_ref, zt_ref, carry_ref, bias_ref, *, l, tm, tiles_per_seq):
    s = pl.program_id(0)
    rows = SWA_GROUP * WINDOW
    kcol, vcol = C_K - C_Q, C_VV - C_Q
    mcol = C_Q

    @pl.when(s < NZ)
    def _():
        wbf_ref[s] = w_ref[...].astype(BF)

    @pl.when(s == NZ)
    def _():
        _swa_bias_table(bias_ref)
        carry_ref[...] = jnp.zeros(carry_ref.shape, F32)

    def column(x, c):
        h = _dot(x, wbf_ref[c])
        return jax.nn.gelu(h) if c < N_ACT else h

    @pl.when(s >= NZ)
    def _():
        x = xn_ref[...]
        for c in QKV_TILES:
            qkv_ref[:, c * ZT - C_Q:(c + 1) * ZT - C_Q] = column(x, c)
        kv_ref[...] = qkv_ref[:, kcol:]

        def project(c, col):
            def task():
                zt_ref[:, col:col + ZT] = column(x, c)
            return task

        tasks = [project(c, c * ZT) for c in range(N_ACT)]
        tasks += [project(c, mcol + (c - C_M // ZT) * ZT) for c in range(C_M // ZT, NZ)]

        r = lax.broadcasted_iota(jnp.int32, (CHUNK, CHUNK), 0)
        cc = lax.broadcasted_iota(jnp.int32, (CHUNK, CHUNK), 1)
        lane = lax.broadcasted_iota(jnp.int32, (CHUNK, LANES), 1)

        def gmlp_chunk(n):
            def task():
                wtri = [jnp.where(r >= cc, ws_ref[g], 0.0).astype(BF) for g in range(G_GROUPS)]
                crow = slice(n * CHUNK, (n + 1) * CHUNK)
                vg = _rms(zt_ref[crow, C_V:C_Q], gg_ref[l:l + 1, :])
                if n == tm // CHUNK - 1:
                    vg_ref[...] = vg
                for t in range(G_TILES):
                    cols = slice(t * LANES, (t + 1) * LANES)
                    wst = jnp.concatenate([wtri[GA[t]], wtri[GA[t] + 1]], axis=0)
                    both = _dot(wst, vg[:, cols].astype(BF))
                    mix = jnp.where(lane < BND[t], both[:CHUNK], both[CHUNK:]) + bsb_ref[:, cols]
                    og_ref[crow, cols] = (zt_ref[crow, cols] * mix).astype(BF)
            return task

        tasks += [gmlp_chunk(n) for n in range(tm // CHUNK)]

        def mem_head(h):
            def task():
                cols = slice(h * MEM_HD, (h + 1) * MEM_HD)
                q = zt_ref[:, mcol + h * MEM_HD:mcol + (h + 1) * MEM_HD].astype(BF)
                sc = lax.dot_general(q, mkv_ref[:, cols].astype(BF), NT_DIMS,
                                     preferred_element_type=F32) * (MEM_HD ** -0.5)
                e = jnp.exp(sc - jnp.max(sc, axis=-1, keepdims=True))
                p = (e / jnp.sum(e, axis=-1, keepdims=True)).astype(BF)
                vcols = slice(MEM_WIDTH + h * MEM_HD, MEM_WIDTH + (h + 1) * MEM_HD)
                om_ref[:, cols] = _dot(p, mkv_ref[:, vcols].astype(BF)).astype(BF)
            return task

        tasks += [mem_head(h) for h in range(MEM_HEADS)]

        units = [(b, j) for b in range(tm // WINDOW) for j in range(SWA_KV)]
        assert len(tasks) == len(units)
        head = lax.broadcasted_iota(jnp.int32, (rows, 1), 0) // WINDOW
        fresh = ((s - NZ) % tiles_per_seq == 0).astype(jnp.int32) * SWA_KV
        for task, (b, j) in zip(tasks, units):
            task()
            own = slice(b * WINDOW, (b + 1) * WINDOW)
            kvh = lambda base: slice(base + j * SWA_HD, base + (j + 1) * SWA_HD)
            if b == 0:
                kprev, vprev = carry_ref[:, kvh(0)], carry_ref[:, kvh(SWA_KV_WIDTH)]
            else:
                prev = slice((b - 1) * WINDOW, b * WINDOW)
                kprev, vprev = qkv_ref[prev, kvh(kcol)], qkv_ref[prev, kvh(vcol)]
            kj = jnp.concatenate([kprev, qkv_ref[own, kvh(kcol)]], axis=0).astype(BF)
            vj = jnp.concatenate([vprev, qkv_ref[own, kvh(vcol)]], axis=0).astype(BF)
            hs = range(j * SWA_GROUP, (j + 1) * SWA_GROUP)
            q = jnp.concatenate([qkv_ref[own, h * SWA_HD:(h + 1) * SWA_HD] for h in hs], axis=0).astype(BF)
            sink = jnp.full((rows, 1), sink_ref[l, hs[-1]], F32)
            for g in range(SWA_GROUP - 1):
                sink = jnp.where(head == g, sink_ref[l, hs[g]], sink)
            bias = bias_ref[(fresh if b == 0 else 0) + j]
            sc = lax.dot_general(q, kj, NT_DIMS, preferred_element_type=F32) * (SWA_HD ** -0.5) + bias
            mx = jnp.maximum(jnp.max(sc, axis=-1, keepdims=True), sink)
            p = jnp.exp(sc - mx)
            den = jnp.sum(p, axis=-1, keepdims=True) + jnp.exp(sink - mx)
            o = _dot(p.astype(BF), vj) / den
            for g, h in enumerate(hs):
                oa_ref[own, h * SWA_HD:(h + 1) * SWA_HD] = o[g * WINDOW:(g + 1) * WINDOW].astype(BF)
        last = slice(tm - WINDOW, tm)
        carry_ref[...] = qkv_ref[last, kcol:]

    @pl.when(s == pl.num_programs(0) - 1)
    def _():
        xs = xs_ref[...]
        for c in range(NZ):
            zs_ref[:, c * ZT:(c + 1) * ZT] = column(xs, c)


def _front(sinks, xn, xs, w_in, ws, gg, bsb, mkv, l, seq, tm):
    m, ms = xn.shape[0], xs.shape[0]
    tiles_per_seq = seq // tm
    tile = lambda s: jnp.maximum(s - NZ, 0)
    rows = lambda n: pl.BlockSpec((tm, n), lambda s: (tile(s), 0))
    per_seq = lambda r, n: pl.BlockSpec((r, n), lambda s: (tile(s) // tiles_per_seq, 0))
    outs = [(2 * SWA_KV_WIDTH, F32), (SWA_WIDTH, BF), (G_WIDTH, BF)]
    return pl.pallas_call(
        functools.partial(_front_kernel, l=l, tm=tm, tiles_per_seq=tiles_per_seq),
        grid=(NZ + m // tm,),
        in_specs=[pl.BlockSpec(memory_space=pltpu.SMEM), rows(D_MODEL), _resident((ms, D_MODEL)),
                  pl.BlockSpec((None, D_MODEL, ZT), lambda s: (l, 0, jnp.minimum(s, NZ - 1))),
                  pl.BlockSpec((None, G_GROUPS, CHUNK, CHUNK), lambda s: (l, 0, 0, 0)),
                  _resident(gg.shape), _resident((CHUNK, G_WIDTH)), per_seq(MEM_LEN, 2 * MEM_WIDTH)],
        out_specs=[rows(n) for n, _ in outs] + [per_seq(CHUNK, G_WIDTH), rows(MEM_WIDTH),
                                               pl.BlockSpec((ms, C_G), lambda s: (0, 0))],
        out_shape=[jax.ShapeDtypeStruct((m, n), dt) for n, dt in outs]
                  + [jax.ShapeDtypeStruct((m // seq * CHUNK, G_WIDTH), F32), jax.ShapeDtypeStruct((m, MEM_WIDTH), BF),
                     jax.ShapeDtypeStruct((ms, C_G), F32)],
        scratch_shapes=[pltpu.VMEM((NZ, D_MODEL, ZT), BF), pltpu.VMEM((tm, QKV_W), F32),
                        pltpu.VMEM((tm, C_Q + MEM_WIDTH), F32), pltpu.VMEM((WINDOW, 2 * SWA_KV_WIDTH), F32),
                        pltpu.VMEM((2 * SWA_KV, SWA_GROUP * WINDOW, 2 * WINDOW), F32)],
        compiler_params=_params("arbitrary"),
        name="front",
    )(sinks, xn, xs, w_in, ws, gg, bsb, mkv)


G_TILES = G_WIDTH // LANES
GA = [(t * LANES) // G_GDIM for t in range(G_TILES)]
BND = [(GA[t] + 1) * G_GDIM - t * LANES for t in range(G_TILES)]
assert all(0 < b < LANES and (GA[t] + 2) * G_GDIM >= (t + 1) * LANES for t, b in enumerate(BND))


def _gmlp_sample_kernel(zs_ref, gg_ref, coef_ref, og_ref, vg_ref, *, l):
    vg = _rms(zs_ref[:, C_V:C_Q], gg_ref[l:l + 1, :])
    vg_ref[...] = vg
    og_ref[...] = (zs_ref[:, C_U:C_V] * (coef_ref[0:1, :] * vg + coef_ref[1:2, :])).astype(BF)


def _gmlp_sample(zs, gg, coef, l):
    m = zs.shape[0]
    return pl.pallas_call(
        functools.partial(_gmlp_sample_kernel, l=l),
        out_shape=[jax.ShapeDtypeStruct((m, G_WIDTH), BF), jax.ShapeDtypeStruct((m, G_WIDTH), F32)],
        name="gmlp_sample",
    )(zs, gg, coef)


def _swa_sample_kernel(q_ref, kt_ref, vt_ref, kn_ref, vn_ref, knc_ref, vnc_ref, slope_ref, sink_ref, *rest, l, nreq):
    o_ref, nk_ref, nv_ref = rest[-3:]
    kvw = SWA_KV_WIDTH
    hrow = lax.broadcasted_iota(jnp.int32, (SWA_HEADS, kvw), 0) // SWA_GROUP
    lblk = lax.broadcasted_iota(jnp.int32, (SWA_HEADS, kvw), 1) // SWA_HD
    own = hrow == lblk
    kvh = lax.broadcasted_iota(jnp.int32, (SWA_HEADS, SWA_HD), 0) // SWA_GROUP
    c = lax.broadcasted_iota(jnp.int32, (SWA_HEADS, WINDOW), 1)
    bias = slope_ref[...] * (WINDOW - c).astype(F32)
    sink = sink_ref[:, 0:1]
    pos = lax.broadcasted_iota(jnp.int32, (kvw, WINDOW), 1)
    scale = SWA_HD ** -0.5
    q = q_ref[...]
    qm = jnp.where(own, jnp.concatenate([q] * SWA_KV, axis=2), 0.0).astype(BF)
    kt = kt_ref[...].reshape(nreq, kvw, WINDOW)
    vt = vt_ref[...].reshape(nreq, kvw, WINDOW)
    rows = lambda ref: jnp.stack([ref[r:r + 1, :] for r in range(nreq)]).astype(BF).astype(F32)
    kn, vn = rows(kn_ref), rows(vn_ref)
    s_c = jnp.einsum("rhk,rkc->rhc", qm, kt.astype(BF), preferred_element_type=F32) * scale - bias
    s_n = jnp.sum(qm.astype(F32) * kn, axis=-1, keepdims=True) * scale
    mx = jnp.maximum(jnp.maximum(jnp.max(s_c, axis=-1, keepdims=True), s_n), sink)
    p_c = jnp.exp(s_c - mx)
    p_n = jnp.exp(s_n - mx)
    den = jnp.sum(p_c, axis=-1, keepdims=True) + p_n + jnp.exp(sink - mx)
    p_c = (p_c / den).astype(BF)
    p_n = (p_n / den).astype(BF).astype(F32)
    o_all = jnp.einsum("rhc,rkc->rhk", p_c, vt.astype(BF), preferred_element_type=F32) + p_n * vn
    o = jnp.zeros((nreq, SWA_HEADS, SWA_HD), F32)
    for j in range(SWA_KV):
        o = jnp.where(kvh == j, o_all[:, :, j * SWA_HD:(j + 1) * SWA_HD], o)
    o_ref[...] = o
    last = pos == WINDOW - 1
    for r in range(nreq):
        nk_ref[r] = jnp.where(last, knc_ref[:, r:r + 1], pltpu.roll(kt[r], WINDOW - 1, axis=1)).reshape(SWA_KV, SWA_HD, WINDOW)
        nv_ref[r] = jnp.where(last, vnc_ref[:, r:r + 1], pltpu.roll(vt[r], WINDOW - 1, axis=1)).reshape(SWA_KV, SWA_HD, WINDOW)


def _swa_sample(q3, kt, vt, zs, knc, vnc, slope_b, sink_b, prev_nk, prev_nv, l, nreq):
    m = q3.shape[0]
    cache = pl.BlockSpec((None, nreq, SWA_KV, SWA_HD, WINDOW), lambda i: (l, i, 0, 0, 0))
    newcol = pl.BlockSpec((None, SWA_KV_WIDTH, nreq), lambda i: (i, 0, 0))
    in_specs = [pl.BlockSpec((nreq, SWA_HEADS, SWA_HD), lambda i: (i, 0, 0)), cache, cache,
                pl.BlockSpec((nreq, SWA_KV_WIDTH), lambda i: (i, C_K // SWA_KV_WIDTH)),
                pl.BlockSpec((nreq, SWA_KV_WIDTH), lambda i: (i, C_VV // SWA_KV_WIDTH)),
                newcol, newcol, _resident((SWA_HEADS, LANES)),
                pl.BlockSpec((None, SWA_HEADS, LANES), lambda i: (l, 0, 0))]
    args = [q3, kt, vt, zs, zs, knc, vnc, slope_b, sink_b]
    aliases = {}
    if prev_nk is not None:
        in_specs += [pl.BlockSpec(memory_space=pl.ANY)] * 2
        aliases = {len(args): 1, len(args) + 1: 2}
        args += [prev_nk, prev_nv]
    return pl.pallas_call(
        functools.partial(_swa_sample_kernel, l=l, nreq=nreq),
        grid=(m // nreq,),
        in_specs=in_specs,
        out_specs=[pl.BlockSpec((nreq, SWA_HEADS, SWA_HD), lambda i: (i, 0, 0)), cache, cache],
        out_shape=[jax.ShapeDtypeStruct((m, SWA_HEADS, SWA_HD), F32),
                   jax.ShapeDtypeStruct(kt.shape, F32), jax.ShapeDtypeStruct(vt.shape, F32)],
        input_output_aliases=aliases,
        compiler_params=_params("arbitrary"),
        name="swa_sample",
    )(*args)


def _memkv_kernel(x_ref, g_ref, w_ref, o_ref, xn_ref, *, l):
    @pl.when(pl.program_id(0) == 0)
    def _():
        xn_ref[...] = _rms(x_ref[...], g_ref[l:l + 1, :]).astype(BF)

    o_ref[...] = _dot(xn_ref[...], w_ref[...].astype(BF))


def _memkv(mem, g, w, l, tn):
    m = mem.shape[0]
    return pl.pallas_call(
        functools.partial(_memkv_kernel, l=l),
        grid=(2 * MEM_WIDTH // tn,),
        in_specs=[_resident((m, D_MODEL)), _resident(g.shape),
                  pl.BlockSpec((None, D_MODEL, tn), lambda j: (l, 0, j))],
        out_specs=pl.BlockSpec((m, tn), lambda j: (0, j)),
        out_shape=jax.ShapeDtypeStruct((m, 2 * MEM_WIDTH), F32),
        scratch_shapes=[pltpu.VMEM((m, D_MODEL), BF)],
        compiler_params=_params("arbitrary"),
        name="mem_kv",
    )(mem, g, w)


def _mem_sample_kernel(q_ref, mk_ref, mv_ref, o_ref, *, nreq):
    nrow = MEM_LEN * MEM_HEADS
    own = (lax.broadcasted_iota(jnp.int32, (nreq, MEM_HEADS, nrow), 2) % MEM_HEADS
           == lax.broadcasted_iota(jnp.int32, (nreq, MEM_HEADS, nrow), 1))
    s = jnp.einsum("rhd,rkd->rhk", q_ref[...].astype(BF), mk_ref[...].astype(BF),
                   preferred_element_type=F32) * (MEM_HD ** -0.5)
    s = jnp.where(own, s, -jnp.inf)
    e = jnp.exp(s - jnp.max(s, axis=-1, keepdims=True))
    p = (e / jnp.sum(e, axis=-1, keepdims=True)).astype(BF)
    o_ref[...] = jnp.einsum("rhk,rkd->rhd", p, mv_ref[...].astype(BF), preferred_element_type=F32)


def _mem_sample(q3, mk, mv, l, nreq):
    m = q3.shape[0]
    qspec = pl.BlockSpec((nreq, MEM_HEADS, MEM_HD), lambda i: (i, 0, 0))
    kvspec = pl.BlockSpec((None, nreq, MEM_LEN * MEM_HEADS, MEM_HD), lambda i: (l, i, 0, 0))
    return pl.pallas_call(
        functools.partial(_mem_sample_kernel, nreq=nreq),
        grid=(m // nreq,),
        in_specs=[qspec, kvspec, kvspec],
        out_specs=qspec,
        out_shape=jax.ShapeDtypeStruct((m, MEM_HEADS, MEM_HD), F32),
        compiler_params=_params("arbitrary"),
        name="mem_attn_sample",
    )(q3, mk, mv)


def _merge_kernel(xn_ref, og_ref, oa_ref, om_ref, xns_ref, ogs_ref, oas_ref, oms_ref,
                  wg0_ref, wg1_ref, wg2_ref, wbg_ref, wba_ref, wbm_ref, out_ref, outs_ref,
                  bg0_ref, bg1_ref, bg2_ref, bbg_ref, bba_ref, bbm_ref):
    pairs = [(wg0_ref, bg0_ref), (wg1_ref, bg1_ref), (wg2_ref, bg2_ref),
             (wbg_ref, bbg_ref), (wba_ref, bba_ref), (wbm_ref, bbm_ref)]

    @pl.when(_first_inner())
    def _():
        for w_ref, b_ref in pairs:
            b_ref[...] = w_ref[...].astype(BF)

    def merged(xn, og, oa, om):
        def branch(bg_ref, o, bb_ref):
            return jax.nn.sigmoid(_dot(xn, bg_ref[...])) * _dot(o, bb_ref[...])
        return (branch(bg0_ref, og, bbg_ref) + branch(bg1_ref, oa, bba_ref) + branch(bg2_ref, om, bbm_ref)).astype(BF)

    out_ref[...] = merged(xn_ref[...], og_ref[...], oa_ref[...], om_ref[...])

    @pl.when(_last_inner())
    def _():
        outs_ref[...] = merged(xns_ref[...], ogs_ref[...], oas_ref[...], oms_ref[...])


def _merge(xn, og, oa, om, xns, ogs, oas, oms, w_in, wbg, wba, wbm, l, tm, tn):
    m, ms = xn.shape[0], xns.shape[0]
    nt = D_MODEL // tn
    row = lambda n: pl.BlockSpec((tm, n), lambda j, i: (i, 0))
    gate = lambda b: pl.BlockSpec((None, D_MODEL, tn), lambda j, i: (l, 0, C_G // tn + b * nt + j))
    col = lambda n: pl.BlockSpec((None, n, tn), lambda j, i: (l, 0, j))
    widths = (G_WIDTH, SWA_WIDTH, MEM_WIDTH)
    return pl.pallas_call(
        _merge_kernel,
        grid=(nt, m // tm),
        in_specs=[row(D_MODEL)] + [row(n) for n in widths]
                 + [_resident((ms, D_MODEL))] + [_resident((ms, n)) for n in widths]
                 + [gate(0), gate(1), gate(2)] + [col(n) for n in widths],
        out_specs=[pl.BlockSpec((tm, tn), lambda j, i: (i, j)), pl.BlockSpec((ms, tn), lambda j, i: (0, j))],
        out_shape=[jax.ShapeDtypeStruct((m, D_MODEL), BF), jax.ShapeDtypeStruct((ms, D_MODEL), BF)],
        scratch_shapes=[pltpu.VMEM((D_MODEL, tn), BF)] * 3 + [pltpu.VMEM((n, tn), BF) for n in widths],
        compiler_params=_params("arbitrary", "arbitrary"),
        name="merge",
    )(xn, og, oa, om, xns, ogs, oas, oms, w_in, w_in, w_in, wbg, wba, wbm)


def _out_kernel(x_ref, mg_ref, xs_ref, mgs_ref, w_ref, g_ref, xo_ref, xn_ref, xos_ref, xns_ref, wbf_ref, *, l):
    @pl.when(pl.program_id(0) == 0)
    def _():
        wbf_ref[...] = w_ref[...].astype(BF)

    g = g_ref[l:l + 1, :]
    x = x_ref[...] + _dot(mg_ref[...], wbf_ref[...])
    xo_ref[...] = x
    xn_ref[...] = _rms(x, g).astype(BF)

    @pl.when(pl.program_id(0) == pl.num_programs(0) - 1)
    def _():
        xs = xs_ref[...] + _dot(mgs_ref[...], wbf_ref[...])
        xos_ref[...] = xs
        xns_ref[...] = _rms(xs, g).astype(BF)


def _out_proj(x, merged, xs, mergeds, w_out, g, l, tm):
    m, ms = x.shape[0], xs.shape[0]
    row = pl.BlockSpec((tm, D_MODEL), lambda i: (i, 0))
    srow = pl.BlockSpec((ms, D_MODEL), lambda i: (0, 0))
    return pl.pallas_call(
        functools.partial(_out_kernel, l=l),
        grid=(m // tm,),
        in_specs=[row, row, _resident((ms, D_MODEL)), _resident((ms, D_MODEL)),
                  pl.BlockSpec((None, D_MODEL, D_MODEL), lambda i: (l, 0, 0), pipeline_mode=pl.Buffered(1)),
                  _resident(g.shape)],
        out_specs=[row, row, srow, srow],
        out_shape=[jax.ShapeDtypeStruct((m, D_MODEL), F32), jax.ShapeDtypeStruct((m, D_MODEL), BF),
                   jax.ShapeDtypeStruct((ms, D_MODEL), F32), jax.ShapeDtypeStruct((ms, D_MODEL), BF)],
        scratch_shapes=[pltpu.VMEM((D_MODEL, D_MODEL), BF)],
        compiler_params=_params("arbitrary"),
        name="out_proj",
    )(x, merged, xs, mergeds, w_out, g)


def _up_kernel(xn_ref, xns_ref, wa_ref, wb_ref, cwa_ref, cwb_ref, cba_ref, cbb_ref, sta_ref, stb_ref,
               g_ref, ca_ref, cb_ref, gs_ref, has_ref, hbs_ref, ha_ref, hb_ref, ba_ref, bb_ref,
               *, l, tm, tiles_per_seq):
    @pl.when(_first_inner())
    def _():
        ba_ref[...] = wa_ref[...].astype(BF)
        bb_ref[...] = wb_ref[...].astype(BF)

    @pl.when(pl.program_id(1) % tiles_per_seq == 0)
    def _():
        ha_ref[...] = jnp.zeros(ha_ref.shape, F32)
        hb_ref[...] = jnp.zeros(hb_ref.shape, F32)

    top = lax.broadcasted_iota(jnp.int32, (SUBLANES, ha_ref.shape[1]), 0)
    xn = xn_ref[...]

    def conv(h, prev_ref, cw_ref, bias_ref, tail_ref):
        prev = prev_ref[...]
        acc = bias_ref[l:l + 1, :]
        for j in range(CONV_W - 1):
            back = CONV_W - 1 - j
            rolled = pltpu.roll(h, back, axis=0)
            head = jnp.where(top < back, pltpu.roll(prev, back, axis=0), rolled[0:SUBLANES])
            acc = acc + jnp.concatenate([head, rolled[SUBLANES:]], axis=0) * cw_ref[j:j + 1, :]
        tail = h[tm - SUBLANES:]
        tail_ref[...] = tail
        prev_ref[...] = tail
        return acc + h * cw_ref[CONV_W - 1:CONV_W, :]

    a = conv(_dot(xn, ba_ref[...]), ha_ref, cwa_ref, cba_ref, ca_ref)
    b = conv(_dot(xn, bb_ref[...]), hb_ref, cwb_ref, cbb_ref, cb_ref)
    g_ref[...] = (jax.nn.gelu(a) * b).astype(BF)

    @pl.when(_last_inner())
    def _():
        xns = xns_ref[...]

        def conv_s(bw_ref, cw_ref, bias_ref, st_ref, h_out_ref):
            h = _dot(xns, bw_ref[...])
            h_out_ref[...] = h
            return (bias_ref[l:l + 1, :] + st_ref[:, 0, :] * cw_ref[0:1, :] + st_ref[:, 1, :] * cw_ref[1:2, :]
                    + h * cw_ref[2:3, :])

        a_s = conv_s(ba_ref, cwa_ref, cba_ref, sta_ref, has_ref)
        b_s = conv_s(bb_ref, cwb_ref, cbb_ref, stb_ref, hbs_ref)
        gs_ref[...] = (jax.nn.gelu(a_s) * b_s).astype(BF)


def _up_conv(xn, xns, w_up, conv_w, conv_b, state, l, batch, seq, tm, tn):
    m, ms = xn.shape[0], xns.shape[0]
    nt = D_FF // tn
    nrow = m // tm
    tiles_per_seq = seq // tm
    half = lambda off: (lambda j, i: (l, 0, off * nt + j))
    wspec = lambda off: pl.BlockSpec((None, D_MODEL, tn), half(off))
    cwspec = lambda off: pl.BlockSpec((None, CONV_W, tn), half(off))
    cbspec = lambda off: pl.BlockSpec((DEPTH, tn), lambda j, i: (0, off * nt + j))
    stspec = lambda off: pl.BlockSpec((None, ms, CONV_W - 1, tn), lambda j, i: (l, 0, 0, off * nt + j))
    tail = pl.BlockSpec((SUBLANES, tn), lambda j, i: (i // tiles_per_seq, j))
    scol = pl.BlockSpec((ms, tn), lambda j, i: (0, j))
    return pl.pallas_call(
        functools.partial(_up_kernel, l=l, tm=tm, tiles_per_seq=tiles_per_seq),
        grid=(nt, nrow),
        in_specs=[pl.BlockSpec((tm, D_MODEL), lambda j, i: (i, 0)), _resident((ms, D_MODEL)),
                  wspec(0), wspec(1), cwspec(0), cwspec(1), cbspec(0), cbspec(1), stspec(0), stspec(1)],
        out_specs=[pl.BlockSpec((tm, tn), lambda j, i: (i, j)), tail, tail, scol, scol, scol],
        out_shape=[jax.ShapeDtypeStruct((m, D_FF), BF),
                   jax.ShapeDtypeStruct((batch * SUBLANES, D_FF), F32),
                   jax.ShapeDtypeStruct((batch * SUBLANES, D_FF), F32),
                   jax.ShapeDtypeStruct((ms, D_FF), BF),
                   jax.ShapeDtypeStruct((ms, D_FF), F32), jax.ShapeDtypeStruct((ms, D_FF), F32)],
        scratch_shapes=[pltpu.VMEM((SUBLANES, tn), F32)] * 2 + [pltpu.VMEM((D_MODEL, tn), BF)] * 2,
        compiler_params=_params("arbitrary", "arbitrary"),
        name="up_conv_glu",
    )(xn, xns, w_up, w_up, conv_w, conv_w, conv_b, conv_b, state, state)


def _down_kernel(x_ref, g_ref, xs_ref, gs_ref, w_ref, gain_ref, *rest, tk, ncast, emit_x):
    wbf_ref = rest[-1]
    outs = rest[:-1]
    s = pl.program_id(0)

    @pl.when(s < ncast)
    def _():
        wbf_ref[pl.ds(pl.multiple_of(s * tk, tk), tk), :] = w_ref[...].astype(BF)

    def tile(x, g, o_refs):
        x = x + _dot(g, wbf_ref[...])
        if emit_x:
            o_refs[0][...] = x
        o_refs[-1][...] = _rms(x, gain_ref[...]).astype(o_refs[-1].dtype)

    nout = len(outs) // 2

    @pl.when(s >= ncast)
    def _():
        tile(x_ref[...], g_ref[...], outs[:nout])

    @pl.when(s == pl.num_programs(0) - 1)
    def _():
        tile(xs_ref[...], gs_ref[...], outs[nout:])


def _down_proj(x, g, xs, gs, w_down, gain, l, tm, tk, norm_dtype, emit_x):
    m, ms = x.shape[0], xs.shape[0]
    ncast = D_FF // tk
    rows = lambda n: pl.BlockSpec((tm, n), lambda s: (jnp.maximum(s - ncast, 0), 0))
    fixed = lambda n: pl.BlockSpec((ms, n), lambda s: (0, 0))
    dts = ([F32] if emit_x else []) + [norm_dtype]
    return pl.pallas_call(
        functools.partial(_down_kernel, tk=tk, ncast=ncast, emit_x=emit_x),
        grid=(ncast + m // tm,),
        in_specs=[rows(D_MODEL), rows(D_FF), _resident((ms, D_MODEL)), _resident((ms, D_FF)),
                  pl.BlockSpec((None, tk, D_MODEL), lambda s: (l, jnp.minimum(s, ncast - 1), 0)),
                  _resident((1, D_MODEL))],
        out_specs=[rows(D_MODEL) for _ in dts] + [fixed(D_MODEL) for _ in dts],
        out_shape=[jax.ShapeDtypeStruct((m, D_MODEL), dt) for dt in dts]
                  + [jax.ShapeDtypeStruct((ms, D_MODEL), dt) for dt in dts],
        scratch_shapes=[pltpu.VMEM((D_FF, D_MODEL), BF)],
        compiler_params=_params("arbitrary"),
        name="down_proj",
    )(x, g, xs, gs, w_down, gain)


def _conv_state_kernel(st_ref, *rest, nt):
    o_ref = rest[-1]
    first_half = pl.program_id(0) < nt
    for l in range(DEPTH):
        ha_ref, hb_ref = rest[2 * l], rest[2 * l + 1]
        o_ref[l, :, 0, :] = st_ref[l, :, CONV_W - 2, :]
        o_ref[l, :, 1, :] = jnp.where(first_half, ha_ref[...], hb_ref[...])


def _conv_state(state, halves, tn):
    depth, ms = state.shape[:2]
    nt = D_FF // tn
    blk = pl.BlockSpec((depth, ms, CONV_W - 1, tn), lambda j: (0, 0, 0, j))
    a_spec = pl.BlockSpec((ms, tn), lambda j: (0, jnp.minimum(j, nt - 1)))
    b_spec = pl.BlockSpec((ms, tn), lambda j: (0, jnp.maximum(j - nt, 0)))
    return pl.pallas_call(
        functools.partial(_conv_state_kernel, nt=nt),
        grid=(2 * nt,),
        in_specs=[blk] + [a_spec, b_spec] * depth,
        out_specs=blk,
        out_shape=jax.ShapeDtypeStruct(state.shape, F32),
        compiler_params=_params("arbitrary"),
        name="conv_state_sample",
    )(state, *halves)


def _layer(l, last, next_gain, xp, xs, xn, xns, mem, batch, seq, kt, vt, mk4, mv4, state_conv,
           prev_nk, prev_nv, slope_b, sink_b,
           norm_mix_g, w_in, gmlp_norm_g, gmlp_ws, gmlp_bs, attn_sinks, mem_norm_g, w_mem_kv,
           w_br_g, w_br_a, w_br_m, w_out, norm_ffn_g, w_up, conv_w, conv_b, w_down):
    dec = xs.shape[0]
    nreq = 8
    bs = gmlp_bs[l]
    bsb = jnp.repeat(bs.T, G_GDIM, axis=1)
    mkv = _memkv(mem, mem_norm_g, w_mem_kv, l, 256)
    kv, oa, og, vg_last, om, zs = _front(attn_sinks, xn, xns, w_in, gmlp_ws, gmlp_norm_g, bsb, mkv, l, seq, 512)

    coef = jnp.stack([jnp.repeat(gmlp_ws[l, :, 0, 0], G_GDIM), jnp.repeat(bs[:, 0], G_GDIM)])
    ogs, vgs = _gmlp_sample(zs, gmlp_norm_g, coef, l)

    q3 = zs[:, C_Q:C_K].reshape(dec, SWA_HEADS, SWA_HD)
    newcol = lambda a, b: zs[:, a:b].reshape(dec // nreq, nreq, SWA_KV_WIDTH).transpose(0, 2, 1)
    oas, nk, nv = _swa_sample(q3, kt, vt, zs, newcol(C_K, C_VV), newcol(C_VV, C_M), slope_b, sink_b,
                              prev_nk, prev_nv, l, nreq)

    oms = _mem_sample(zs[:, C_M:C_G].reshape(dec, MEM_HEADS, MEM_HD), mk4, mv4, l, nreq)

    merged, mergeds = _merge(xn, og, oa, om, xns, ogs, oas.reshape(dec, SWA_WIDTH).astype(BF),
                             oms.reshape(dec, MEM_WIDTH).astype(BF), w_in, w_br_g, w_br_a, w_br_m, l, 1024, 256)
    xp, xn2, xs, xn2s = _out_proj(xp, merged, xs, mergeds, w_out, norm_ffn_g, l, 256)
    g, ca, cb, gs, has, hbs = _up_conv(xn2, xn2s, w_up, conv_w, conv_b, state_conv, l, batch, seq, 1024, 512)
    outs = _down_proj(xp, g, xs, gs, w_down, next_gain, l, 256, 256, F32 if last else BF, not last)

    tail = lambda t: t.reshape(batch, SUBLANES, D_FF)[:, SUBLANES - (CONV_W - 1):]
    kvb = kv.reshape(batch, seq, 2 * SWA_KV_WIDTH)[:, seq - WINDOW:]
    st = dict(
        pk=kvb[:, :, :SWA_KV_WIDTH].reshape(batch, WINDOW, SWA_KV, SWA_HD),
        pv=kvb[:, :, SWA_KV_WIDTH:].reshape(batch, WINDOW, SWA_KV, SWA_HD),
        mk=mkv[:, :MEM_WIDTH].reshape(batch, MEM_LEN, MEM_HEADS, MEM_HD),
        mv=mkv[:, MEM_WIDTH:].reshape(batch, MEM_LEN, MEM_HEADS, MEM_HD),
        gvp=vg_last.reshape(batch, CHUNK, G_WIDTH),
        gvs=vgs.reshape(dec, 1, G_WIDTH),
        cvp=jnp.concatenate([tail(ca), tail(cb)], axis=-1),
        cvs=(has, hbs),
    )
    return outs, nk, nv, st


def kernel(x_prompt, x_sample, cache_swa_k, cache_swa_v, cache_mem_k, cache_mem_v, state_conv, mem_prompt, norm_mix_g, w_in, gmlp_norm_g, gmlp_ws, gmlp_bs, attn_sinks, mem_norm_g, w_mem_kv, w_br_g, w_br_a, w_br_m, w_out, norm_ffn_g, w_up, conv_w, conv_b, w_down, final_norm_g):
    batch, seq, _ = x_prompt.shape
    dec = x_sample.shape[0]
    assert x_sample.shape[1] == 1 and PAST_LEN % CHUNK == 0 and PAST_LEN >= WINDOW
    assert seq % 2048 == 0 and seq >= WINDOW and (seq - 1) // CHUNK * CHUNK == seq - CHUNK
    xp = x_prompt.reshape(batch * seq, D_MODEL)
    xs = x_sample.reshape(dec, D_MODEL)
    mem = mem_prompt.reshape(batch * MEM_LEN, D_MODEL)
    kt = cache_swa_k.transpose(0, 1, 3, 4, 2)
    vt = cache_swa_v.transpose(0, 1, 3, 4, 2)
    mk4 = cache_mem_k.reshape(DEPTH, dec, MEM_LEN * MEM_HEADS, MEM_HD)
    mv4 = cache_mem_v.reshape(DEPTH, dec, MEM_LEN * MEM_HEADS, MEM_HD)
    slope_b = jnp.broadcast_to(jnp.asarray(SLOPES, F32)[:, None], (SWA_HEADS, LANES))
    sink_b = jnp.broadcast_to(attn_sinks[:, :, None], (DEPTH, SWA_HEADS, LANES))
    nk = nv = None
    sts = []
    xn, xns = _norm(xp, xs, norm_mix_g, 0, BF, 512)
    for l in range(DEPTH):
        last = l == DEPTH - 1
        next_gain = (final_norm_g if last else norm_mix_g[l + 1]).reshape(1, D_MODEL)
        outs, nk, nv, st = _layer(l, last, next_gain, xp, xs, xn, xns, mem, batch, seq, kt, vt, mk4, mv4, state_conv,
                                  nk, nv, slope_b, sink_b,
                                  norm_mix_g, w_in, gmlp_norm_g, gmlp_ws, gmlp_bs, attn_sinks, mem_norm_g,
                                  w_mem_kv, w_br_g, w_br_a, w_br_m, w_out, norm_ffn_g, w_up, conv_w, conv_b, w_down)
        sts.append(st)
        if last:
            y_prompt, y_sample = outs
        else:
            xp, xn, xs, xns = outs
    stack = lambda key: jnp.stack([s[key] for s in sts])
    return (y_prompt.reshape(batch, seq, D_MODEL), y_sample.reshape(dec, 1, D_MODEL), stack("pk"), stack("pv"),
            nk.transpose(0, 1, 4, 2, 3), nv.transpose(0, 1, 4, 2, 3),
            stack("mk"), stack("mv"), stack("gvp"), stack("gvs"), stack("cvp"),
            _conv_state(state_conv, [h for s in sts for h in s["cvs"]], 512))
```

```python
import functools
import math

import jax
import jax.numpy as jnp
from jax import lax
from jax.experimental import pallas as pl
from jax.experimental.pallas import tpu as pltpu

D_MODEL = 2048
DEPTH = 2
PAST_LEN = 8192
MEM_LEN = 256
CHUNK = 128
G_GROUPS = 8
G_WIDTH = 768
G_GDIM = G_WIDTH // G_GROUPS
SWA_HEADS = 12
SWA_KV = 4
SWA_GROUP = SWA_HEADS // SWA_KV
SWA_HD = 64
SWA_WIDTH = SWA_HEADS * SWA_HD
SWA_KV_WIDTH = SWA_KV * SWA_HD
WINDOW = 128
MEM_HEADS = 4
MEM_HD = 128
MEM_WIDTH = MEM_HEADS * MEM_HD
N_BRANCH = 3
D_FF = 5632
CONV_W = 3
EPS = 1e-6

LANES = 128
SUBLANES = 8
VMEM_LIMIT = 56 * 1024 * 1024

C_U, C_V, C_Q, C_K, C_VV, C_M, C_G = 0, 768, 1536, 2304, 2560, 2816, 3328
ZT = 256
N_ACT = C_Q // ZT
assert all(c % ZT == 0 for c in (C_V, C_Q, C_K, C_VV, C_M, C_G))

BF = jnp.bfloat16
F32 = jnp.float32
NT_DIMS = (((1,), (1,)), ((), ()))


def _alibi_slopes(n):
    p = 2 ** int(math.floor(math.log2(n)))
    base = [2.0 ** (-8.0 * (i + 1) / p) for i in range(p)]
    extra = [2.0 ** (-8.0 * (2 * i + 1) / (2 * p)) for i in range(n - p)]
    return base + extra


SLOPES = _alibi_slopes(SWA_HEADS)


def _params(*sem):
    return pltpu.CompilerParams(dimension_semantics=sem, vmem_limit_bytes=VMEM_LIMIT)


def _resident(shape):
    return pl.BlockSpec(shape, lambda *_: (0,) * len(shape), pipeline_mode=pl.Buffered(1))


def _rms(x, g):
    ms = jnp.mean(x * x, axis=-1, keepdims=True)
    return x * lax.rsqrt(ms + EPS) * g


def _dot(a, b):
    return jnp.dot(a, b, preferred_element_type=F32)


def _first_inner():
    return pl.program_id(1) == 0


def _last_inner():
    return pl.program_id(1) == pl.num_programs(1) - 1


NZ = C_G // ZT


QKV_TILES = range(C_Q // ZT, C_M // ZT)
QKV_W = C_M - C_Q


def _swa_bias_table(bias_ref):
    r = lax.broadcasted_iota(jnp.int32, (WINDOW, 2 * WINDOW), 0)
    c = lax.broadcasted_iota(jnp.int32, (WINDOW, 2 * WINDOW), 1)
    dist = r + WINDOW - c
    valid = (dist >= 0) & (dist <= WINDOW)
    distf = dist.astype(F32)
    for h in range(SWA_HEADS):
        j, g = divmod(h, SWA_GROUP)
        pen = -SLOPES[h] * distf
        bias_ref[j, g * WINDOW:(g + 1) * WINDOW, :] = jnp.where(valid, pen, -jnp.inf)
        bias_ref[SWA_KV + j, g * WINDOW:(g + 1) * WINDOW, :] = jnp.where(valid & (c >= WINDOW), pen, -jnp.inf)


def _front_kernel(sink_ref, xn_ref, xs_ref, ng_ref, w_ref, ws_ref, gg_ref, bsb_ref, mkv_ref,
                  kv_ref, oa_ref, og_ref, vg_ref, om_ref, zs_ref, *rest, l, tm, tiles_per_seq, raw):
    xno_ref, xnso_ref = rest[:2] if raw else (None, None)
    wbf_ref, qkv_ref, zt_ref, carry_ref, bias_ref = rest[-5:]
    s = pl.program_id(0)

    def normed(x_ref, out_ref):
        if not raw:
            return x_ref[...]
        x = _rms(x_ref[...], ng_ref[l:l + 1, :]).astype(BF)
        out_ref[...] = x
        return x

    rows = SWA_GROUP * WINDOW
    kcol, vcol = C_K - C_Q, C_VV - C_Q
    mcol = C_Q

    @pl.when(s < NZ)
    def _():
        wbf_ref[s] = w_ref[...].astype(BF)

    @pl.when(s == NZ)
    def _():
        _swa_bias_table(bias_ref)
        carry_ref[...] = jnp.zeros(carry_ref.shape, F32)

    def column(x, c):
        h = _dot(x, wbf_ref[c])
        return jax.nn.gelu(h) if c < N_ACT else h

    @pl.when(s >= NZ)
    def _():
        x = normed(xn_ref, xno_ref)
        for c in QKV_TILES:
            qkv_ref[:, c * ZT - C_Q:(c + 1) * ZT - C_Q] = column(x, c)
        kv_ref[...] = qkv_ref[:, kcol:]

        def project(c, col):
            def task():
                zt_ref[:, col:col + ZT] = column(x, c)
            return task

        tasks = [project(c, c * ZT) for c in range(N_ACT)]
        tasks += [project(c, mcol + (c - C_M // ZT) * ZT) for c in range(C_M // ZT, NZ)]

        r = lax.broadcasted_iota(jnp.int32, (CHUNK, CHUNK), 0)
        cc = lax.broadcasted_iota(jnp.int32, (CHUNK, CHUNK), 1)
        lane = lax.broadcasted_iota(jnp.int32, (CHUNK, LANES), 1)

        def gmlp_chunk(n):
            def task():
                wtri = [jnp.where(r >= cc, ws_ref[g], 0.0).astype(BF) for g in range(G_GROUPS)]
                crow = slice(n * CHUNK, (n + 1) * CHUNK)
                vg = _rms(zt_ref[crow, C_V:C_Q], gg_ref[l:l + 1, :])
                if n == tm // CHUNK - 1:
                    vg_ref[...] = vg
                for t in range(G_TILES):
                    cols = slice(t * LANES, (t + 1) * LANES)
                    wst = jnp.concatenate([wtri[GA[t]], wtri[GA[t] + 1]], axis=0)
                    both = _dot(wst, vg[:, cols].astype(BF))
                    mix = jnp.where(lane < BND[t], both[:CHUNK], both[CHUNK:]) + bsb_ref[:, cols]
                    og_ref[crow, cols] = (zt_ref[crow, cols] * mix).astype(BF)
            return task

        tasks += [gmlp_chunk(n) for n in range(tm // CHUNK)]

        def mem_head(h):
            def task():
                cols = slice(h * MEM_HD, (h + 1) * MEM_HD)
                q = zt_ref[:, mcol + h * MEM_HD:mcol + (h + 1) * MEM_HD].astype(BF)
                sc = lax.dot_general(q, mkv_ref[:, cols].astype(BF), NT_DIMS,
                                     preferred_element_type=F32) * (MEM_HD ** -0.5)
                e = jnp.exp(sc - jnp.max(sc, axis=-1, keepdims=True))
                p = (e / jnp.sum(e, axis=-1, keepdims=True)).astype(BF)
                vcols = slice(MEM_WIDTH + h * MEM_HD, MEM_WIDTH + (h + 1) * MEM_HD)
                om_ref[:, cols] = _dot(p, mkv_ref[:, vcols].astype(BF)).astype(BF)
            return task

        tasks += [mem_head(h) for h in range(MEM_HEADS)]

        units = [(b, j) for b in range(tm // WINDOW) for j in range(SWA_KV)]
        assert len(tasks) == len(units)
        head = lax.broadcasted_iota(jnp.int32, (rows, 1), 0) // WINDOW
        fresh = ((s - NZ) % tiles_per_seq == 0).astype(jnp.int32) * SWA_KV
        for task, (b, j) in zip(tasks, units):
            task()
            own = slice(b * WINDOW, (b + 1) * WINDOW)
            kvh = lambda base: slice(base + j * SWA_HD, base + (j + 1) * SWA_HD)
            if b == 0:
                kprev, vprev = carry_ref[:, kvh(0)], carry_ref[:, kvh(SWA_KV_WIDTH)]
            else:
                prev = slice((b - 1) * WINDOW, b * WINDOW)
                kprev, vprev = qkv_ref[prev, kvh(kcol)], qkv_ref[prev, kvh(vcol)]
            kj = jnp.concatenate([kprev, qkv_ref[own, kvh(kcol)]], axis=0).astype(BF)
            vj = jnp.concatenate([vprev, qkv_ref[own, kvh(vcol)]], axis=0).astype(BF)
            hs = range(j * SWA_GROUP, (j + 1) * SWA_GROUP)
            q = jnp.concatenate([qkv_ref[own, h * SWA_HD:(h + 1) * SWA_HD] for h in hs], axis=0).astype(BF)
            sink = jnp.full((rows, 1), sink_ref[l, hs[-1]], F32)
            for g in range(SWA_GROUP - 1):
                sink = jnp.where(head == g, sink_ref[l, hs[g]], sink)
            bias = bias_ref[(fresh if b == 0 else 0) + j]
            sc = lax.dot_general(q, kj, NT_DIMS, preferred_element_type=F32) * (SWA_HD ** -0.5) + bias
            mx = jnp.maximum(jnp.max(sc, axis=-1, keepdims=True), sink)
            p = jnp.exp(sc - mx)
            den = jnp.sum(p, axis=-1, keepdims=True) + jnp.exp(sink - mx)
            o = _dot(p.astype(BF), vj) / den
            for g, h in enumerate(hs):
                oa_ref[own, h * SWA_HD:(h + 1) * SWA_HD] = o[g * WINDOW:(g + 1) * WINDOW].astype(BF)
        last = slice(tm - WINDOW, tm)
        carry_ref[...] = qkv_ref[last, kcol:]

    @pl.when(s == pl.num_programs(0) - 1)
    def _():
        xs = normed(xs_ref, xnso_ref)
        for c in range(NZ):
            zs_ref[:, c * ZT:(c + 1) * ZT] = column(xs, c)


def _front(sinks, xn, xs, ng, w_in, ws, gg, bsb, mkv, l, seq, tm):
    m, ms = xn.shape[0], xs.shape[0]
    raw = xn.dtype == F32
    tiles_per_seq = seq // tm
    tile = lambda s: jnp.maximum(s - NZ, 0)
    rows = lambda n: pl.BlockSpec((tm, n), lambda s: (tile(s), 0))
    fixed = lambda n: pl.BlockSpec((ms, n), lambda s: (0, 0))
    per_seq = lambda r, n: pl.BlockSpec((r, n), lambda s: (tile(s) // tiles_per_seq, 0))
    outs = [(2 * SWA_KV_WIDTH, F32), (SWA_WIDTH, BF), (G_WIDTH, BF)]
    return pl.pallas_call(
        functools.partial(_front_kernel, l=l, tm=tm, tiles_per_seq=tiles_per_seq, raw=raw),
        grid=(NZ + m // tm,),
        in_specs=[pl.BlockSpec(memory_space=pltpu.SMEM), rows(D_MODEL), _resident((ms, D_MODEL)), _resident(ng.shape),
                  pl.BlockSpec((None, D_MODEL, ZT), lambda s: (l, 0, jnp.minimum(s, NZ - 1))),
                  pl.BlockSpec((None, G_GROUPS, CHUNK, CHUNK), lambda s: (l, 0, 0, 0)),
                  _resident(gg.shape), _resident((CHUNK, G_WIDTH)), per_seq(MEM_LEN, 2 * MEM_WIDTH)],
        out_specs=[rows(n) for n, _ in outs] + [per_seq(CHUNK, G_WIDTH), rows(MEM_WIDTH), fixed(C_G)]
                  + ([rows(D_MODEL), fixed(D_MODEL)] if raw else []),
        out_shape=[jax.ShapeDtypeStruct((m, n), dt) for n, dt in outs]
                  + [jax.ShapeDtypeStruct((m // seq * CHUNK, G_WIDTH), F32), jax.ShapeDtypeStruct((m, MEM_WIDTH), BF),
                     jax.ShapeDtypeStruct((ms, C_G), F32)]
                  + ([jax.ShapeDtypeStruct((m, D_MODEL), BF), jax.ShapeDtypeStruct((ms, D_MODEL), BF)] if raw else []),
        scratch_shapes=[pltpu.VMEM((NZ, D_MODEL, ZT), BF), pltpu.VMEM((tm, QKV_W), F32),
                        pltpu.VMEM((tm, C_Q + MEM_WIDTH), F32), pltpu.VMEM((WINDOW, 2 * SWA_KV_WIDTH), F32),
                        pltpu.VMEM((2 * SWA_KV, SWA_GROUP * WINDOW, 2 * WINDOW), F32)],
        compiler_params=_params("arbitrary"),
        name="front",
    )(sinks, xn, xs, ng, w_in, ws, gg, bsb, mkv)


G_TILES = G_WIDTH // LANES
GA = [(t * LANES) // G_GDIM for t in range(G_TILES)]
BND = [(GA[t] + 1) * G_GDIM - t * LANES for t in range(G_TILES)]
assert all(0 < b < LANES and (GA[t] + 2) * G_GDIM >= (t + 1) * LANES for t, b in enumerate(BND))


def _gmlp_sample_kernel(zs_ref, gg_ref, coef_ref, og_ref, vg_ref, *, l):
    vg = _rms(zs_ref[:, C_V:C_Q], gg_ref[l:l + 1, :])
    vg_ref[...] = vg
    og_ref[...] = (zs_ref[:, C_U:C_V] * (coef_ref[0:1, :] * vg + coef_ref[1:2, :])).astype(BF)


def _gmlp_sample(zs, gg, coef, l):
    m = zs.shape[0]
    return pl.pallas_call(
        functools.partial(_gmlp_sample_kernel, l=l),
        out_shape=[jax.ShapeDtypeStruct((m, G_WIDTH), BF), jax.ShapeDtypeStruct((m, G_WIDTH), F32)],
        name="gmlp_sample",
    )(zs, gg, coef)


def _swa_sample_kernel(q_ref, kt_ref, vt_ref, kn_ref, vn_ref, slope_ref, sink_ref, o_ref, *, nreq):
    kvw = SWA_KV_WIDTH
    hrow = lax.broadcasted_iota(jnp.int32, (SWA_HEADS, kvw), 0) // SWA_GROUP
    lblk = lax.broadcasted_iota(jnp.int32, (SWA_HEADS, kvw), 1) // SWA_HD
    own = hrow == lblk
    kvh = lax.broadcasted_iota(jnp.int32, (SWA_HEADS, SWA_HD), 0) // SWA_GROUP
    c = lax.broadcasted_iota(jnp.int32, (SWA_HEADS, WINDOW), 1)
    bias = slope_ref[...] * (WINDOW - c).astype(F32)
    sink = sink_ref[:, 0:1]
    scale = SWA_HD ** -0.5
    q = q_ref[...]
    qm = jnp.where(own, jnp.concatenate([q] * SWA_KV, axis=2), 0.0).astype(BF)
    kt = kt_ref[...].reshape(nreq, kvw, WINDOW)
    vt = vt_ref[...].reshape(nreq, kvw, WINDOW)
    rows = lambda ref: jnp.stack([ref[r:r + 1, :] for r in range(nreq)]).astype(BF).astype(F32)
    kn, vn = rows(kn_ref), rows(vn_ref)
    s_c = jnp.einsum("rhk,rkc->rhc", qm, kt.astype(BF), preferred_element_type=F32) * scale - bias
    s_n = jnp.sum(qm.astype(F32) * kn, axis=-1, keepdims=True) * scale
    mx = jnp.maximum(jnp.maximum(jnp.max(s_c, axis=-1, keepdims=True), s_n), sink)
    p_c = jnp.exp(s_c - mx)
    p_n = jnp.exp(s_n - mx)
    den = jnp.sum(p_c, axis=-1, keepdims=True) + p_n + jnp.exp(sink - mx)
    p_c = (p_c / den).astype(BF)
    p_n = (p_n / den).astype(BF).astype(F32)
    o_all = jnp.einsum("rhc,rkc->rhk", p_c, vt.astype(BF), preferred_element_type=F32) + p_n * vn
    o = jnp.zeros((nreq, SWA_HEADS, SWA_HD), F32)
    for j in range(SWA_KV):
        o = jnp.where(kvh == j, o_all[:, :, j * SWA_HD:(j + 1) * SWA_HD], o)
    o_ref[...] = o


def _swa_sample(q3, kt, vt, zs, slope_b, sink_b, l, nreq):
    m = q3.shape[0]
    cache = pl.BlockSpec((None, nreq, SWA_KV, SWA_HD, WINDOW), lambda i: (l, i, 0, 0, 0))
    return pl.pallas_call(
        functools.partial(_swa_sample_kernel, nreq=nreq),
        grid=(m // nreq,),
        in_specs=[pl.BlockSpec((nreq, SWA_HEADS, SWA_HD), lambda i: (i, 0, 0)), cache, cache,
                  pl.BlockSpec((nreq, SWA_KV_WIDTH), lambda i: (i, C_K // SWA_KV_WIDTH)),
                  pl.BlockSpec((nreq, SWA_KV_WIDTH), lambda i: (i, C_VV // SWA_KV_WIDTH)),
                  _resident((SWA_HEADS, LANES)), pl.BlockSpec((None, SWA_HEADS, LANES), lambda i: (l, 0, 0))],
        out_specs=pl.BlockSpec((nreq, SWA_HEADS, SWA_HD), lambda i: (i, 0, 0)),
        out_shape=jax.ShapeDtypeStruct((m, SWA_HEADS, SWA_HD), F32),
        compiler_params=_params("arbitrary"),
        name="swa_sample",
    )(q3, kt, vt, zs, zs, slope_b, sink_b)


def _window_update_kernel(kt_ref, vt_ref, knc_ref, vnc_ref, nk_ref, nv_ref, *, nreq):
    last = lax.broadcasted_iota(jnp.int32, (SWA_KV_WIDTH, WINDOW), 1) == WINDOW - 1
    for src_ref, col_ref, dst_ref in ((kt_ref, knc_ref, nk_ref), (vt_ref, vnc_ref, nv_ref)):
        for l in range(DEPTH):
            for r in range(nreq):
                old = src_ref[l, r].reshape(SWA_KV_WIDTH, WINDOW)
                new = jnp.where(last, col_ref[l, :, r:r + 1], pltpu.roll(old, WINDOW - 1, axis=1))
                dst_ref[l, r] = new.reshape(SWA_KV, SWA_HD, WINDOW)


def _window_update(kt, vt, knc, vnc, nreq):
    depth, m = kt.shape[:2]
    cache = pl.BlockSpec((depth, nreq, SWA_KV, SWA_HD, WINDOW), lambda i: (0, i, 0, 0, 0))
    newcol = pl.BlockSpec((depth, None, SWA_KV_WIDTH, nreq), lambda i: (0, i, 0, 0))
    return pl.pallas_call(
        functools.partial(_window_update_kernel, nreq=nreq),
        grid=(m // nreq,),
        in_specs=[cache, cache, newcol, newcol],
        out_specs=[cache, cache],
        out_shape=[jax.ShapeDtypeStruct(kt.shape, F32), jax.ShapeDtypeStruct(vt.shape, F32)],
        compiler_params=_params("arbitrary"),
        name="window_update_sample",
    )(kt, vt, knc, vnc)


def _memkv_kernel(x_ref, g_ref, w_ref, o_ref, xn_ref, *, l):
    @pl.when(pl.program_id(0) == 0)
    def _():
        xn_ref[...] = _rms(x_ref[...], g_ref[l:l + 1, :]).astype(BF)

    o_ref[...] = _dot(xn_ref[...], w_ref[...].astype(BF))


def _memkv(mem, g, w, l, tn):
    m = mem.shape[0]
    return pl.pallas_call(
        functools.partial(_memkv_kernel, l=l),
        grid=(2 * MEM_WIDTH // tn,),
        in_specs=[_resident((m, D_MODEL)), _resident(g.shape),
                  pl.BlockSpec((None, D_MODEL, tn), lambda j: (l, 0, j))],
        out_specs=pl.BlockSpec((m, tn), lambda j: (0, j)),
        out_shape=jax.ShapeDtypeStruct((m, 2 * MEM_WIDTH), F32),
        scratch_shapes=[pltpu.VMEM((m, D_MODEL), BF)],
        compiler_params=_params("arbitrary"),
        name="mem_kv",
    )(mem, g, w)


def _mem_sample_kernel(q_ref, mk_ref, mv_ref, o_ref, *, nreq):
    nrow = MEM_LEN * MEM_HEADS
    own = (lax.broadcasted_iota(jnp.int32, (nreq, MEM_HEADS, nrow), 2) % MEM_HEADS
           == lax.broadcasted_iota(jnp.int32, (nreq, MEM_HEADS, nrow), 1))
    s = jnp.einsum("rhd,rkd->rhk", q_ref[...].astype(BF), mk_ref[...].astype(BF),
                   preferred_element_type=F32) * (MEM_HD ** -0.5)
    s = jnp.where(own, s, -jnp.inf)
    e = jnp.exp(s - jnp.max(s, axis=-1, keepdims=True))
    p = (e / jnp.sum(e, axis=-1, keepdims=True)).astype(BF)
    o_ref[...] = jnp.einsum("rhk,rkd->rhd", p, mv_ref[...].astype(BF), preferred_element_type=F32)


def _mem_sample(q3, mk, mv, l, nreq):
    m = q3.shape[0]
    qspec = pl.BlockSpec((nreq, MEM_HEADS, MEM_HD), lambda i: (i, 0, 0))
    kvspec = pl.BlockSpec((None, nreq, MEM_LEN * MEM_HEADS, MEM_HD), lambda i: (l, i, 0, 0))
    return pl.pallas_call(
        functools.partial(_mem_sample_kernel, nreq=nreq),
        grid=(m // nreq,),
        in_specs=[qspec, kvspec, kvspec],
        out_specs=qspec,
        out_shape=jax.ShapeDtypeStruct((m, MEM_HEADS, MEM_HD), F32),
        compiler_params=_params("arbitrary"),
        name="mem_attn_sample",
    )(q3, mk, mv)


def _merge_kernel(xn_ref, og_ref, oa_ref, om_ref, xns_ref, ogs_ref, oas_ref, oms_ref,
                  wg0_ref, wg1_ref, wg2_ref, wbg_ref, wba_ref, wbm_ref, out_ref, outs_ref,
                  bg0_ref, bg1_ref, bg2_ref, bbg_ref, bba_ref, bbm_ref):
    pairs = [(wg0_ref, bg0_ref), (wg1_ref, bg1_ref), (wg2_ref, bg2_ref),
             (wbg_ref, bbg_ref), (wba_ref, bba_ref), (wbm_ref, bbm_ref)]

    @pl.when(_first_inner())
    def _():
        for w_ref, b_ref in pairs:
            b_ref[...] = w_ref[...].astype(BF)

    def merged(xn, og, oa, om):
        def branch(bg_ref, o, bb_ref):
            return jax.nn.sigmoid(_dot(xn, bg_ref[...])) * _dot(o, bb_ref[...])
        return (branch(bg0_ref, og, bbg_ref) + branch(bg1_ref, oa, bba_ref) + branch(bg2_ref, om, bbm_ref)).astype(BF)

    out_ref[...] = merged(xn_ref[...], og_ref[...], oa_ref[...], om_ref[...])

    @pl.when(_last_inner())
    def _():
        outs_ref[...] = merged(xns_ref[...], ogs_ref[...], oas_ref[...], oms_ref[...])


def _merge(xn, og, oa, om, xns, ogs, oas, oms, w_in, wbg, wba, wbm, l, tm, tn):
    m, ms = xn.shape[0], xns.shape[0]
    nt = D_MODEL // tn
    row = lambda n: pl.BlockSpec((tm, n), lambda j, i: (i, 0))
    gate = lambda b: pl.BlockSpec((None, D_MODEL, tn), lambda j, i: (l, 0, C_G // tn + b * nt + j))
    col = lambda n: pl.BlockSpec((None, n, tn), lambda j, i: (l, 0, j))
    widths = (G_WIDTH, SWA_WIDTH, MEM_WIDTH)
    return pl.pallas_call(
        _merge_kernel,
        grid=(nt, m // tm),
        in_specs=[row(D_MODEL)] + [row(n) for n in widths]
                 + [_resident((ms, D_MODEL))] + [_resident((ms, n)) for n in widths]
                 + [gate(0), gate(1), gate(2)] + [col(n) for n in widths],
        out_specs=[pl.BlockSpec((tm, tn), lambda j, i: (i, j)), pl.BlockSpec((ms, tn), lambda j, i: (0, j))],
        out_shape=[jax.ShapeDtypeStruct((m, D_MODEL), BF), jax.ShapeDtypeStruct((ms, D_MODEL), BF)],
        scratch_shapes=[pltpu.VMEM((D_MODEL, tn), BF)] * 3 + [pltpu.VMEM((n, tn), BF) for n in widths],
        compiler_params=_params("arbitrary", "arbitrary"),
        name="merge",
    )(xn, og, oa, om, xns, ogs, oas, oms, w_in, w_in, w_in, wbg, wba, wbm)


def _out_kernel(x_ref, mg_ref, xs_ref, mgs_ref, w_ref, g_ref, xo_ref, xn_ref, xos_ref, xns_ref, wbf_ref, *, l):
    @pl.when(pl.program_id(0) == 0)
    def _():
        wbf_ref[...] = w_ref[...].astype(BF)

    g = g_ref[l:l + 1, :]
    x = x_ref[...] + _dot(mg_ref[...], wbf_ref[...])
    xo_ref[...] = x
    xn_ref[...] = _rms(x, g).astype(BF)

    @pl.when(pl.program_id(0) == pl.num_programs(0) - 1)
    def _():
        xs = xs_ref[...] + _dot(mgs_ref[...], wbf_ref[...])
        xos_ref[...] = xs
        xns_ref[...] = _rms(xs, g).astype(BF)


def _out_proj(x, merged, xs, mergeds, w_out, g, l, tm):
    m, ms = x.shape[0], xs.shape[0]
    row = pl.BlockSpec((tm, D_MODEL), lambda i: (i, 0))
    srow = pl.BlockSpec((ms, D_MODEL), lambda i: (0, 0))
    return pl.pallas_call(
        functools.partial(_out_kernel, l=l),
        grid=(m // tm,),
        in_specs=[row, row, _resident((ms, D_MODEL)), _resident((ms, D_MODEL)),
                  pl.BlockSpec((None, D_MODEL, D_MODEL), lambda i: (l, 0, 0), pipeline_mode=pl.Buffered(1)),
                  _resident(g.shape)],
        out_specs=[row, row, srow, srow],
        out_shape=[jax.ShapeDtypeStruct((m, D_MODEL), F32), jax.ShapeDtypeStruct((m, D_MODEL), BF),
                   jax.ShapeDtypeStruct((ms, D_MODEL), F32), jax.ShapeDtypeStruct((ms, D_MODEL), BF)],
        scratch_shapes=[pltpu.VMEM((D_MODEL, D_MODEL), BF)],
        compiler_params=_params("arbitrary"),
        name="out_proj",
    )(x, merged, xs, mergeds, w_out, g)


def _up_kernel(xn_ref, xns_ref, wa_ref, wb_ref, cwa_ref, cwb_ref, cba_ref, cbb_ref, sta_ref, stb_ref,
               g_ref, ca_ref, cb_ref, gs_ref, has_ref, hbs_ref, ha_ref, hb_ref, ba_ref, bb_ref,
               *, l, tm, tiles_per_seq):
    @pl.when(_first_inner())
    def _():
        ba_ref[...] = wa_ref[...].astype(BF)
        bb_ref[...] = wb_ref[...].astype(BF)

    @pl.when(pl.program_id(1) % tiles_per_seq == 0)
    def _():
        ha_ref[...] = jnp.zeros(ha_ref.shape, F32)
        hb_ref[...] = jnp.zeros(hb_ref.shape, F32)

    top = lax.broadcasted_iota(jnp.int32, (SUBLANES, ha_ref.shape[1]), 0)
    xn = xn_ref[...]

    def conv(h, prev_ref, cw_ref, bias_ref, tail_ref):
        prev = prev_ref[...]
        acc = bias_ref[l:l + 1, :]
        for j in range(CONV_W - 1):
            back = CONV_W - 1 - j
            rolled = pltpu.roll(h, back, axis=0)
            head = jnp.where(top < back, pltpu.roll(prev, back, axis=0), rolled[0:SUBLANES])
            acc = acc + jnp.concatenate([head, rolled[SUBLANES:]], axis=0) * cw_ref[j:j + 1, :]
        tail = h[tm - SUBLANES:]
        tail_ref[...] = tail
        prev_ref[...] = tail
        return acc + h * cw_ref[CONV_W - 1:CONV_W, :]

    a = conv(_dot(xn, ba_ref[...]), ha_ref, cwa_ref, cba_ref, ca_ref)
    b = conv(_dot(xn, bb_ref[...]), hb_ref, cwb_ref, cbb_ref, cb_ref)
    g_ref[...] = (jax.nn.gelu(a) * b).astype(BF)

    @pl.when(_last_inner())
    def _():
        xns = xns_ref[...]

        def conv_s(bw_ref, cw_ref, bias_ref, st_ref, h_out_ref):
            h = _dot(xns, bw_ref[...])
            h_out_ref[...] = h
            return (bias_ref[l:l + 1, :] + st_ref[:, 0, :] * cw_ref[0:1, :] + st_ref[:, 1, :] * cw_ref[1:2, :]
                    + h * cw_ref[2:3, :])

        a_s = conv_s(ba_ref, cwa_ref, cba_ref, sta_ref, has_ref)
        b_s = conv_s(bb_ref, cwb_ref, cbb_ref, stb_ref, hbs_ref)
        gs_ref[...] = (jax.nn.gelu(a_s) * b_s).astype(BF)


def _up_conv(xn, xns, w_up, conv_w, conv_b, state, l, batch, seq, tm, tn):
    m, ms = xn.shape[0], xns.shape[0]
    nt = D_FF // tn
    nrow = m // tm
    tiles_per_seq = seq // tm
    half = lambda off: (lambda j, i: (l, 0, off * nt + j))
    wspec = lambda off: pl.BlockSpec((None, D_MODEL, tn), half(off))
    cwspec = lambda off: pl.BlockSpec((None, CONV_W, tn), half(off))
    cbspec = lambda off: pl.BlockSpec((DEPTH, tn), lambda j, i: (0, off * nt + j))
    stspec = lambda off: pl.BlockSpec((None, ms, CONV_W - 1, tn), lambda j, i: (l, 0, 0, off * nt + j))
    tail = pl.BlockSpec((SUBLANES, tn), lambda j, i: (i // tiles_per_seq, j))
    scol = pl.BlockSpec((ms, tn), lambda j, i: (0, j))
    return pl.pallas_call(
        functools.partial(_up_kernel, l=l, tm=tm, tiles_per_seq=tiles_per_seq),
        grid=(nt, nrow),
        in_specs=[pl.BlockSpec((tm, D_MODEL), lambda j, i: (i, 0)), _resident((ms, D_MODEL)),
                  wspec(0), wspec(1), cwspec(0), cwspec(1), cbspec(0), cbspec(1), stspec(0), stspec(1)],
        out_specs=[pl.BlockSpec((tm, tn), lambda j, i: (i, j)), tail, tail, scol, scol, scol],
        out_shape=[jax.ShapeDtypeStruct((m, D_FF), BF),
                   jax.ShapeDtypeStruct((batch * SUBLANES, D_FF), F32),
                   jax.ShapeDtypeStruct((batch * SUBLANES, D_FF), F32),
                   jax.ShapeDtypeStruct((ms, D_FF), BF),
                   jax.ShapeDtypeStruct((ms, D_FF), F32), jax.ShapeDtypeStruct((ms, D_FF), F32)],
        scratch_shapes=[pltpu.VMEM((SUBLANES, tn), F32)] * 2 + [pltpu.VMEM((D_MODEL, tn), BF)] * 2,
        compiler_params=_params("arbitrary", "arbitrary"),
        name="up_conv_glu",
    )(xn, xns, w_up, w_up, conv_w, conv_w, conv_b, conv_b, state, state)


def _down_kernel(x_ref, g_ref, xs_ref, gs_ref, w_ref, gain_ref, *rest, tk, ncast, emit_x):
    wbf_ref = rest[-1]
    outs = rest[:-1]
    s = pl.program_id(0)

    @pl.when(s < ncast)
    def _():
        wbf_ref[pl.ds(pl.multiple_of(s * tk, tk), tk), :] = w_ref[...].astype(BF)

    def tile(x, g, o_refs):
        x = x + _dot(g, wbf_ref[...])
        if emit_x:
            o_refs[0][...] = x
        o_refs[-1][...] = _rms(x, gain_ref[...]).astype(o_refs[-1].dtype)

    nout = len(outs) // 2

    @pl.when(s >= ncast)
    def _():
        tile(x_ref[...], g_ref[...], outs[:nout])

    @pl.when(s == pl.num_programs(0) - 1)
    def _():
        tile(xs_ref[...], gs_ref[...], outs[nout:])


def _down_proj(x, g, xs, gs, w_down, gain, l, tm, tk, norm_dtype, emit_x):
    m, ms = x.shape[0], xs.shape[0]
    ncast = D_FF // tk
    rows = lambda n: pl.BlockSpec((tm, n), lambda s: (jnp.maximum(s - ncast, 0), 0))
    fixed = lambda n: pl.BlockSpec((ms, n), lambda s: (0, 0))
    dts = ([F32] if emit_x else []) + [norm_dtype]
    return pl.pallas_call(
        functools.partial(_down_kernel, tk=tk, ncast=ncast, emit_x=emit_x),
        grid=(ncast + m // tm,),
        in_specs=[rows(D_MODEL), rows(D_FF), _resident((ms, D_MODEL)), _resident((ms, D_FF)),
                  pl.BlockSpec((None, tk, D_MODEL), lambda s: (l, jnp.minimum(s, ncast - 1), 0)),
                  _resident((1, D_MODEL))],
        out_specs=[rows(D_MODEL) for _ in dts] + [fixed(D_MODEL) for _ in dts],
        out_shape=[jax.ShapeDtypeStruct((m, D_MODEL), dt) for dt in dts]
                  + [jax.ShapeDtypeStruct((ms, D_MODEL), dt) for dt in dts],
        scratch_shapes=[pltpu.VMEM((D_FF, D_MODEL), BF)],
        compiler_params=_params("arbitrary"),
        name="down_proj",
    )(x, g, xs, gs, w_down, gain)


def _conv_state_kernel(st_ref, *rest, nt):
    o_ref = rest[-1]
    first_half = pl.program_id(0) < nt
    for l in range(DEPTH):
        ha_ref, hb_ref = rest[2 * l], rest[2 * l + 1]
        o_ref[l, :, 0, :] = st_ref[l, :, CONV_W - 2, :]
        o_ref[l, :, 1, :] = jnp.where(first_half, ha_ref[...], hb_ref[...])


def _conv_state(state, halves, tn):
    depth, ms = state.shape[:2]
    nt = D_FF // tn
    blk = pl.BlockSpec((depth, ms, CONV_W - 1, tn), lambda j: (0, 0, 0, j))
    a_spec = pl.BlockSpec((ms, tn), lambda j: (0, jnp.minimum(j, nt - 1)))
    b_spec = pl.BlockSpec((ms, tn), lambda j: (0, jnp.maximum(j - nt, 0)))
    return pl.pallas_call(
        functools.partial(_conv_state_kernel, nt=nt),
        grid=(2 * nt,),
        in_specs=[blk] + [a_spec, b_spec] * depth,
        out_specs=blk,
        out_shape=jax.ShapeDtypeStruct(state.shape, F32),
        compiler_params=_params("arbitrary"),
        name="conv_state_sample",
    )(state, *halves)


def _layer(l, last, next_gain, xp, xs, xn, xns, mem, batch, seq, kt, vt, mk4, mv4, state_conv, slope_b, sink_b,
           norm_mix_g, w_in, gmlp_norm_g, gmlp_ws, gmlp_bs, attn_sinks, mem_norm_g, w_mem_kv,
           w_br_g, w_br_a, w_br_m, w_out, norm_ffn_g, w_up, conv_w, conv_b, w_down):
    dec = xs.shape[0]
    nreq = 8
    bs = gmlp_bs[l]
    bsb = jnp.repeat(bs.T, G_GDIM, axis=1)
    mkv = _memkv(mem, mem_norm_g, w_mem_kv, l, 256)
    front = _front(attn_sinks, xp if xn is None else xn, xs if xn is None else xns, norm_mix_g, w_in,
                   gmlp_ws, gmlp_norm_g, bsb, mkv, l, seq, 512)
    kv, oa, og, vg_last, om, zs = front[:6]
    if xn is None:
        xn, xns = front[6:]

    coef = jnp.stack([jnp.repeat(gmlp_ws[l, :, 0, 0], G_GDIM), jnp.repeat(bs[:, 0], G_GDIM)])
    ogs, vgs = _gmlp_sample(zs, gmlp_norm_g, coef, l)

    q3 = zs[:, C_Q:C_K].reshape(dec, SWA_HEADS, SWA_HD)
    newcol = lambda a, b: zs[:, a:b].reshape(dec // nreq, nreq, SWA_KV_WIDTH).transpose(0, 2, 1)
    oas = _swa_sample(q3, kt, vt, zs, slope_b, sink_b, l, nreq)

    oms = _mem_sample(zs[:, C_M:C_G].reshape(dec, MEM_HEADS, MEM_HD), mk4, mv4, l, nreq)

    merged, mergeds = _merge(xn, og, oa, om, xns, ogs, oas.reshape(dec, SWA_WIDTH).astype(BF),
                             oms.reshape(dec, MEM_WIDTH).astype(BF), w_in, w_br_g, w_br_a, w_br_m, l, 1024, 256)
    xp, xn2, xs, xn2s = _out_proj(xp, merged, xs, mergeds, w_out, norm_ffn_g, l, 256)
    g, ca, cb, gs, has, hbs = _up_conv(xn2, xn2s, w_up, conv_w, conv_b, state_conv, l, batch, seq, 1024, 512)
    outs = _down_proj(xp, g, xs, gs, w_down, next_gain, l, 256, 256, F32 if last else BF, not last)

    tail = lambda t: t.reshape(batch, SUBLANES, D_FF)[:, SUBLANES - (CONV_W - 1):]
    kvb = kv.reshape(batch, seq, 2 * SWA_KV_WIDTH)[:, seq - WINDOW:]
    st = dict(
        pk=kvb[:, :, :SWA_KV_WIDTH].reshape(batch, WINDOW, SWA_KV, SWA_HD),
        pv=kvb[:, :, SWA_KV_WIDTH:].reshape(batch, WINDOW, SWA_KV, SWA_HD),
        mk=mkv[:, :MEM_WIDTH].reshape(batch, MEM_LEN, MEM_HEADS, MEM_HD),
        mv=mkv[:, MEM_WIDTH:].reshape(batch, MEM_LEN, MEM_HEADS, MEM_HD),
        gvp=vg_last.reshape(batch, CHUNK, G_WIDTH),
        gvs=vgs.reshape(dec, 1, G_WIDTH),
        cvp=jnp.concatenate([tail(ca), tail(cb)], axis=-1),
        cvs=(has, hbs),
        knc=newcol(C_K, C_VV),
        vnc=newcol(C_VV, C_M),
    )
    return outs, st


def kernel(x_prompt, x_sample, cache_swa_k, cache_swa_v, cache_mem_k, cache_mem_v, state_conv, mem_prompt, norm_mix_g, w_in, gmlp_norm_g, gmlp_ws, gmlp_bs, attn_sinks, mem_norm_g, w_mem_kv, w_br_g, w_br_a, w_br_m, w_out, norm_ffn_g, w_up, conv_w, conv_b, w_down, final_norm_g):
    batch, seq, _ = x_prompt.shape
    dec = x_sample.shape[0]
    assert x_sample.shape[1] == 1 and PAST_LEN % CHUNK == 0 and PAST_LEN >= WINDOW
    assert seq % 2048 == 0 and seq >= WINDOW and (seq - 1) // CHUNK * CHUNK == seq - CHUNK
    xp = x_prompt.reshape(batch * seq, D_MODEL)
    xs = x_sample.reshape(dec, D_MODEL)
    mem = mem_prompt.reshape(batch * MEM_LEN, D_MODEL)
    kt = cache_swa_k.transpose(0, 1, 3, 4, 2)
    vt = cache_swa_v.transpose(0, 1, 3, 4, 2)
    mk4 = cache_mem_k.reshape(DEPTH, dec, MEM_LEN * MEM_HEADS, MEM_HD)
    mv4 = cache_mem_v.reshape(DEPTH, dec, MEM_LEN * MEM_HEADS, MEM_HD)
    slope_b = jnp.broadcast_to(jnp.asarray(SLOPES, F32)[:, None], (SWA_HEADS, LANES))
    sink_b = jnp.broadcast_to(attn_sinks[:, :, None], (DEPTH, SWA_HEADS, LANES))
    sts = []
    xn = xns = None
    for l in range(DEPTH):
        last = l == DEPTH - 1
        next_gain = (final_norm_g if last else norm_mix_g[l + 1]).reshape(1, D_MODEL)
        outs, st = _layer(l, last, next_gain, xp, xs, xn, xns, mem, batch, seq, kt, vt, mk4, mv4, state_conv,
                          slope_b, sink_b,
                          norm_mix_g, w_in, gmlp_norm_g, gmlp_ws, gmlp_bs, attn_sinks, mem_norm_g,
                          w_mem_kv, w_br_g, w_br_a, w_br_m, w_out, norm_ffn_g, w_up, conv_w, conv_b, w_down)
        sts.append(st)
        if last:
            y_prompt, y_sample = outs
        else:
            xp, xn, xs, xns = outs
    stack = lambda key: jnp.stack([s[key] for s in sts])
    nk, nv = _window_update(kt, vt, stack("knc"), stack("vnc"), 8)
    return (y_prompt.reshape(batch, seq, D_MODEL), y_sample.reshape(dec, 1, D_MODEL), stack("pk"), stack("pv"),
            nk.transpose(0, 1, 4, 2, 3), nv.transpose(0, 1, 4, 2, 3),
            stack("mk"), stack("mv"), stack("gvp"), stack("gvs"), stack("cvp"),
            _conv_state(state_conv, [h for s in sts for h in s["cvs"]], 512))
```

```python
import functools
import math

import jax
import jax.numpy as jnp
from jax import lax
from jax.experimental import pallas as pl
from jax.experimental.pallas import tpu as pltpu

D_MODEL = 2048
DEPTH = 2
PAST_LEN = 8192
MEM_LEN = 256
CHUNK = 128
G_GROUPS = 8
G_WIDTH = 768
G_GDIM = G_WIDTH // G_GROUPS
SWA_HEADS = 12
SWA_KV = 4
SWA_GROUP = SWA_HEADS // SWA_KV
SWA_HD = 64
SWA_WIDTH = SWA_HEADS * SWA_HD
SWA_KV_WIDTH = SWA_KV * SWA_HD
WINDOW = 128
MEM_HEADS = 4
MEM_HD = 128
MEM_WIDTH = MEM_HEADS * MEM_HD
N_BRANCH = 3
D_FF = 5632
CONV_W = 3
EPS = 1e-6

LANES = 128
SUBLANES = 8
VMEM_LIMIT = 56 * 1024 * 1024

C_U, C_V, C_Q, C_K, C_VV, C_M, C_G = 0, 768, 1536, 2304, 2560, 2816, 3328
ZT = 256
N_ACT = C_Q // ZT
assert all(c % ZT == 0 for c in (C_V, C_Q, C_K, C_VV, C_M, C_G))

BF = jnp.bfloat16
F32 = jnp.float32
NT_DIMS = (((1,), (1,)), ((), ()))


def _alibi_slopes(n):
    p = 2 ** int(math.floor(math.log2(n)))
    base = [2.0 ** (-8.0 * (i + 1) / p) for i in range(p)]
    extra = [2.0 ** (-8.0 * (2 * i + 1) / (2 * p)) for i in range(n - p)]
    return base + extra


SLOPES = _alibi_slopes(SWA_HEADS)


def _params(*sem):
    return pltpu.CompilerParams(dimension_semantics=sem, vmem_limit_bytes=VMEM_LIMIT)


def _resident(shape):
    return pl.BlockSpec(shape, lambda *_: (0,) * len(shape), pipeline_mode=pl.Buffered(1))


def _rms(x, g):
    ms = jnp.mean(x * x, axis=-1, keepdims=True)
    return x * lax.rsqrt(ms + EPS) * g


def _dot(a, b):
    return jnp.dot(a, b, preferred_element_type=F32)


def _first_inner():
    return pl.program_id(1) == 0


def _last_inner():
    return pl.program_id(1) == pl.num_programs(1) - 1


NZ = C_G // ZT


QKV_TILES = range(C_Q // ZT, C_M // ZT)
QKV_W = C_M - C_Q


def _swa_bias_table(bias_ref):
    r = lax.broadcasted_iota(jnp.int32, (WINDOW, 2 * WINDOW), 0)
    c = lax.broadcasted_iota(jnp.int32, (WINDOW, 2 * WINDOW), 1)
    dist = r + WINDOW - c
    valid = (dist >= 0) & (dist <= WINDOW)
    distf = dist.astype(F32)
    for h in range(SWA_HEADS):
        j, g = divmod(h, SWA_GROUP)
        pen = -SLOPES[h] * distf
        bias_ref[j, g * WINDOW:(g + 1) * WINDOW, :] = jnp.where(valid, pen, -jnp.inf)
        bias_ref[SWA_KV + j, g * WINDOW:(g + 1) * WINDOW, :] = jnp.where(valid & (c >= WINDOW), pen, -jnp.inf)


def _front_kernel(sink_ref, xn_ref, xs_ref, ng_ref, w_ref, ws_ref, gg_ref, bsb_ref, mkv_ref,
                  kv_ref, oa_ref, og_ref, vg_ref, om_ref, zs_ref, *rest, l, tm, tiles_per_seq, raw):
    xno_ref, xnso_ref = rest[:2] if raw else (None, None)
    wbf_ref, qkv_ref, zt_ref, carry_ref, bias_ref = rest[-5:]
    s = pl.program_id(0)

    def normed(x_ref, out_ref):
        if not raw:
            return x_ref[...]
        x = _rms(x_ref[...], ng_ref[l:l + 1, :]).astype(BF)
        out_ref[...] = x
        return x

    rows = SWA_GROUP * WINDOW
    kcol, vcol = C_K - C_Q, C_VV - C_Q
    mcol = C_Q

    @pl.when(s < NZ)
    def _():
        wbf_ref[s] = w_ref[...].astype(BF)

    @pl.when(s == NZ)
    def _():
        _swa_bias_table(bias_ref)
        carry_ref[...] = jnp.zeros(carry_ref.shape, F32)

    def column(x, c):
        h = _dot(x, wbf_ref[c])
        return jax.nn.gelu(h) if c < N_ACT else h

    @pl.when(s >= NZ)
    def _():
        x = normed(xn_ref, xno_ref)
        for c in QKV_TILES:
            qkv_ref[:, c * ZT - C_Q:(c + 1) * ZT - C_Q] = column(x, c)
        kv_ref[...] = qkv_ref[:, kcol:]

        def project(c, col):
            def task():
                zt_ref[:, col:col + ZT] = column(x, c)
            return task

        tasks = [project(c, c * ZT) for c in range(N_ACT)]
        tasks += [project(c, mcol + (c - C_M // ZT) * ZT) for c in range(C_M // ZT, NZ)]

        r = lax.broadcasted_iota(jnp.int32, (CHUNK, CHUNK), 0)
        cc = lax.broadcasted_iota(jnp.int32, (CHUNK, CHUNK), 1)
        lane = lax.broadcasted_iota(jnp.int32, (CHUNK, LANES), 1)

        def gmlp_chunk(n):
            def task():
                wtri = [jnp.where(r >= cc, ws_ref[g], 0.0).astype(BF) for g in range(G_GROUPS)]
                crow = slice(n * CHUNK, (n + 1) * CHUNK)
                vg = _rms(zt_ref[crow, C_V:C_Q], gg_ref[l:l + 1, :])
                if n == tm // CHUNK - 1:
                    vg_ref[...] = vg
                for t in range(G_TILES):
                    cols = slice(t * LANES, (t + 1) * LANES)
                    wst = jnp.concatenate([wtri[GA[t]], wtri[GA[t] + 1]], axis=0)
                    both = _dot(wst, vg[:, cols].astype(BF))
                    mix = jnp.where(lane < BND[t], both[:CHUNK], both[CHUNK:]) + bsb_ref[:, cols]
                    og_ref[crow, cols] = (zt_ref[crow, cols] * mix).astype(BF)
            return task

        tasks += [gmlp_chunk(n) for n in range(tm // CHUNK)]

        def mem_head(h):
            def task():
                cols = slice(h * MEM_HD, (h + 1) * MEM_HD)
                q = zt_ref[:, mcol + h * MEM_HD:mcol + (h + 1) * MEM_HD].astype(BF)
                sc = lax.dot_general(q, mkv_ref[:, cols].astype(BF), NT_DIMS,
                                     preferred_element_type=F32) * (MEM_HD ** -0.5)
                e = jnp.exp(sc - jnp.max(sc, axis=-1, keepdims=True))
                p = (e / jnp.sum(e, axis=-1, keepdims=True)).astype(BF)
                vcols = slice(MEM_WIDTH + h * MEM_HD, MEM_WIDTH + (h + 1) * MEM_HD)
                om_ref[:, cols] = _dot(p, mkv_ref[:, vcols].astype(BF)).astype(BF)
            return task

        tasks += [mem_head(h) for h in range(MEM_HEADS)]

        units = [(b, j) for b in range(tm // WINDOW) for j in range(SWA_KV)]
        assert len(tasks) == len(units)
        head = lax.broadcasted_iota(jnp.int32, (rows, 1), 0) // WINDOW
        fresh = ((s - NZ) % tiles_per_seq == 0).astype(jnp.int32) * SWA_KV
        for task, (b, j) in zip(tasks, units):
            task()
            own = slice(b * WINDOW, (b + 1) * WINDOW)
            kvh = lambda base: slice(base + j * SWA_HD, base + (j + 1) * SWA_HD)
            if b == 0:
                kprev, vprev = carry_ref[:, kvh(0)], carry_ref[:, kvh(SWA_KV_WIDTH)]
            else:
                prev = slice((b - 1) * WINDOW, b * WINDOW)
                kprev, vprev = qkv_ref[prev, kvh(kcol)], qkv_ref[prev, kvh(vcol)]
            kj = jnp.concatenate([kprev, qkv_ref[own, kvh(kcol)]], axis=0).astype(BF)
            vj = jnp.concatenate([vprev, qkv_ref[own, kvh(vcol)]], axis=0).astype(BF)
            hs = range(j * SWA_GROUP, (j + 1) * SWA_GROUP)
            q = jnp.concatenate([qkv_ref[own, h * SWA_HD:(h + 1) * SWA_HD] for h in hs], axis=0).astype(BF)
            sink = jnp.full((rows, 1), sink_ref[l, hs[-1]], F32)
            for g in range(SWA_GROUP - 1):
                sink = jnp.where(head == g, sink_ref[l, hs[g]], sink)
            bias = bias_ref[(fresh if b == 0 else 0) + j]
            sc = lax.dot_general(q, kj, NT_DIMS, preferred_element_type=F32) * (SWA_HD ** -0.5) + bias
            mx = jnp.maximum(jnp.max(sc, axis=-1, keepdims=True), sink)
            p = jnp.exp(sc - mx)
            den = jnp.sum(p, axis=-1, keepdims=True) + jnp.exp(sink - mx)
            o = _dot(p.astype(BF), vj) / den
            for g, h in enumerate(hs):
                oa_ref[own, h * SWA_HD:(h + 1) * SWA_HD] = o[g * WINDOW:(g + 1) * WINDOW].astype(BF)
        last = slice(tm - WINDOW, tm)
        carry_ref[...] = qkv_ref[last, kcol:]

    @pl.when(s == pl.num_programs(0) - 1)
    def _():
        xs = normed(xs_ref, xnso_ref)
        for c in range(NZ):
            zs_ref[:, c * ZT:(c + 1) * ZT] = column(xs, c)


def _front(sinks, xn, xs, ng, w_in, ws, gg, bsb, mkv, l, seq, tm):
    m, ms = xn.shape[0], xs.shape[0]
    raw = xn.dtype == F32
    tiles_per_seq = seq // tm
    tile = lambda s: jnp.maximum(s - NZ, 0)
    rows = lambda n: pl.BlockSpec((tm, n), lambda s: (tile(s), 0))
    fixed = lambda n: pl.BlockSpec((ms, n), lambda s: (0, 0))
    per_seq = lambda r, n: pl.BlockSpec((r, n), lambda s: (tile(s) // tiles_per_seq, 0))
    outs = [(2 * SWA_KV_WIDTH, F32), (SWA_WIDTH, BF), (G_WIDTH, BF)]
    return pl.pallas_call(
        functools.partial(_front_kernel, l=l, tm=tm, tiles_per_seq=tiles_per_seq, raw=raw),
        grid=(NZ + m // tm,),
        in_specs=[pl.BlockSpec(memory_space=pltpu.SMEM), rows(D_MODEL), _resident((ms, D_MODEL)), _resident(ng.shape),
                  pl.BlockSpec((None, D_MODEL, ZT), lambda s: (l, 0, jnp.minimum(s, NZ - 1))),
                  pl.BlockSpec((None, G_GROUPS, CHUNK, CHUNK), lambda s: (l, 0, 0, 0)),
                  _resident(gg.shape), _resident((CHUNK, G_WIDTH)), per_seq(MEM_LEN, 2 * MEM_WIDTH)],
        out_specs=[rows(n) for n, _ in outs] + [per_seq(CHUNK, G_WIDTH), rows(MEM_WIDTH), fixed(C_G)]
                  + ([rows(D_MODEL), fixed(D_MODEL)] if raw else []),
        out_shape=[jax.ShapeDtypeStruct((m, n), dt) for n, dt in outs]
                  + [jax.ShapeDtypeStruct((m // seq * CHUNK, G_WIDTH), F32), jax.ShapeDtypeStruct((m, MEM_WIDTH), BF),
                     jax.ShapeDtypeStruct((ms, C_G), F32)]
                  + ([jax.ShapeDtypeStruct((m, D_MODEL), BF), jax.ShapeDtypeStruct((ms, D_MODEL), BF)] if raw else []),
        scratch_shapes=[pltpu.VMEM((NZ, D_MODEL, ZT), BF), pltpu.VMEM((tm, QKV_W), F32),
                        pltpu.VMEM((tm, C_Q + MEM_WIDTH), F32), pltpu.VMEM((WINDOW, 2 * SWA_KV_WIDTH), F32),
                        pltpu.VMEM((2 * SWA_KV, SWA_GROUP * WINDOW, 2 * WINDOW), F32)],
        compiler_params=_params("arbitrary"),
        name="front",
    )(sinks, xn, xs, ng, w_in, ws, gg, bsb, mkv)


G_TILES = G_WIDTH // LANES
GA = [(t * LANES) // G_GDIM for t in range(G_TILES)]
BND = [(GA[t] + 1) * G_GDIM - t * LANES for t in range(G_TILES)]
assert all(0 < b < LANES and (GA[t] + 2) * G_GDIM >= (t + 1) * LANES for t, b in enumerate(BND))


def _gmlp_sample_kernel(zs_ref, gg_ref, coef_ref, og_ref, vg_ref, *, l):
    vg = _rms(zs_ref[:, C_V:C_Q], gg_ref[l:l + 1, :])
    vg_ref[...] = vg
    og_ref[...] = (zs_ref[:, C_U:C_V] * (coef_ref[0:1, :] * vg + coef_ref[1:2, :])).astype(BF)


def _gmlp_sample(zs, gg, coef, l):
    m = zs.shape[0]
    return pl.pallas_call(
        functools.partial(_gmlp_sample_kernel, l=l),
        out_shape=[jax.ShapeDtypeStruct((m, G_WIDTH), BF), jax.ShapeDtypeStruct((m, G_WIDTH), F32)],
        name="gmlp_sample",
    )(zs, gg, coef)


def _swa_sample_kernel(q_ref, kt_ref, vt_ref, kn_ref, vn_ref, slope_ref, sink_ref, o_ref, *, nreq):
    kvw = SWA_KV_WIDTH
    hrow = lax.broadcasted_iota(jnp.int32, (SWA_HEADS, kvw), 0) // SWA_GROUP
    lblk = lax.broadcasted_iota(jnp.int32, (SWA_HEADS, kvw), 1) // SWA_HD
    own = hrow == lblk
    kvh = lax.broadcasted_iota(jnp.int32, (SWA_HEADS, SWA_HD), 0) // SWA_GROUP
    c = lax.broadcasted_iota(jnp.int32, (SWA_HEADS, WINDOW), 1)
    bias = slope_ref[...] * (WINDOW - c).astype(F32)
    sink = sink_ref[:, 0:1]
    scale = SWA_HD ** -0.5
    q = q_ref[...]
    qm = jnp.where(own, jnp.concatenate([q] * SWA_KV, axis=2), 0.0).astype(BF)
    kt = kt_ref[...].reshape(nreq, kvw, WINDOW)
    vt = vt_ref[...].reshape(nreq, kvw, WINDOW)
    rows = lambda ref: jnp.stack([ref[r:r + 1, :] for r in range(nreq)]).astype(BF).astype(F32)
    kn, vn = rows(kn_ref), rows(vn_ref)
    s_c = jnp.einsum("rhk,rkc->rhc", qm, kt.astype(BF), preferred_element_type=F32) * scale - bias
    s_n = jnp.sum(qm.astype(F32) * kn, axis=-1, keepdims=True) * scale
    mx = jnp.maximum(jnp.maximum(jnp.max(s_c, axis=-1, keepdims=True), s_n), sink)
    p_c = jnp.exp(s_c - mx)
    p_n = jnp.exp(s_n - mx)
    den = jnp.sum(p_c, axis=-1, keepdims=True) + p_n + jnp.exp(sink - mx)
    p_c = (p_c / den).astype(BF)
    p_n = (p_n / den).astype(BF).astype(F32)
    o_all = jnp.einsum("rhc,rkc->rhk", p_c, vt.astype(BF), preferred_element_type=F32) + p_n * vn
    o = jnp.zeros((nreq, SWA_HEADS, SWA_HD), F32)
    for j in range(SWA_KV):
        o = jnp.where(kvh == j, o_all[:, :, j * SWA_HD:(j + 1) * SWA_HD], o)
    o_ref[...] = o


def _swa_sample(q3, kt, vt, zs, slope_b, sink_b, l, nreq):
    m = q3.shape[0]
    cache = pl.BlockSpec((None, nreq, SWA_KV, SWA_HD, WINDOW), lambda i: (l, i, 0, 0, 0))
    return pl.pallas_call(
        functools.partial(_swa_sample_kernel, nreq=nreq),
        grid=(m // nreq,),
        in_specs=[pl.BlockSpec((nreq, SWA_HEADS, SWA_HD), lambda i: (i, 0, 0)), cache, cache,
                  pl.BlockSpec((nreq, SWA_KV_WIDTH), lambda i: (i, C_K // SWA_KV_WIDTH)),
                  pl.BlockSpec((nreq, SWA_KV_WIDTH), lambda i: (i, C_VV // SWA_KV_WIDTH)),
                  _resident((SWA_HEADS, LANES)), pl.BlockSpec((None, SWA_HEADS, LANES), lambda i: (l, 0, 0))],
        out_specs=pl.BlockSpec((nreq, SWA_HEADS, SWA_HD), lambda i: (i, 0, 0)),
        out_shape=jax.ShapeDtypeStruct((m, SWA_HEADS, SWA_HD), F32),
        compiler_params=_params("arbitrary"),
        name="swa_sample",
    )(q3, kt, vt, zs, zs, slope_b, sink_b)


def _window_update_kernel(kt_ref, vt_ref, knc_ref, vnc_ref, nk_ref, nv_ref, *, nreq):
    last = lax.broadcasted_iota(jnp.int32, (SWA_KV_WIDTH, WINDOW), 1) == WINDOW - 1
    for src_ref, col_ref, dst_ref in ((kt_ref, knc_ref, nk_ref), (vt_ref, vnc_ref, nv_ref)):
        for l in range(DEPTH):
            for r in range(nreq):
                old = src_ref[l, r].reshape(SWA_KV_WIDTH, WINDOW)
                new = jnp.where(last, col_ref[l, :, r:r + 1], pltpu.roll(old, WINDOW - 1, axis=1))
                dst_ref[l, r] = new.reshape(SWA_KV, SWA_HD, WINDOW)


def _window_update(kt, vt, knc, vnc, nreq):
    depth, m = kt.shape[:2]
    cache = pl.BlockSpec((depth, nreq, SWA_KV, SWA_HD, WINDOW), lambda i: (0, i, 0, 0, 0))
    newcol = pl.BlockSpec((depth, None, SWA_KV_WIDTH, nreq), lambda i: (0, i, 0, 0))
    return pl.pallas_call(
        functools.partial(_window_update_kernel, nreq=nreq),
        grid=(m // nreq,),
        in_specs=[cache, cache, newcol, newcol],
        out_specs=[cache, cache],
        out_shape=[jax.ShapeDtypeStruct(kt.shape, F32), jax.ShapeDtypeStruct(vt.shape, F32)],
        compiler_params=_params("arbitrary"),
        name="window_update_sample",
    )(kt, vt, knc, vnc)


def _memkv_kernel(x_ref, g_ref, w_ref, o_ref, xn_ref, *, l):
    @pl.when(pl.program_id(0) == 0)
    def _():
        xn_ref[...] = _rms(x_ref[...], g_ref[l:l + 1, :]).astype(BF)

    o_ref[...] = _dot(xn_ref[...], w_ref[...].astype(BF))


def _memkv(mem, g, w, l, tn):
    m = mem.shape[0]
    return pl.pallas_call(
        functools.partial(_memkv_kernel, l=l),
        grid=(2 * MEM_WIDTH // tn,),
        in_specs=[_resident((m, D_MODEL)), _resident(g.shape),
                  pl.BlockSpec((None, D_MODEL, tn), lambda j: (l, 0, j))],
        out_specs=pl.BlockSpec((m, tn), lambda j: (0, j)),
        out_shape=jax.ShapeDtypeStruct((m, 2 * MEM_WIDTH), F32),
        scratch_shapes=[pltpu.VMEM((m, D_MODEL), BF)],
        compiler_params=_params("arbitrary"),
        name="mem_kv",
    )(mem, g, w)


def _mem_sample_kernel(q_ref, mk_ref, mv_ref, o_ref, *, nreq):
    nrow = MEM_LEN * MEM_HEADS
    own = (lax.broadcasted_iota(jnp.int32, (nreq, MEM_HEADS, nrow), 2) % MEM_HEADS
           == lax.broadcasted_iota(jnp.int32, (nreq, MEM_HEADS, nrow), 1))
    s = jnp.einsum("rhd,rkd->rhk", q_ref[...].astype(BF), mk_ref[...].astype(BF),
                   preferred_element_type=F32) * (MEM_HD ** -0.5)
    s = jnp.where(own, s, -jnp.inf)
    e = jnp.exp(s - jnp.max(s, axis=-1, keepdims=True))
    p = (e / jnp.sum(e, axis=-1, keepdims=True)).astype(BF)
    o_ref[...] = jnp.einsum("rhk,rkd->rhd", p, mv_ref[...].astype(BF), preferred_element_type=F32)


def _mem_sample(q3, mk, mv, l, nreq):
    m = q3.shape[0]
    qspec = pl.BlockSpec((nreq, MEM_HEADS, MEM_HD), lambda i: (i, 0, 0))
    kvspec = pl.BlockSpec((None, nreq, MEM_LEN * MEM_HEADS, MEM_HD), lambda i: (l, i, 0, 0))
    return pl.pallas_call(
        functools.partial(_mem_sample_kernel, nreq=nreq),
        grid=(m // nreq,),
        in_specs=[qspec, kvspec, kvspec],
        out_specs=qspec,
        out_shape=jax.ShapeDtypeStruct((m, MEM_HEADS, MEM_HD), F32),
        compiler_params=_params("arbitrary"),
        name="mem_attn_sample",
    )(q3, mk, mv)


def _merge_kernel(xn_ref, og_ref, oa_ref, om_ref, xns_ref, ogs_ref, oas_ref, oms_ref,
                  wg0_ref, wg1_ref, wg2_ref, wbg_ref, wba_ref, wbm_ref, out_ref, outs_ref,
                  bg0_ref, bg1_ref, bg2_ref, bbg_ref, bba_ref, bbm_ref):
    pairs = [(wg0_ref, bg0_ref), (wg1_ref, bg1_ref), (wg2_ref, bg2_ref),
             (wbg_ref, bbg_ref), (wba_ref, bba_ref), (wbm_ref, bbm_ref)]

    @pl.when(_first_inner())
    def _():
        for w_ref, b_ref in pairs:
            b_ref[...] = w_ref[...].astype(BF)

    def merged(xn, og, oa, om):
        def branch(bg_ref, o, bb_ref):
            return jax.nn.sigmoid(_dot(xn, bg_ref[...])) * _dot(o, bb_ref[...])
        return (branch(bg0_ref, og, bbg_ref) + branch(bg1_ref, oa, bba_ref) + branch(bg2_ref, om, bbm_ref)).astype(BF)

    out_ref[...] = merged(xn_ref[...], og_ref[...], oa_ref[...], om_ref[...])

    @pl.when(_last_inner())
    def _():
        outs_ref[...] = merged(xns_ref[...], ogs_ref[...], oas_ref[...], oms_ref[...])


def _merge(xn, og, oa, om, xns, ogs, oas, oms, w_in, wbg, wba, wbm, l, tm, tn):
    m, ms = xn.shape[0], xns.shape[0]
    nt = D_MODEL // tn
    row = lambda n: pl.BlockSpec((tm, n), lambda j, i: (i, 0))
    gate = lambda b: pl.BlockSpec((None, D_MODEL, tn), lambda j, i: (l, 0, C_G // tn + b * nt + j))
    col = lambda n: pl.BlockSpec((None, n, tn), lambda j, i: (l, 0, j))
    widths = (G_WIDTH, SWA_WIDTH, MEM_WIDTH)
    return pl.pallas_call(
        _merge_kernel,
        grid=(nt, m // tm),
        in_specs=[row(D_MODEL)] + [row(n) for n in widths]
                 + [_resident((ms, D_MODEL))] + [_resident((ms, n)) for n in widths]
                 + [gate(0), gate(1), gate(2)] + [col(n) for n in widths],
        out_specs=[pl.BlockSpec((tm, tn), lambda j, i: (i, j)), pl.BlockSpec((ms, tn), lambda j, i: (0, j))],
        out_shape=[jax.ShapeDtypeStruct((m, D_MODEL), BF), jax.ShapeDtypeStruct((ms, D_MODEL), BF)],
        scratch_shapes=[pltpu.VMEM((D_MODEL, tn), BF)] * 3 + [pltpu.VMEM((n, tn), BF) for n in widths],
        compiler_params=_params("arbitrary", "arbitrary"),
        name="merge",
    )(xn, og, oa, om, xns, ogs, oas, oms, w_in, w_in, w_in, wbg, wba, wbm)


def _out_kernel(x_ref, mg_ref, xs_ref, mgs_ref, w_ref, g_ref, xo_ref, xn_ref, xos_ref, xns_ref, wbf_ref,
                *, l, tk, ncast):
    s = pl.program_id(0)

    @pl.when(s < ncast)
    def _():
        wbf_ref[pl.ds(pl.multiple_of(s * tk, tk), tk), :] = w_ref[...].astype(BF)

    def tile(x_ref, mg_ref, xo_ref, xn_ref):
        x = x_ref[...] + _dot(mg_ref[...], wbf_ref[...])
        xo_ref[...] = x
        xn_ref[...] = _rms(x, g_ref[l:l + 1, :]).astype(BF)

    @pl.when(s >= ncast)
    def _():
        tile(x_ref, mg_ref, xo_ref, xn_ref)

    @pl.when(s == pl.num_programs(0) - 1)
    def _():
        tile(xs_ref, mgs_ref, xos_ref, xns_ref)


def _out_proj(x, merged, xs, mergeds, w_out, g, l, tm, tk):
    m, ms = x.shape[0], xs.shape[0]
    ncast = D_MODEL // tk
    row = pl.BlockSpec((tm, D_MODEL), lambda s: (jnp.maximum(s - ncast, 0), 0))
    srow = pl.BlockSpec((ms, D_MODEL), lambda s: (0, 0))
    return pl.pallas_call(
        functools.partial(_out_kernel, l=l, tk=tk, ncast=ncast),
        grid=(ncast + m // tm,),
        in_specs=[row, row, _resident((ms, D_MODEL)), _resident((ms, D_MODEL)),
                  pl.BlockSpec((None, tk, D_MODEL), lambda s: (l, jnp.minimum(s, ncast - 1), 0)),
                  _resident(g.shape)],
        out_specs=[row, row, srow, srow],
        out_shape=[jax.ShapeDtypeStruct((m, D_MODEL), F32), jax.ShapeDtypeStruct((m, D_MODEL), BF),
                   jax.ShapeDtypeStruct((ms, D_MODEL), F32), jax.ShapeDtypeStruct((ms, D_MODEL), BF)],
        scratch_shapes=[pltpu.VMEM((D_MODEL, D_MODEL), BF)],
        compiler_params=_params("arbitrary"),
        name="out_proj",
    )(x, merged, xs, mergeds, w_out, g)


def _up_kernel(xn_ref, xns_ref, wa_ref, wb_ref, cwa_ref, cwb_ref, cba_ref, cbb_ref, sta_ref, stb_ref,
               g_ref, ca_ref, cb_ref, gs_ref, has_ref, hbs_ref, ha_ref, hb_ref, ba_ref, bb_ref,
               *, l, tm, tiles_per_seq):
    @pl.when(_first_inner())
    def _():
        ba_ref[...] = wa_ref[...].astype(BF)
        bb_ref[...] = wb_ref[...].astype(BF)

    @pl.when(pl.program_id(1) % tiles_per_seq == 0)
    def _():
        ha_ref[...] = jnp.zeros(ha_ref.shape, F32)
        hb_ref[...] = jnp.zeros(hb_ref.shape, F32)

    top = lax.broadcasted_iota(jnp.int32, (SUBLANES, ha_ref.shape[1]), 0)
    xn = xn_ref[...]

    def conv(h, prev_ref, cw_ref, bias_ref, tail_ref):
        prev = prev_ref[...]
        acc = bias_ref[l:l + 1, :]
        for j in range(CONV_W - 1):
            back = CONV_W - 1 - j
            rolled = pltpu.roll(h, back, axis=0)
            head = jnp.where(top < back, pltpu.roll(prev, back, axis=0), rolled[0:SUBLANES])
            acc = acc + jnp.concatenate([head, rolled[SUBLANES:]], axis=0) * cw_ref[j:j + 1, :]
        tail = h[tm - SUBLANES:]
        tail_ref[...] = tail
        prev_ref[...] = tail
        return acc + h * cw_ref[CONV_W - 1:CONV_W, :]

    a = conv(_dot(xn, ba_ref[...]), ha_ref, cwa_ref, cba_ref, ca_ref)
    b = conv(_dot(xn, bb_ref[...]), hb_ref, cwb_ref, cbb_ref, cb_ref)
    g_ref[...] = (jax.nn.gelu(a) * b).astype(BF)

    @pl.when(_last_inner())
    def _():
        xns = xns_ref[...]

        def conv_s(bw_ref, cw_ref, bias_ref, st_ref, h_out_ref):
            h = _dot(xns, bw_ref[...])
            h_out_ref[...] = h
            return (bias_ref[l:l + 1, :] + st_ref[:, 0, :] * cw_ref[0:1, :] + st_ref[:, 1, :] * cw_ref[1:2, :]
                    + h * cw_ref[2:3, :])

        a_s = conv_s(ba_ref, cwa_ref, cba_ref, sta_ref, has_ref)
        b_s = conv_s(bb_ref, cwb_ref, cbb_ref, stb_ref, hbs_ref)
        gs_ref[...] = (jax.nn.gelu(a_s) * b_s).astype(BF)


def _up_conv(xn, xns, w_up, conv_w, conv_b, state, l, batch, seq, tm, tn):
    m, ms = xn.shape[0], xns.shape[0]
    nt = D_FF // tn
    nrow = m // tm
    tiles_per_seq = seq // tm
    half = lambda off: (lambda j, i: (l, 0, off * nt + j))
    wspec = lambda off: pl.BlockSpec((None, D_MODEL, tn), half(off))
    cwspec = lambda off: pl.BlockSpec((None, CONV_W, tn), half(off))
    cbspec = lambda off: pl.BlockSpec((DEPTH, tn), lambda j, i: (0, off * nt + j))
    stspec = lambda off: pl.BlockSpec((None, ms, CONV_W - 1, tn), lambda j, i: (l, 0, 0, off * nt + j))
    tail = pl.BlockSpec((SUBLANES, tn), lambda j, i: (i // tiles_per_seq, j))
    scol = pl.BlockSpec((ms, tn), lambda j, i: (0, j))
    return pl.pallas_call(
        functools.partial(_up_kernel, l=l, tm=tm, tiles_per_seq=tiles_per_seq),
        grid=(nt, nrow),
        in_specs=[pl.BlockSpec((tm, D_MODEL), lambda j, i: (i, 0)), _resident((ms, D_MODEL)),
                  wspec(0), wspec(1), cwspec(0), cwspec(1), cbspec(0), cbspec(1), stspec(0), stspec(1)],
        out_specs=[pl.BlockSpec((tm, tn), lambda j, i: (i, j)), tail, tail, scol, scol, scol],
        out_shape=[jax.ShapeDtypeStruct((m, D_FF), BF),
                   jax.ShapeDtypeStruct((batch * SUBLANES, D_FF), F32),
                   jax.ShapeDtypeStruct((batch * SUBLANES, D_FF), F32),
                   jax.ShapeDtypeStruct((ms, D_FF), BF),
                   jax.ShapeDtypeStruct((ms, D_FF), F32), jax.ShapeDtypeStruct((ms, D_FF), F32)],
        scratch_shapes=[pltpu.VMEM((SUBLANES, tn), F32)] * 2 + [pltpu.VMEM((D_MODEL, tn), BF)] * 2,
        compiler_params=_params("arbitrary", "arbitrary"),
        name="up_conv_glu",
    )(xn, xns, w_up, w_up, conv_w, conv_w, conv_b, conv_b, state, state)


def _down_kernel(x_ref, g_ref, xs_ref, gs_ref, w_ref, gain_ref, *rest, tk, ncast, emit_x):
    wbf_ref = rest[-1]
    outs = rest[:-1]
    s = pl.program_id(0)

    @pl.when(s < ncast)
    def _():
        wbf_ref[pl.ds(pl.multiple_of(s * tk, tk), tk), :] = w_ref[...].astype(BF)

    def tile(x, g, o_refs):
        x = x + _dot(g, wbf_ref[...])
        if emit_x:
            o_refs[0][...] = x
        o_refs[-1][...] = _rms(x, gain_ref[...]).astype(o_refs[-1].dtype)

    nout = len(outs) // 2

    @pl.when(s >= ncast)
    def _():
        tile(x_ref[...], g_ref[...], outs[:nout])

    @pl.when(s == pl.num_programs(0) - 1)
    def _():
        tile(xs_ref[...], gs_ref[...], outs[nout:])


def _down_proj(x, g, xs, gs, w_down, gain, l, tm, tk, norm_dtype, emit_x):
    m, ms = x.shape[0], xs.shape[0]
    ncast = D_FF // tk
    rows = lambda n: pl.BlockSpec((tm, n), lambda s: (jnp.maximum(s - ncast, 0), 0))
    fixed = lambda n: pl.BlockSpec((ms, n), lambda s: (0, 0))
    dts = ([F32] if emit_x else []) + [norm_dtype]
    return pl.pallas_call(
        functools.partial(_down_kernel, tk=tk, ncast=ncast, emit_x=emit_x),
        grid=(ncast + m // tm,),
        in_specs=[rows(D_MODEL), rows(D_FF), _resident((ms, D_MODEL)), _resident((ms, D_FF)),
                  pl.BlockSpec((None, tk, D_MODEL), lambda s: (l, jnp.minimum(s, ncast - 1), 0)),
                  _resident((1, D_MODEL))],
        out_specs=[rows(D_MODEL) for _ in dts] + [fixed(D_MODEL) for _ in dts],
        out_shape=[jax.ShapeDtypeStruct((m, D_MODEL), dt) for dt in dts]
                  + [jax.ShapeDtypeStruct((ms, D_MODEL), dt) for dt in dts],
        scratch_shapes=[pltpu.VMEM((D_FF, D_MODEL), BF)],
        compiler_params=_params("arbitrary"),
        name="down_proj",
    )(x, g, xs, gs, w_down, gain)


def _conv_state_kernel(st_ref, *rest, nt):
    o_ref = rest[-1]
    first_half = pl.program_id(0) < nt
    for l in range(DEPTH):
        ha_ref, hb_ref = rest[2 * l], rest[2 * l + 1]
        o_ref[l, :, 0, :] = st_ref[l, :, CONV_W - 2, :]
        o_ref[l, :, 1, :] = jnp.where(first_half, ha_ref[...], hb_ref[...])


def _conv_state(state, halves, tn):
    depth, ms = state.shape[:2]
    nt = D_FF // tn
    blk = pl.BlockSpec((depth, ms, CONV_W - 1, tn), lambda j: (0, 0, 0, j))
    a_spec = pl.BlockSpec((ms, tn), lambda j: (0, jnp.minimum(j, nt - 1)))
    b_spec = pl.BlockSpec((ms, tn), lambda j: (0, jnp.maximum(j - nt, 0)))
    return pl.pallas_call(
        functools.partial(_conv_state_kernel, nt=nt),
        grid=(2 * nt,),
        in_specs=[blk] + [a_spec, b_spec] * depth,
        out_specs=blk,
        out_shape=jax.ShapeDtypeStruct(state.shape, F32),
        compiler_params=_params("arbitrary"),
        name="conv_state_sample",
    )(state, *halves)


def _layer(l, last, next_gain, xp, xs, xn, xns, mem, batch, seq, kt, vt, mk4, mv4, state_conv, slope_b, sink_b,
           norm_mix_g, w_in, gmlp_norm_g, gmlp_ws, gmlp_bs, attn_sinks, mem_norm_g, w_mem_kv,
           w_br_g, w_br_a, w_br_m, w_out, norm_ffn_g, w_up, conv_w, conv_b, w_down):
    dec = xs.shape[0]
    nreq = 8
    bs = gmlp_bs[l]
    bsb = jnp.repeat(bs.T, G_GDIM, axis=1)
    mkv = _memkv(mem, mem_norm_g, w_mem_kv, l, 256)
    front = _front(attn_sinks, xp if xn is None else xn, xs if xn is None else xns, norm_mix_g, w_in,
                   gmlp_ws, gmlp_norm_g, bsb, mkv, l, seq, 512)
    kv, oa, og, vg_last, om, zs = front[:6]
    if xn is None:
        xn, xns = front[6:]

    coef = jnp.stack([jnp.repeat(gmlp_ws[l, :, 0, 0], G_GDIM), jnp.repeat(bs[:, 0], G_GDIM)])
    ogs, vgs = _gmlp_sample(zs, gmlp_norm_g, coef, l)

    q3 = zs[:, C_Q:C_K].reshape(dec, SWA_HEADS, SWA_HD)
    newcol = lambda a, b: zs[:, a:b].reshape(dec // nreq, nreq, SWA_KV_WIDTH).transpose(0, 2, 1)
    oas = _swa_sample(q3, kt, vt, zs, slope_b, sink_b, l, nreq)

    oms = _mem_sample(zs[:, C_M:C_G].reshape(dec, MEM_HEADS, MEM_HD), mk4, mv4, l, nreq)

    merged, mergeds = _merge(xn, og, oa, om, xns, ogs, oas.reshape(dec, SWA_WIDTH).astype(BF),
                             oms.reshape(dec, MEM_WIDTH).astype(BF), w_in, w_br_g, w_br_a, w_br_m, l, 1024, 256)
    xp, xn2, xs, xn2s = _out_proj(xp, merged, xs, mergeds, w_out, norm_ffn_g, l, 512, 256)
    g, ca, cb, gs, has, hbs = _up_conv(xn2, xn2s, w_up, conv_w, conv_b, state_conv, l, batch, seq, 1024, 512)
    outs = _down_proj(xp, g, xs, gs, w_down, next_gain, l, 256, 256, F32 if last else BF, not last)

    tail = lambda t: t.reshape(batch, SUBLANES, D_FF)[:, SUBLANES - (CONV_W - 1):]
    kvb = kv.reshape(batch, seq, 2 * SWA_KV_WIDTH)[:, seq - WINDOW:]
    st = dict(
        pk=kvb[:, :, :SWA_KV_WIDTH].reshape(batch, WINDOW, SWA_KV, SWA_HD),
        pv=kvb[:, :, SWA_KV_WIDTH:].reshape(batch, WINDOW, SWA_KV, SWA_HD),
        mk=mkv[:, :MEM_WIDTH].reshape(batch, MEM_LEN, MEM_HEADS, MEM_HD),
        mv=mkv[:, MEM_WIDTH:].reshape(batch, MEM_LEN, MEM_HEADS, MEM_HD),
        gvp=vg_last.reshape(batch, CHUNK, G_WIDTH),
        gvs=vgs.reshape(dec, 1, G_WIDTH),
        cvp=jnp.concatenate([tail(ca), tail(cb)], axis=-1),
        cvs=(has, hbs),
        knc=newcol(C_K, C_VV),
        vnc=newcol(C_VV, C_M),
    )
    return outs, st


def kernel(x_prompt, x_sample, cache_swa_k, cache_swa_v, cache_mem_k, cache_mem_v, state_conv, mem_prompt, norm_mix_g, w_in, gmlp_norm_g, gmlp_ws, gmlp_bs, attn_sinks, mem_norm_g, w_mem_kv, w_br_g, w_br_a, w_br_m, w_out, norm_ffn_g, w_up, conv_w, conv_b, w_down, final_norm_g):
    batch, seq, _ = x_prompt.shape
    dec = x_sample.shape[0]
    assert x_sample.shape[1] == 1 and PAST_LEN % CHUNK == 0 and PAST_LEN >= WINDOW
    assert seq % 2048 == 0 and seq >= WINDOW and (seq - 1) // CHUNK * CHUNK == seq - CHUNK
    xp = x_prompt.reshape(batch * seq, D_MODEL)
    xs = x_sample.reshape(dec, D_MODEL)
    mem = mem_prompt.reshape(batch * MEM_LEN, D_MODEL)
    kt = cache_swa_k.transpose(0, 1, 3, 4, 2)
    vt = cache_swa_v.transpose(0, 1, 3, 4, 2)
    mk4 = cache_mem_k.reshape(DEPTH, dec, MEM_LEN * MEM_HEADS, MEM_HD)
    mv4 = cache_mem_v.reshape(DEPTH, dec, MEM_LEN * MEM_HEADS, MEM_HD)
    slope_b = jnp.broadcast_to(jnp.asarray(SLOPES, F32)[:, None], (SWA_HEADS, LANES))
    sink_b = jnp.broadcast_to(attn_sinks[:, :, None], (DEPTH, SWA_HEADS, LANES))
    sts = []
    xn = xns = None
    for l in range(DEPTH):
        last = l == DEPTH - 1
        next_gain = (final_norm_g if last else norm_mix_g[l + 1]).reshape(1, D_MODEL)
        outs, st = _layer(l, last, next_gain, xp, xs, xn, xns, mem, batch, seq, kt, vt, mk4, mv4, state_conv,
                          slope_b, sink_b,
                          norm_mix_g, w_in, gmlp_norm_g, gmlp_ws, gmlp_bs, attn_sinks, mem_norm_g,
                          w_mem_kv, w_br_g, w_br_a, w_br_m, w_out, norm_ffn_g, w_up, conv_w, conv_b, w_down)
        sts.append(st)
        if last:
            y_prompt, y_sample = outs
        else:
            xp, xn, xs, xns = outs
    stack = lambda key: jnp.stack([s[key] for s in sts])
    nk, nv = _window_update(kt, vt, stack("knc"), stack("vnc"), 8)
    return (y_prompt.reshape(batch, seq, D_MODEL), y_sample.reshape(dec, 1, D_MODEL), stack("pk"), stack("pv"),
            nk.transpose(0, 1, 4, 2, 3), nv.transpose(0, 1, 4, 2, 3),
            stack("mk"), stack("mv"), stack("gvp"), stack("gvs"), stack("cvp"),
            _conv_state(state_conv, [h for s in sts for h in s["cvs"]], 512))
```

```python
import functools
import math

import jax
import jax.numpy as jnp
from jax import lax
from jax.experimental import pallas as pl
from jax.experimental.pallas import tpu as pltpu

D_MODEL = 2048
DEPTH = 2
PAST_LEN = 8192
MEM_LEN = 256
CHUNK = 128
G_GROUPS = 8
G_WIDTH = 768
G_GDIM = G_WIDTH // G_GROUPS
SWA_HEADS = 12
SWA_KV = 4
SWA_GROUP = SWA_HEADS // SWA_KV
SWA_HD = 64
SWA_WIDTH = SWA_HEADS * SWA_HD
SWA_KV_WIDTH = SWA_KV * SWA_HD
WINDOW = 128
MEM_HEADS = 4
MEM_HD = 128
MEM_WIDTH = MEM_HEADS * MEM_HD
N_BRANCH = 3
D_FF = 5632
CONV_W = 3
EPS = 1e-6

LANES = 128
SUBLANES = 8
VMEM_LIMIT = 56 * 1024 * 1024

C_U, C_V, C_Q, C_K, C_VV, C_M, C_G = 0, 768, 1536, 2304, 2560, 2816, 3328
ZT = 256
N_ACT = C_Q // ZT
assert all(c % ZT == 0 for c in (C_V, C_Q, C_K, C_VV, C_M, C_G))

BF = jnp.bfloat16
F32 = jnp.float32
NT_DIMS = (((1,), (1,)), ((), ()))


def _alibi_slopes(n):
    p = 2 ** int(math.floor(math.log2(n)))
    base = [2.0 ** (-8.0 * (i + 1) / p) for i in range(p)]
    extra = [2.0 ** (-8.0 * (2 * i + 1) / (2 * p)) for i in range(n - p)]
    return base + extra


SLOPES = _alibi_slopes(SWA_HEADS)


def _params(*sem):
    return pltpu.CompilerParams(dimension_semantics=sem, vmem_limit_bytes=VMEM_LIMIT)


def _resident(shape):
    return pl.BlockSpec(shape, lambda *_: (0,) * len(shape), pipeline_mode=pl.Buffered(1))


def _rms(x, g):
    ms = jnp.mean(x * x, axis=-1, keepdims=True)
    return x * lax.rsqrt(ms + EPS) * g


def _dot(a, b):
    return jnp.dot(a, b, preferred_element_type=F32)


def _first_inner():
    return pl.program_id(1) == 0


def _last_inner():
    return pl.program_id(1) == pl.num_programs(1) - 1


NZ = C_G // ZT


QKV_TILES = range(C_Q // ZT, C_M // ZT)
QKV_W = C_M - C_Q


def _swa_bias_table(bias_ref):
    r = lax.broadcasted_iota(jnp.int32, (WINDOW, 2 * WINDOW), 0)
    c = lax.broadcasted_iota(jnp.int32, (WINDOW, 2 * WINDOW), 1)
    dist = r + WINDOW - c
    valid = (dist >= 0) & (dist <= WINDOW)
    distf = dist.astype(F32)
    for h in range(SWA_HEADS):
        j, g = divmod(h, SWA_GROUP)
        pen = -SLOPES[h] * distf
        bias_ref[j, g * WINDOW:(g + 1) * WINDOW, :] = jnp.where(valid, pen, -jnp.inf)
        bias_ref[SWA_KV + j, g * WINDOW:(g + 1) * WINDOW, :] = jnp.where(valid & (c >= WINDOW), pen, -jnp.inf)


def _front_kernel(sink_ref, xn_ref, xs_ref, ng_ref, w_ref, ws_ref, gg_ref, bsb_ref, mkv_ref,
                  kv_ref, oa_ref, og_ref, vg_ref, om_ref, zs_ref, *rest, l, tm, tiles_per_seq, raw, group=4):
    xno_ref, xnso_ref = rest[:2] if raw else (None, None)
    wbf_ref, qkv_ref, zt_ref, carry_ref, bias_ref = rest[-5:]
    s = pl.program_id(0)

    def normed(x_ref, out_ref):
        if not raw:
            return x_ref[...]
        x = _rms(x_ref[...], ng_ref[l:l + 1, :]).astype(BF)
        out_ref[...] = x
        return x

    rows = SWA_GROUP * WINDOW
    kcol, vcol = C_K - C_Q, C_VV - C_Q
    mcol = C_Q

    @pl.when(s < NZ)
    def _():
        wbf_ref[s] = w_ref[...].astype(BF)

    @pl.when(s == NZ)
    def _():
        _swa_bias_table(bias_ref)
        carry_ref[...] = jnp.zeros(carry_ref.shape, F32)

    def column(x, c):
        h = _dot(x, wbf_ref[c])
        return jax.nn.gelu(h) if c < N_ACT else h

    @pl.when(s >= NZ)
    def _():
        x = normed(xn_ref, xno_ref)
        for c in QKV_TILES:
            qkv_ref[:, c * ZT - C_Q:(c + 1) * ZT - C_Q] = column(x, c)
        kv_ref[...] = qkv_ref[:, kcol:]

        def project(c, col):
            def task():
                zt_ref[:, col:col + ZT] = column(x, c)
            return task

        tasks = [project(c, c * ZT) for c in range(N_ACT)]
        tasks += [project(c, mcol + (c - C_M // ZT) * ZT) for c in range(C_M // ZT, NZ)]

        r = lax.broadcasted_iota(jnp.int32, (CHUNK, CHUNK), 0)
        cc = lax.broadcasted_iota(jnp.int32, (CHUNK, CHUNK), 1)
        lane = lax.broadcasted_iota(jnp.int32, (CHUNK, LANES), 1)

        def gmlp_chunk(n):
            def task():
                wtri = [jnp.where(r >= cc, ws_ref[g], 0.0).astype(BF) for g in range(G_GROUPS)]
                crow = slice(n * CHUNK, (n + 1) * CHUNK)
                vg = _rms(zt_ref[crow, C_V:C_Q], gg_ref[l:l + 1, :])
                if n == tm // CHUNK - 1:
                    vg_ref[...] = vg
                for t in range(G_TILES):
                    cols = slice(t * LANES, (t + 1) * LANES)
                    wst = jnp.concatenate([wtri[GA[t]], wtri[GA[t] + 1]], axis=0)
                    both = _dot(wst, vg[:, cols].astype(BF))
                    mix = jnp.where(lane < BND[t], both[:CHUNK], both[CHUNK:]) + bsb_ref[:, cols]
                    og_ref[crow, cols] = (zt_ref[crow, cols] * mix).astype(BF)
            return task

        tasks += [gmlp_chunk(n) for n in range(tm // CHUNK)]

        def mem_head(h):
            def task():
                cols = slice(h * MEM_HD, (h + 1) * MEM_HD)
                q = zt_ref[:, mcol + h * MEM_HD:mcol + (h + 1) * MEM_HD].astype(BF)
                sc = lax.dot_general(q, mkv_ref[:, cols].astype(BF), NT_DIMS,
                                     preferred_element_type=F32) * (MEM_HD ** -0.5)
                e = jnp.exp(sc - jnp.max(sc, axis=-1, keepdims=True))
                p = (e / jnp.sum(e, axis=-1, keepdims=True)).astype(BF)
                vcols = slice(MEM_WIDTH + h * MEM_HD, MEM_WIDTH + (h + 1) * MEM_HD)
                om_ref[:, cols] = _dot(p, mkv_ref[:, vcols].astype(BF)).astype(BF)
            return task

        tasks += [mem_head(h) for h in range(MEM_HEADS)]

        units = [(b, j) for b in range(tm // WINDOW) for j in range(SWA_KV)]
        groups = [units[k:k + group] for k in range(0, len(units), group)]
        per = len(tasks) // len(groups)
        assert per * len(groups) == len(tasks) and per % 2 == 0
        head = lax.broadcasted_iota(jnp.int32, (rows, 1), 0) // WINDOW
        fresh = ((s - NZ) % tiles_per_seq == 0).astype(jnp.int32) * SWA_KV

        def scores(b, j):
            own = slice(b * WINDOW, (b + 1) * WINDOW)
            kvh = lambda base: slice(base + j * SWA_HD, base + (j + 1) * SWA_HD)
            if b == 0:
                kprev, vprev = carry_ref[:, kvh(0)], carry_ref[:, kvh(SWA_KV_WIDTH)]
            else:
                prev = slice((b - 1) * WINDOW, b * WINDOW)
                kprev, vprev = qkv_ref[prev, kvh(kcol)], qkv_ref[prev, kvh(vcol)]
            kj = jnp.concatenate([kprev, qkv_ref[own, kvh(kcol)]], axis=0).astype(BF)
            vj = jnp.concatenate([vprev, qkv_ref[own, kvh(vcol)]], axis=0).astype(BF)
            hs = range(j * SWA_GROUP, (j + 1) * SWA_GROUP)
            q = jnp.concatenate([qkv_ref[own, h * SWA_HD:(h + 1) * SWA_HD] for h in hs], axis=0).astype(BF)
            sink = jnp.full((rows, 1), sink_ref[l, hs[-1]], F32)
            for g in range(SWA_GROUP - 1):
                sink = jnp.where(head == g, sink_ref[l, hs[g]], sink)
            bias = bias_ref[(fresh if b == 0 else 0) + j]
            sc = lax.dot_general(q, kj, NT_DIMS, preferred_element_type=F32) * (SWA_HD ** -0.5) + bias
            return sc, sink, vj

        def softmax(sc, sink, vj):
            mx = jnp.maximum(jnp.max(sc, axis=-1, keepdims=True), sink)
            p = jnp.exp(sc - mx)
            den = jnp.sum(p, axis=-1, keepdims=True) + jnp.exp(sink - mx)
            return p.astype(BF), den, vj

        def values(b, j, p, den, vj):
            o = _dot(p, vj) / den
            own = slice(b * WINDOW, (b + 1) * WINDOW)
            for g, h in enumerate(range(j * SWA_GROUP, (j + 1) * SWA_GROUP)):
                oa_ref[own, h * SWA_HD:(h + 1) * SWA_HD] = o[g * WINDOW:(g + 1) * WINDOW].astype(BF)

        for k, grp in enumerate(groups):
            for task in tasks[per * k:per * k + per // 2]:
                task()
            stage = [scores(b, j) for b, j in grp]
            stage = [softmax(*args) for args in stage]
            for task in tasks[per * k + per // 2:per * (k + 1)]:
                task()
            for (b, j), args in zip(grp, stage):
                values(b, j, *args)
        last = slice(tm - WINDOW, tm)
        carry_ref[...] = qkv_ref[last, kcol:]

    @pl.when(s == pl.num_programs(0) - 1)
    def _():
        xs = normed(xs_ref, xnso_ref)
        for c in range(NZ):
            zs_ref[:, c * ZT:(c + 1) * ZT] = column(xs, c)


def _front(sinks, xn, xs, ng, w_in, ws, gg, bsb, mkv, l, seq, tm):
    m, ms = xn.shape[0], xs.shape[0]
    raw = xn.dtype == F32
    tiles_per_seq = seq // tm
    tile = lambda s: jnp.maximum(s - NZ, 0)
    rows = lambda n: pl.BlockSpec((tm, n), lambda s: (tile(s), 0))
    fixed = lambda n: pl.BlockSpec((ms, n), lambda s: (0, 0))
    per_seq = lambda r, n: pl.BlockSpec((r, n), lambda s: (tile(s) // tiles_per_seq, 0))
    outs = [(2 * SWA_KV_WIDTH, F32), (SWA_WIDTH, BF), (G_WIDTH, BF)]
    return pl.pallas_call(
        functools.partial(_front_kernel, l=l, tm=tm, tiles_per_seq=tiles_per_seq, raw=raw),
        grid=(NZ + m // tm,),
        in_specs=[pl.BlockSpec(memory_space=pltpu.SMEM), rows(D_MODEL), _resident((ms, D_MODEL)), _resident(ng.shape),
                  pl.BlockSpec((None, D_MODEL, ZT), lambda s: (l, 0, jnp.minimum(s, NZ - 1))),
                  pl.BlockSpec((None, G_GROUPS, CHUNK, CHUNK), lambda s: (l, 0, 0, 0)),
                  _resident(gg.shape), _resident((CHUNK, G_WIDTH)), per_seq(MEM_LEN, 2 * MEM_WIDTH)],
        out_specs=[rows(n) for n, _ in outs] + [per_seq(CHUNK, G_WIDTH), rows(MEM_WIDTH), fixed(C_G)]
                  + ([rows(D_MODEL), fixed(D_MODEL)] if raw else []),
        out_shape=[jax.ShapeDtypeStruct((m, n), dt) for n, dt in outs]
                  + [jax.ShapeDtypeStruct((m // seq * CHUNK, G_WIDTH), F32), jax.ShapeDtypeStruct((m, MEM_WIDTH), BF),
                     jax.ShapeDtypeStruct((ms, C_G), F32)]
                  + ([jax.ShapeDtypeStruct((m, D_MODEL), BF), jax.ShapeDtypeStruct((ms, D_MODEL), BF)] if raw else []),
        scratch_shapes=[pltpu.VMEM((NZ, D_MODEL, ZT), BF), pltpu.VMEM((tm, QKV_W), F32),
                        pltpu.VMEM((tm, C_Q + MEM_WIDTH), F32), pltpu.VMEM((WINDOW, 2 * SWA_KV_WIDTH), F32),
                        pltpu.VMEM((2 * SWA_KV, SWA_GROUP * WINDOW, 2 * WINDOW), F32)],
        compiler_params=_params("arbitrary"),
        name="front",
    )(sinks, xn, xs, ng, w_in, ws, gg, bsb, mkv)


G_TILES = G_WIDTH // LANES
GA = [(t * LANES) // G_GDIM for t in range(G_TILES)]
BND = [(GA[t] + 1) * G_GDIM - t * LANES for t in range(G_TILES)]
assert all(0 < b < LANES and (GA[t] + 2) * G_GDIM >= (t + 1) * LANES for t, b in enumerate(BND))


def _gmlp_sample_kernel(zs_ref, gg_ref, coef_ref, og_ref, vg_ref, *, l):
    vg = _rms(zs_ref[:, C_V:C_Q], gg_ref[l:l + 1, :])
    vg_ref[...] = vg
    og_ref[...] = (zs_ref[:, C_U:C_V] * (coef_ref[0:1, :] * vg + coef_ref[1:2, :])).astype(BF)


def _gmlp_sample(zs, gg, coef, l):
    m = zs.shape[0]
    return pl.pallas_call(
        functools.partial(_gmlp_sample_kernel, l=l),
        out_shape=[jax.ShapeDtypeStruct((m, G_WIDTH), BF), jax.ShapeDtypeStruct((m, G_WIDTH), F32)],
        name="gmlp_sample",
    )(zs, gg, coef)


def _swa_sample_kernel(q_ref, kt_ref, vt_ref, kn_ref, vn_ref, slope_ref, sink_ref, o_ref, *, nreq):
    kvw = SWA_KV_WIDTH
    hrow = lax.broadcasted_iota(jnp.int32, (SWA_HEADS, kvw), 0) // SWA_GROUP
    lblk = lax.broadcasted_iota(jnp.int32, (SWA_HEADS, kvw), 1) // SWA_HD
    own = hrow == lblk
    kvh = lax.broadcasted_iota(jnp.int32, (SWA_HEADS, SWA_HD), 0) // SWA_GROUP
    c = lax.broadcasted_iota(jnp.int32, (SWA_HEADS, WINDOW), 1)
    bias = slope_ref[...] * (WINDOW - c).astype(F32)
    sink = sink_ref[:, 0:1]
    scale = SWA_HD ** -0.5
    q = q_ref[...]
    qm = jnp.where(own, jnp.concatenate([q] * SWA_KV, axis=2), 0.0).astype(BF)
    kt = kt_ref[...].reshape(nreq, kvw, WINDOW)
    vt = vt_ref[...].reshape(nreq, kvw, WINDOW)
    rows = lambda ref: jnp.stack([ref[r:r + 1, :] for r in range(nreq)]).astype(BF).astype(F32)
    kn, vn = rows(kn_ref), rows(vn_ref)
    s_c = jnp.einsum("rhk,rkc->rhc", qm, kt.astype(BF), preferred_element_type=F32) * scale - bias
    s_n = jnp.sum(qm.astype(F32) * kn, axis=-1, keepdims=True) * scale
    mx = jnp.maximum(jnp.maximum(jnp.max(s_c, axis=-1, keepdims=True), s_n), sink)
    p_c = jnp.exp(s_c - mx)
    p_n = jnp.exp(s_n - mx)
    den = jnp.sum(p_c, axis=-1, keepdims=True) + p_n + jnp.exp(sink - mx)
    p_c = (p_c / den).astype(BF)
    p_n = (p_n / den).astype(BF).astype(F32)
    o_all = jnp.einsum("rhc,rkc->rhk", p_c, vt.astype(BF), preferred_element_type=F32) + p_n * vn
    o = jnp.zeros((nreq, SWA_HEADS, SWA_HD), F32)
    for j in range(SWA_KV):
        o = jnp.where(kvh == j, o_all[:, :, j * SWA_HD:(j + 1) * SWA_HD], o)
    o_ref[...] = o


def _swa_sample(q3, kt, vt, zs, slope_b, sink_b, l, nreq):
    m = q3.shape[0]
    cache = pl.BlockSpec((None, nreq, SWA_KV, SWA_HD, WINDOW), lambda i: (l, i, 0, 0, 0))
    return pl.pallas_call(
        functools.partial(_swa_sample_kernel, nreq=nreq),
        grid=(m // nreq,),
        in_specs=[pl.BlockSpec((nreq, SWA_HEADS, SWA_HD), lambda i: (i, 0, 0)), cache, cache,
                  pl.BlockSpec((nreq, SWA_KV_WIDTH), lambda i: (i, C_K // SWA_KV_WIDTH)),
                  pl.BlockSpec((nreq, SWA_KV_WIDTH), lambda i: (i, C_VV // SWA_KV_WIDTH)),
                  _resident((SWA_HEADS, LANES)), pl.BlockSpec((None, SWA_HEADS, LANES), lambda i: (l, 0, 0))],
        out_specs=pl.BlockSpec((nreq, SWA_HEADS, SWA_HD), lambda i: (i, 0, 0)),
        out_shape=jax.ShapeDtypeStruct((m, SWA_HEADS, SWA_HD), F32),
        compiler_params=_params("arbitrary"),
        name="swa_sample",
    )(q3, kt, vt, zs, zs, slope_b, sink_b)


def _window_update_kernel(kt_ref, vt_ref, knc_ref, vnc_ref, nk_ref, nv_ref, *, nreq):
    last = lax.broadcasted_iota(jnp.int32, (SWA_KV_WIDTH, WINDOW), 1) == WINDOW - 1
    for src_ref, col_ref, dst_ref in ((kt_ref, knc_ref, nk_ref), (vt_ref, vnc_ref, nv_ref)):
        for l in range(DEPTH):
            for r in range(nreq):
                old = src_ref[l, r].reshape(SWA_KV_WIDTH, WINDOW)
                new = jnp.where(last, col_ref[l, :, r:r + 1], pltpu.roll(old, WINDOW - 1, axis=1))
                dst_ref[l, r] = new.reshape(SWA_KV, SWA_HD, WINDOW)


def _window_update(kt, vt, knc, vnc, nreq):
    depth, m = kt.shape[:2]
    cache = pl.BlockSpec((depth, nreq, SWA_KV, SWA_HD, WINDOW), lambda i: (0, i, 0, 0, 0))
    newcol = pl.BlockSpec((depth, None, SWA_KV_WIDTH, nreq), lambda i: (0, i, 0, 0))
    return pl.pallas_call(
        functools.partial(_window_update_kernel, nreq=nreq),
        grid=(m // nreq,),
        in_specs=[cache, cache, newcol, newcol],
        out_specs=[cache, cache],
        out_shape=[jax.ShapeDtypeStruct(kt.shape, F32), jax.ShapeDtypeStruct(vt.shape, F32)],
        compiler_params=_params("arbitrary"),
        name="window_update_sample",
    )(kt, vt, knc, vnc)


def _memkv_kernel(x_ref, g_ref, w_ref, o_ref, xn_ref, *, l):
    @pl.when(pl.program_id(0) == 0)
    def _():
        xn_ref[...] = _rms(x_ref[...], g_ref[l:l + 1, :]).astype(BF)

    o_ref[...] = _dot(xn_ref[...], w_ref[...].astype(BF))


def _memkv(mem, g, w, l, tn):
    m = mem.shape[0]
    return pl.pallas_call(
        functools.partial(_memkv_kernel, l=l),
        grid=(2 * MEM_WIDTH // tn,),
        in_specs=[_resident((m, D_MODEL)), _resident(g.shape),
                  pl.BlockSpec((None, D_MODEL, tn), lambda j: (l, 0, j))],
        out_specs=pl.BlockSpec((m, tn), lambda j: (0, j)),
        out_shape=jax.ShapeDtypeStruct((m, 2 * MEM_WIDTH), F32),
        scratch_shapes=[pltpu.VMEM((m, D_MODEL), BF)],
        compiler_params=_params("arbitrary"),
        name="mem_kv",
    )(mem, g, w)


def _mem_sample_kernel(q_ref, mk_ref, mv_ref, o_ref, *, nreq):
    nrow = MEM_LEN * MEM_HEADS
    own = (lax.broadcasted_iota(jnp.int32, (nreq, MEM_HEADS, nrow), 2) % MEM_HEADS
           == lax.broadcasted_iota(jnp.int32, (nreq, MEM_HEADS, nrow), 1))
    s = jnp.einsum("rhd,rkd->rhk", q_ref[...].astype(BF), mk_ref[...].astype(BF),
                   preferred_element_type=F32) * (MEM_HD ** -0.5)
    s = jnp.where(own, s, -jnp.inf)
    e = jnp.exp(s - jnp.max(s, axis=-1, keepdims=True))
    p = (e / jnp.sum(e, axis=-1, keepdims=True)).astype(BF)
    o_ref[...] = jnp.einsum("rhk,rkd->rhd", p, mv_ref[...].astype(BF), preferred_element_type=F32)


def _mem_sample(q3, mk, mv, l, nreq):
    m = q3.shape[0]
    qspec = pl.BlockSpec((nreq, MEM_HEADS, MEM_HD), lambda i: (i, 0, 0))
    kvspec = pl.BlockSpec((None, nreq, MEM_LEN * MEM_HEADS, MEM_HD), lambda i: (l, i, 0, 0))
    return pl.pallas_call(
        functools.partial(_mem_sample_kernel, nreq=nreq),
        grid=(m // nreq,),
        in_specs=[qspec, kvspec, kvspec],
        out_specs=qspec,
        out_shape=jax.ShapeDtypeStruct((m, MEM_HEADS, MEM_HD), F32),
        compiler_params=_params("arbitrary"),
        name="mem_attn_sample",
    )(q3, mk, mv)


def _merge_kernel(xn_ref, og_ref, oa_ref, om_ref, xns_ref, ogs_ref, oas_ref, oms_ref,
                  wg0_ref, wg1_ref, wg2_ref, wbg_ref, wba_ref, wbm_ref, out_ref, outs_ref,
                  bg0_ref, bg1_ref, bg2_ref, bbg_ref, bba_ref, bbm_ref):
    pairs = [(wg0_ref, bg0_ref), (wg1_ref, bg1_ref), (wg2_ref, bg2_ref),
             (wbg_ref, bbg_ref), (wba_ref, bba_ref), (wbm_ref, bbm_ref)]

    @pl.when(_first_inner())
    def _():
        for w_ref, b_ref in pairs:
            b_ref[...] = w_ref[...].astype(BF)

    def merged(xn, og, oa, om):
        def branch(bg_ref, o, bb_ref):
            return jax.nn.sigmoid(_dot(xn, bg_ref[...])) * _dot(o, bb_ref[...])
        return (branch(bg0_ref, og, bbg_ref) + branch(bg1_ref, oa, bba_ref) + branch(bg2_ref, om, bbm_ref)).astype(BF)

    out_ref[...] = merged(xn_ref[...], og_ref[...], oa_ref[...], om_ref[...])

    @pl.when(_last_inner())
    def _():
        outs_ref[...] = merged(xns_ref[...], ogs_ref[...], oas_ref[...], oms_ref[...])


def _merge(xn, og, oa, om, xns, ogs, oas, oms, w_in, wbg, wba, wbm, l, tm, tn):
    m, ms = xn.shape[0], xns.shape[0]
    nt = D_MODEL // tn
    row = lambda n: pl.BlockSpec((tm, n), lambda j, i: (i, 0))
    gate = lambda b: pl.BlockSpec((None, D_MODEL, tn), lambda j, i: (l, 0, C_G // tn + b * nt + j))
    col = lambda n: pl.BlockSpec((None, n, tn), lambda j, i: (l, 0, j))
    widths = (G_WIDTH, SWA_WIDTH, MEM_WIDTH)
    return pl.pallas_call(
        _merge_kernel,
        grid=(nt, m // tm),
        in_specs=[row(D_MODEL)] + [row(n) for n in widths]
                 + [_resident((ms, D_MODEL))] + [_resident((ms, n)) for n in widths]
                 + [gate(0), gate(1), gate(2)] + [col(n) for n in widths],
        out_specs=[pl.BlockSpec((tm, tn), lambda j, i: (i, j)), pl.BlockSpec((ms, tn), lambda j, i: (0, j))],
        out_shape=[jax.ShapeDtypeStruct((m, D_MODEL), BF), jax.ShapeDtypeStruct((ms, D_MODEL), BF)],
        scratch_shapes=[pltpu.VMEM((D_MODEL, tn), BF)] * 3 + [pltpu.VMEM((n, tn), BF) for n in widths],
        compiler_params=_params("arbitrary", "arbitrary"),
        name="merge",
    )(xn, og, oa, om, xns, ogs, oas, oms, w_in, w_in, w_in, wbg, wba, wbm)


def _out_kernel(x_ref, mg_ref, xs_ref, mgs_ref, w_ref, g_ref, xo_ref, xn_ref, xos_ref, xns_ref, wbf_ref,
                *, l, tk, ncast):
    s = pl.program_id(0)

    @pl.when(s < ncast)
    def _():
        wbf_ref[pl.ds(pl.multiple_of(s * tk, tk), tk), :] = w_ref[...].astype(BF)

    def tile(x_ref, mg_ref, xo_ref, xn_ref):
        x = x_ref[...] + _dot(mg_ref[...], wbf_ref[...])
        xo_ref[...] = x
        xn_ref[...] = _rms(x, g_ref[l:l + 1, :]).astype(BF)

    @pl.when(s >= ncast)
    def _():
        tile(x_ref, mg_ref, xo_ref, xn_ref)

    @pl.when(s == pl.num_programs(0) - 1)
    def _():
        tile(xs_ref, mgs_ref, xos_ref, xns_ref)


def _out_proj(x, merged, xs, mergeds, w_out, g, l, tm, tk):
    m, ms = x.shape[0], xs.shape[0]
    ncast = D_MODEL // tk
    row = pl.BlockSpec((tm, D_MODEL), lambda s: (jnp.maximum(s - ncast, 0), 0))
    srow = pl.BlockSpec((ms, D_MODEL), lambda s: (0, 0))
    return pl.pallas_call(
        functools.partial(_out_kernel, l=l, tk=tk, ncast=ncast),
        grid=(ncast + m // tm,),
        in_specs=[row, row, _resident((ms, D_MODEL)), _resident((ms, D_MODEL)),
                  pl.BlockSpec((None, tk, D_MODEL), lambda s: (l, jnp.minimum(s, ncast - 1), 0)),
                  _resident(g.shape)],
        out_specs=[row, row, srow, srow],
        out_shape=[jax.ShapeDtypeStruct((m, D_MODEL), F32), jax.ShapeDtypeStruct((m, D_MODEL), BF),
                   jax.ShapeDtypeStruct((ms, D_MODEL), F32), jax.ShapeDtypeStruct((ms, D_MODEL), BF)],
        scratch_shapes=[pltpu.VMEM((D_MODEL, D_MODEL), BF)],
        compiler_params=_params("arbitrary"),
        name="out_proj",
    )(x, merged, xs, mergeds, w_out, g)


def _up_kernel(xn_ref, xns_ref, wa_ref, wb_ref, cwa_ref, cwb_ref, cba_ref, cbb_ref, sta_ref, stb_ref,
               g_ref, ca_ref, cb_ref, gs_ref, has_ref, hbs_ref, ha_ref, hb_ref, ba_ref, bb_ref,
               *, l, tm, tiles_per_seq):
    @pl.when(_first_inner())
    def _():
        ba_ref[...] = wa_ref[...].astype(BF)
        bb_ref[...] = wb_ref[...].astype(BF)

    @pl.when(pl.program_id(1) % tiles_per_seq == 0)
    def _():
        ha_ref[...] = jnp.zeros(ha_ref.shape, F32)
        hb_ref[...] = jnp.zeros(hb_ref.shape, F32)

    top = lax.broadcasted_iota(jnp.int32, (SUBLANES, ha_ref.shape[1]), 0)
    xn = xn_ref[...]

    def conv(h, prev_ref, cw_ref, bias_ref, tail_ref):
        prev = prev_ref[...]
        acc = bias_ref[l:l + 1, :]
        for j in range(CONV_W - 1):
            back = CONV_W - 1 - j
            rolled = pltpu.roll(h, back, axis=0)
            head = jnp.where(top < back, pltpu.roll(prev, back, axis=0), rolled[0:SUBLANES])
            acc = acc + jnp.concatenate([head, rolled[SUBLANES:]], axis=0) * cw_ref[j:j + 1, :]
        tail = h[tm - SUBLANES:]
        tail_ref[...] = tail
        prev_ref[...] = tail
        return acc + h * cw_ref[CONV_W - 1:CONV_W, :]

    a = conv(_dot(xn, ba_ref[...]), ha_ref, cwa_ref, cba_ref, ca_ref)
    b = conv(_dot(xn, bb_ref[...]), hb_ref, cwb_ref, cbb_ref, cb_ref)
    g_ref[...] = (jax.nn.gelu(a) * b).astype(BF)

    @pl.when(_last_inner())
    def _():
        xns = xns_ref[...]

        def conv_s(bw_ref, cw_ref, bias_ref, st_ref, h_out_ref):
            h = _dot(xns, bw_ref[...])
            h_out_ref[...] = h
            return (bias_ref[l:l + 1, :] + st_ref[:, 0, :] * cw_ref[0:1, :] + st_ref[:, 1, :] * cw_ref[1:2, :]
                    + h * cw_ref[2:3, :])

        a_s = conv_s(ba_ref, cwa_ref, cba_ref, sta_ref, has_ref)
        b_s = conv_s(bb_ref, cwb_ref, cbb_ref, stb_ref, hbs_ref)
        gs_ref[...] = (jax.nn.gelu(a_s) * b_s).astype(BF)


def _up_conv(xn, xns, w_up, conv_w, conv_b, state, l, batch, seq, tm, tn):
    m, ms = xn.shape[0], xns.shape[0]
    nt = D_FF // tn
    nrow = m // tm
    tiles_per_seq = seq // tm
    half = lambda off: (lambda j, i: (l, 0, off * nt + j))
    wspec = lambda off: pl.BlockSpec((None, D_MODEL, tn), half(off))
    cwspec = lambda off: pl.BlockSpec((None, CONV_W, tn), half(off))
    cbspec = lambda off: pl.BlockSpec((DEPTH, tn), lambda j, i: (0, off * nt + j))
    stspec = lambda off: pl.BlockSpec((None, ms, CONV_W - 1, tn), lambda j, i: (l, 0, 0, off * nt + j))
    tail = pl.BlockSpec((SUBLANES, tn), lambda j, i: (i // tiles_per_seq, j))
    scol = pl.BlockSpec((ms, tn), lambda j, i: (0, j))
    return pl.pallas_call(
        functools.partial(_up_kernel, l=l, tm=tm, tiles_per_seq=tiles_per_seq),
        grid=(nt, nrow),
        in_specs=[pl.BlockSpec((tm, D_MODEL), lambda j, i: (i, 0)), _resident((ms, D_MODEL)),
                  wspec(0), wspec(1), cwspec(0), cwspec(1), cbspec(0), cbspec(1), stspec(0), stspec(1)],
        out_specs=[pl.BlockSpec((tm, tn), lambda j, i: (i, j)), tail, tail, scol, scol, scol],
        out_shape=[jax.ShapeDtypeStruct((m, D_FF), BF),
                   jax.ShapeDtypeStruct((batch * SUBLANES, D_FF), F32),
                   jax.ShapeDtypeStruct((batch * SUBLANES, D_FF), F32),
                   jax.ShapeDtypeStruct((ms, D_FF), BF),
                   jax.ShapeDtypeStruct((ms, D_FF), F32), jax.ShapeDtypeStruct((ms, D_FF), F32)],
        scratch_shapes=[pltpu.VMEM((SUBLANES, tn), F32)] * 2 + [pltpu.VMEM((D_MODEL, tn), BF)] * 2,
        compiler_params=_params("arbitrary", "arbitrary"),
        name="up_conv_glu",
    )(xn, xns, w_up, w_up, conv_w, conv_w, conv_b, conv_b, state, state)


def _down_kernel(x_ref, g_ref, xs_ref, gs_ref, w_ref, gain_ref, *rest, tk, ncast, emit_x):
    wbf_ref = rest[-1]
    outs = rest[:-1]
    s = pl.program_id(0)

    @pl.when(s < ncast)
    def _():
        wbf_ref[pl.ds(pl.multiple_of(s * tk, tk), tk), :] = w_ref[...].astype(BF)

    def tile(x, g, o_refs):
        x = x + _dot(g, wbf_ref[...])
        if emit_x:
            o_refs[0][...] = x
        o_refs[-1][...] = _rms(x, gain_ref[...]).astype(o_refs[-1].dtype)

    nout = len(outs) // 2

    @pl.when(s >= ncast)
    def _():
        tile(x_ref[...], g_ref[...], outs[:nout])

    @pl.when(s == pl.num_programs(0) - 1)
    def _():
        tile(xs_ref[...], gs_ref[...], outs[nout:])


def _down_proj(x, g, xs, gs, w_down, gain, l, tm, tk, norm_dtype, emit_x):
    m, ms = x.shape[0], xs.shape[0]
    ncast = D_FF // tk
    rows = lambda n: pl.BlockSpec((tm, n), lambda s: (jnp.maximum(s - ncast, 0), 0))
    fixed = lambda n: pl.BlockSpec((ms, n), lambda s: (0, 0))
    dts = ([F32] if emit_x else []) + [norm_dtype]
    return pl.pallas_call(
        functools.partial(_down_kernel, tk=tk, ncast=ncast, emit_x=emit_x),
        grid=(ncast + m // tm,),
        in_specs=[rows(D_MODEL), rows(D_FF), _resident((ms, D_MODEL)), _resident((ms, D_FF)),
                  pl.BlockSpec((None, tk, D_MODEL), lambda s: (l, jnp.minimum(s, ncast - 1), 0)),
                  _resident((1, D_MODEL))],
        out_specs=[rows(D_MODEL) for _ in dts] + [fixed(D_MODEL) for _ in dts],
        out_shape=[jax.ShapeDtypeStruct((m, D_MODEL), dt) for dt in dts]
                  + [jax.ShapeDtypeStruct((ms, D_MODEL), dt) for dt in dts],
        scratch_shapes=[pltpu.VMEM((D_FF, D_MODEL), BF)],
        compiler_params=_params("arbitrary"),
        name="down_proj",
    )(x, g, xs, gs, w_down, gain)


def _conv_state_kernel(st_ref, *rest, nt):
    o_ref = rest[-1]
    first_half = pl.program_id(0) < nt
    for l in range(DEPTH):
        ha_ref, hb_ref = rest[2 * l], rest[2 * l + 1]
        o_ref[l, :, 0, :] = st_ref[l, :, CONV_W - 2, :]
        o_ref[l, :, 1, :] = jnp.where(first_half, ha_ref[...], hb_ref[...])


def _conv_state(state, halves, tn):
    depth, ms = state.shape[:2]
    nt = D_FF // tn
    blk = pl.BlockSpec((depth, ms, CONV_W - 1, tn), lambda j: (0, 0, 0, j))
    a_spec = pl.BlockSpec((ms, tn), lambda j: (0, jnp.minimum(j, nt - 1)))
    b_spec = pl.BlockSpec((ms, tn), lambda j: (0, jnp.maximum(j - nt, 0)))
    return pl.pallas_call(
        functools.partial(_conv_state_kernel, nt=nt),
        grid=(2 * nt,),
        in_specs=[blk] + [a_spec, b_spec] * depth,
        out_specs=blk,
        out_shape=jax.ShapeDtypeStruct(state.shape, F32),
        compiler_params=_params("arbitrary"),
        name="conv_state_sample",
    )(state, *halves)


def _layer(l, last, next_gain, xp, xs, xn, xns, mem, batch, seq, kt, vt, mk4, mv4, state_conv, slope_b, sink_b,
           norm_mix_g, w_in, gmlp_norm_g, gmlp_ws, gmlp_bs, attn_sinks, mem_norm_g, w_mem_kv,
           w_br_g, w_br_a, w_br_m, w_out, norm_ffn_g, w_up, conv_w, conv_b, w_down):
    dec = xs.shape[0]
    nreq = 8
    bs = gmlp_bs[l]
    bsb = jnp.repeat(bs.T, G_GDIM, axis=1)
    mkv = _memkv(mem, mem_norm_g, w_mem_kv, l, 256)
    front = _front(attn_sinks, xp if xn is None else xn, xs if xn is None else xns, norm_mix_g, w_in,
                   gmlp_ws, gmlp_norm_g, bsb, mkv, l, seq, 512)
    kv, oa, og, vg_last, om, zs = front[:6]
    if xn is None:
        xn, xns = front[6:]

    coef = jnp.stack([jnp.repeat(gmlp_ws[l, :, 0, 0], G_GDIM), jnp.repeat(bs[:, 0], G_GDIM)])
    ogs, vgs = _gmlp_sample(zs, gmlp_norm_g, coef, l)

    q3 = zs[:, C_Q:C_K].reshape(dec, SWA_HEADS, SWA_HD)
    newcol = lambda a, b: zs[:, a:b].reshape(dec // nreq, nreq, SWA_KV_WIDTH).transpose(0, 2, 1)
    oas = _swa_sample(q3, kt, vt, zs, slope_b, sink_b, l, nreq)

    oms = _mem_sample(zs[:, C_M:C_G].reshape(dec, MEM_HEADS, MEM_HD), mk4, mv4, l, nreq)

    merged, mergeds = _merge(xn, og, oa, om, xns, ogs, oas.reshape(dec, SWA_WIDTH).astype(BF),
                             oms.reshape(dec, MEM_WIDTH).astype(BF), w_in, w_br_g, w_br_a, w_br_m, l, 1024, 256)
    xp, xn2, xs, xn2s = _out_proj(xp, merged, xs, mergeds, w_out, norm_ffn_g, l, 512, 256)
    g, ca, cb, gs, has, hbs = _up_conv(xn2, xn2s, w_up, conv_w, conv_b, state_conv, l, batch, seq, 1024, 512)
    outs = _down_proj(xp, g, xs, gs, w_down, next_gain, l, 256, 256, F32 if last else BF, not last)

    tail = lambda t: t.reshape(batch, SUBLANES, D_FF)[:, SUBLANES - (CONV_W - 1):]
    kvb = kv.reshape(batch, seq, 2 * SWA_KV_WIDTH)[:, seq - WINDOW:]
    st = dict(
        pk=kvb[:, :, :SWA_KV_WIDTH].reshape(batch, WINDOW, SWA_KV, SWA_HD),
        pv=kvb[:, :, SWA_KV_WIDTH:].reshape(batch, WINDOW, SWA_KV, SWA_HD),
        mk=mkv[:, :MEM_WIDTH].reshape(batch, MEM_LEN, MEM_HEADS, MEM_HD),
        mv=mkv[:, MEM_WIDTH:].reshape(batch, MEM_LEN, MEM_HEADS, MEM_HD),
        gvp=vg_last.reshape(batch, CHUNK, G_WIDTH),
        gvs=vgs.reshape(dec, 1, G_WIDTH),
        cvp=jnp.concatenate([tail(ca), tail(cb)], axis=-1),
        cvs=(has, hbs),
        knc=newcol(C_K, C_VV),
        vnc=newcol(C_VV, C_M),
    )
    return outs, st


def kernel(x_prompt, x_sample, cache_swa_k, cache_swa_v, cache_mem_k, cache_mem_v, state_conv, mem_prompt, norm_mix_g, w_in, gmlp_norm_g, gmlp_ws, gmlp_bs, attn_sinks, mem_norm_g, w_mem_kv, w_br_g, w_br_a, w_br_m, w_out, norm_ffn_g, w_up, conv_w, conv_b, w_down, final_norm_g):
    batch, seq, _ = x_prompt.shape
    dec = x_sample.shape[0]
    assert x_sample.shape[1] == 1 and PAST_LEN % CHUNK == 0 and PAST_LEN >= WINDOW
    assert seq % 2048 == 0 and seq >= WINDOW and (seq - 1) // CHUNK * CHUNK == seq - CHUNK
    xp = x_prompt.reshape(batch * seq, D_MODEL)
    xs = x_sample.reshape(dec, D_MODEL)
    mem = mem_prompt.reshape(batch * MEM_LEN, D_MODEL)
    kt = cache_swa_k.transpose(0, 1, 3, 4, 2)
    vt = cache_swa_v.transpose(0, 1, 3, 4, 2)
    mk4 = cache_mem_k.reshape(DEPTH, dec, MEM_LEN * MEM_HEADS, MEM_HD)
    mv4 = cache_mem_v.reshape(DEPTH, dec, MEM_LEN * MEM_HEADS, MEM_HD)
    slope_b = jnp.broadcast_to(jnp.asarray(SLOPES, F32)[:, None], (SWA_HEADS, LANES))
    sink_b = jnp.broadcast_to(attn_sinks[:, :, None], (DEPTH, SWA_HEADS, LANES))
    sts = []
    xn = xns = None
    for l in range(DEPTH):
        last = l == DEPTH - 1
        next_gain = (final_norm_g if last else norm_mix_g[l + 1]).reshape(1, D_MODEL)
        outs, st = _layer(l, last, next_gain, xp, xs, xn, xns, mem, batch, seq, kt, vt, mk4, mv4, state_conv,
                          slope_b, sink_b,
                          norm_mix_g, w_in, gmlp_norm_g, gmlp_ws, gmlp_bs, attn_sinks, mem_norm_g,
                          w_mem_kv, w_br_g, w_br_a, w_br_m, w_out, norm_ffn_g, w_up, conv_w, conv_b, w_down)
        sts.append(st)
        if last:
            y_prompt, y_sample = outs
        else:
            xp, xn, xs, xns = outs
    stack = lambda key: jnp.stack([s[key] for s in sts])
    nk, nv = _window_update(kt, vt, stack("knc"), stack("vnc"), 8)
    return (y_prompt.reshape(batch, seq, D_MODEL), y_sample.reshape(dec, 1, D_MODEL), stack("pk"), stack("pv"),
            nk.transpose(0, 1, 4, 2, 3), nv.transpose(0, 1, 4, 2, 3),
            stack("mk"), stack("mv"), stack("gvp"), stack("gvs"), stack("cvp"),
            _conv_state(state_conv, [h for s in sts for h in s["cvs"]], 512))
```

```python
import functools
import math

import jax
import jax.numpy as jnp
from jax import lax
from jax.experimental import pallas as pl
from jax.experimental.pallas import tpu as pltpu

D_MODEL = 2048
DEPTH = 2
PAST_LEN = 8192
MEM_LEN = 256
CHUNK = 128
G_GROUPS = 8
G_WIDTH = 768
G_GDIM = G_WIDTH // G_GROUPS
SWA_HEADS = 12
SWA_KV = 4
SWA_GROUP = SWA_HEADS // SWA_KV
SWA_HD = 64
SWA_WIDTH = SWA_HEADS * SWA_HD
SWA_KV_WIDTH = SWA_KV * SWA_HD
WINDOW = 128
MEM_HEADS = 4
MEM_HD = 128
MEM_WIDTH = MEM_HEADS * MEM_HD
N_BRANCH = 3
D_FF = 5632
CONV_W = 3
EPS = 1e-6

LANES = 128
SUBLANES = 8
VMEM_LIMIT = 56 * 1024 * 1024

C_U, C_V, C_Q, C_K, C_VV, C_M, C_G = 0, 768, 1536, 2304, 2560, 2816, 3328
ZT = 256
N_ACT = C_Q // ZT
assert all(c % ZT == 0 for c in (C_V, C_Q, C_K, C_VV, C_M, C_G))

BF = jnp.bfloat16
F32 = jnp.float32
NT_DIMS = (((1,), (1,)), ((), ()))


def _alibi_slopes(n):
    p = 2 ** int(math.floor(math.log2(n)))
    base = [2.0 ** (-8.0 * (i + 1) / p) for i in range(p)]
    extra = [2.0 ** (-8.0 * (2 * i + 1) / (2 * p)) for i in range(n - p)]
    return base + extra


SLOPES = _alibi_slopes(SWA_HEADS)


def _params(*sem):
    return pltpu.CompilerParams(dimension_semantics=sem, vmem_limit_bytes=VMEM_LIMIT)


def _resident(shape):
    return pl.BlockSpec(shape, lambda *_: (0,) * len(shape), pipeline_mode=pl.Buffered(1))


def _rms(x, g):
    ms = jnp.mean(x * x, axis=-1, keepdims=True)
    return x * lax.rsqrt(ms + EPS) * g


def _dot(a, b):
    return jnp.dot(a, b, preferred_element_type=F32)


def _first_inner():
    return pl.program_id(1) == 0


def _last_inner():
    return pl.program_id(1) == pl.num_programs(1) - 1


NZ = C_G // ZT


QKV_TILES = range(C_Q // ZT, C_M // ZT)
QKV_W = C_M - C_Q


def _swa_bias_table(bias_ref):
    r = lax.broadcasted_iota(jnp.int32, (WINDOW, 2 * WINDOW), 0)
    c = lax.broadcasted_iota(jnp.int32, (WINDOW, 2 * WINDOW), 1)
    dist = r + WINDOW - c
    valid = (dist >= 0) & (dist <= WINDOW)
    distf = dist.astype(F32)
    for h in range(SWA_HEADS):
        j, g = divmod(h, SWA_GROUP)
        pen = -SLOPES[h] * distf
        bias_ref[j, g * WINDOW:(g + 1) * WINDOW, :] = jnp.where(valid, pen, -jnp.inf)
        bias_ref[SWA_KV + j, g * WINDOW:(g + 1) * WINDOW, :] = jnp.where(valid & (c >= WINDOW), pen, -jnp.inf)


def _front_kernel(sink_ref, xn_ref, xs_ref, ng_ref, w_ref, ws_ref, gg_ref, bsb_ref, mkv_ref,
                  kv_ref, oa_ref, og_ref, vg_ref, om_ref, zs_ref, *rest, l, tm, tiles_per_seq, raw, group=4):
    xno_ref, xnso_ref = rest[:2] if raw else (None, None)
    wbf_ref, qkv_ref, zt_ref, carry_ref, bias_ref = rest[-5:]
    s = pl.program_id(0)

    def normed(x_ref, out_ref):
        if not raw:
            return x_ref[...]
        x = _rms(x_ref[...], ng_ref[l:l + 1, :]).astype(BF)
        out_ref[...] = x
        return x

    rows = SWA_GROUP * WINDOW
    kcol, vcol = C_K - C_Q, C_VV - C_Q
    mcol = C_Q

    @pl.when(s < NZ)
    def _():
        wbf_ref[s] = w_ref[...].astype(BF)

    @pl.when(s == NZ)
    def _():
        _swa_bias_table(bias_ref)
        carry_ref[...] = jnp.zeros(carry_ref.shape, F32)

    def column(x, c):
        h = _dot(x, wbf_ref[c])
        return jax.nn.gelu(h) if c < N_ACT else h

    @pl.when(s >= NZ)
    def _():
        x = normed(xn_ref, xno_ref)
        for c in QKV_TILES:
            qkv_ref[:, c * ZT - C_Q:(c + 1) * ZT - C_Q] = column(x, c)
        kv_ref[...] = qkv_ref[:, kcol:]

        def project(c, col):
            def task():
                zt_ref[:, col:col + ZT] = column(x, c)
            return task

        tasks = [project(c, c * ZT) for c in range(N_ACT)]
        tasks += [project(c, mcol + (c - C_M // ZT) * ZT) for c in range(C_M // ZT, NZ)]

        r = lax.broadcasted_iota(jnp.int32, (CHUNK, CHUNK), 0)
        cc = lax.broadcasted_iota(jnp.int32, (CHUNK, CHUNK), 1)
        lane = lax.broadcasted_iota(jnp.int32, (CHUNK, LANES), 1)

        def gmlp_chunk(n):
            def task():
                wtri = [jnp.where(r >= cc, ws_ref[g], 0.0).astype(BF) for g in range(G_GROUPS)]
                crow = slice(n * CHUNK, (n + 1) * CHUNK)
                vg = _rms(zt_ref[crow, C_V:C_Q], gg_ref[l:l + 1, :])
                if n == tm // CHUNK - 1:
                    vg_ref[...] = vg
                for t in range(G_TILES):
                    cols = slice(t * LANES, (t + 1) * LANES)
                    wst = jnp.concatenate([wtri[GA[t]], wtri[GA[t] + 1]], axis=0)
                    both = _dot(wst, vg[:, cols].astype(BF))
                    mix = jnp.where(lane < BND[t], both[:CHUNK], both[CHUNK:]) + bsb_ref[:, cols]
                    og_ref[crow, cols] = (zt_ref[crow, cols] * mix).astype(BF)
            return task

        tasks += [gmlp_chunk(n) for n in range(tm // CHUNK)]

        def mem_head(h):
            def task():
                cols = slice(h * MEM_HD, (h + 1) * MEM_HD)
                q = zt_ref[:, mcol + h * MEM_HD:mcol + (h + 1) * MEM_HD].astype(BF)
                sc = lax.dot_general(q, mkv_ref[:, cols].astype(BF), NT_DIMS,
                                     preferred_element_type=F32) * (MEM_HD ** -0.5)
                e = jnp.exp(sc - jnp.max(sc, axis=-1, keepdims=True))
                p = (e / jnp.sum(e, axis=-1, keepdims=True)).astype(BF)
                vcols = slice(MEM_WIDTH + h * MEM_HD, MEM_WIDTH + (h + 1) * MEM_HD)
                om_ref[:, cols] = _dot(p, mkv_ref[:, vcols].astype(BF)).astype(BF)
            return task

        tasks += [mem_head(h) for h in range(MEM_HEADS)]

        units = [(b, j) for b in range(tm // WINDOW) for j in range(SWA_KV)]
        groups = [units[k:k + group] for k in range(0, len(units), group)]
        per = len(tasks) // len(groups)
        assert per * len(groups) == len(tasks) and per % 2 == 0
        head = lax.broadcasted_iota(jnp.int32, (rows, 1), 0) // WINDOW
        fresh = ((s - NZ) % tiles_per_seq == 0).astype(jnp.int32) * SWA_KV

        def scores(b, j):
            own = slice(b * WINDOW, (b + 1) * WINDOW)
            kvh = lambda base: slice(base + j * SWA_HD, base + (j + 1) * SWA_HD)
            if b == 0:
                kprev, vprev = carry_ref[:, kvh(0)], carry_ref[:, kvh(SWA_KV_WIDTH)]
            else:
                prev = slice((b - 1) * WINDOW, b * WINDOW)
                kprev, vprev = qkv_ref[prev, kvh(kcol)], qkv_ref[prev, kvh(vcol)]
            kj = jnp.concatenate([kprev, qkv_ref[own, kvh(kcol)]], axis=0).astype(BF)
            vj = jnp.concatenate([vprev, qkv_ref[own, kvh(vcol)]], axis=0).astype(BF)
            hs = range(j * SWA_GROUP, (j + 1) * SWA_GROUP)
            q = jnp.concatenate([qkv_ref[own, h * SWA_HD:(h + 1) * SWA_HD] for h in hs], axis=0).astype(BF)
            sink = jnp.full((rows, 1), sink_ref[l, hs[-1]], F32)
            for g in range(SWA_GROUP - 1):
                sink = jnp.where(head == g, sink_ref[l, hs[g]], sink)
            bias = bias_ref[(fresh if b == 0 else 0) + j]
            sc = lax.dot_general(q, kj, NT_DIMS, preferred_element_type=F32) * (SWA_HD ** -0.5) + bias
            return sc, sink, vj

        def softmax(sc, sink, vj):
            mx = jnp.maximum(jnp.max(sc, axis=-1, keepdims=True), sink)
            p = jnp.exp(sc - mx)
            den = jnp.sum(p, axis=-1, keepdims=True) + jnp.exp(sink - mx)
            return p.astype(BF), den, vj

        def values(b, j, p, den, vj):
            o = _dot(p, vj) / den
            own = slice(b * WINDOW, (b + 1) * WINDOW)
            for g, h in enumerate(range(j * SWA_GROUP, (j + 1) * SWA_GROUP)):
                oa_ref[own, h * SWA_HD:(h + 1) * SWA_HD] = o[g * WINDOW:(g + 1) * WINDOW].astype(BF)

        for k, grp in enumerate(groups):
            for task in tasks[per * k:per * k + per // 2]:
                task()
            stage = [scores(b, j) for b, j in grp]
            stage = [softmax(*args) for args in stage]
            for task in tasks[per * k + per // 2:per * (k + 1)]:
                task()
            for (b, j), args in zip(grp, stage):
                values(b, j, *args)
        last = slice(tm - WINDOW, tm)
        carry_ref[...] = qkv_ref[last, kcol:]

    @pl.when(s == pl.num_programs(0) - 1)
    def _():
        xs = normed(xs_ref, xnso_ref)
        for c in range(NZ):
            zs_ref[:, c * ZT:(c + 1) * ZT] = column(xs, c)


def _front(sinks, xn, xs, ng, w_in, ws, gg, bsb, mkv, l, seq, tm):
    m, ms = xn.shape[0], xs.shape[0]
    raw = xn.dtype == F32
    tiles_per_seq = seq // tm
    tile = lambda s: jnp.maximum(s - NZ, 0)
    rows = lambda n: pl.BlockSpec((tm, n), lambda s: (tile(s), 0))
    fixed = lambda n: pl.BlockSpec((ms, n), lambda s: (0, 0))
    per_seq = lambda r, n: pl.BlockSpec((r, n), lambda s: (tile(s) // tiles_per_seq, 0))
    outs = [(2 * SWA_KV_WIDTH, F32), (SWA_WIDTH, BF), (G_WIDTH, BF)]
    return pl.pallas_call(
        functools.partial(_front_kernel, l=l, tm=tm, tiles_per_seq=tiles_per_seq, raw=raw),
        grid=(NZ + m // tm,),
        in_specs=[pl.BlockSpec(memory_space=pltpu.SMEM), rows(D_MODEL), _resident((ms, D_MODEL)), _resident(ng.shape),
                  pl.BlockSpec((None, D_MODEL, ZT), lambda s: (l, 0, jnp.minimum(s, NZ - 1))),
                  pl.BlockSpec((None, G_GROUPS, CHUNK, CHUNK), lambda s: (l, 0, 0, 0)),
                  _resident(gg.shape), _resident((CHUNK, G_WIDTH)), per_seq(MEM_LEN, 2 * MEM_WIDTH)],
        out_specs=[rows(n) for n, _ in outs] + [per_seq(CHUNK, G_WIDTH), rows(MEM_WIDTH), fixed(C_G)]
                  + ([rows(D_MODEL), fixed(D_MODEL)] if raw else []),
        out_shape=[jax.ShapeDtypeStruct((m, n), dt) for n, dt in outs]
                  + [jax.ShapeDtypeStruct((m // seq * CHUNK, G_WIDTH), F32), jax.ShapeDtypeStruct((m, MEM_WIDTH), BF),
                     jax.ShapeDtypeStruct((ms, C_G), F32)]
                  + ([jax.ShapeDtypeStruct((m, D_MODEL), BF), jax.ShapeDtypeStruct((ms, D_MODEL), BF)] if raw else []),
        scratch_shapes=[pltpu.VMEM((NZ, D_MODEL, ZT), BF), pltpu.VMEM((tm, QKV_W), F32),
                        pltpu.VMEM((tm, C_Q + MEM_WIDTH), F32), pltpu.VMEM((WINDOW, 2 * SWA_KV_WIDTH), F32),
                        pltpu.VMEM((2 * SWA_KV, SWA_GROUP * WINDOW, 2 * WINDOW), F32)],
        compiler_params=_params("arbitrary"),
        name="front",
    )(sinks, xn, xs, ng, w_in, ws, gg, bsb, mkv)


G_TILES = G_WIDTH // LANES
GA = [(t * LANES) // G_GDIM for t in range(G_TILES)]
BND = [(GA[t] + 1) * G_GDIM - t * LANES for t in range(G_TILES)]
assert all(0 < b < LANES and (GA[t] + 2) * G_GDIM >= (t + 1) * LANES for t, b in enumerate(BND))


def _gmlp_sample_kernel(zs_ref, gg_ref, coef_ref, og_ref, vg_ref, *, l):
    vg = _rms(zs_ref[:, C_V:C_Q], gg_ref[l:l + 1, :])
    vg_ref[...] = vg
    og_ref[...] = (zs_ref[:, C_U:C_V] * (coef_ref[0:1, :] * vg + coef_ref[1:2, :])).astype(BF)


def _gmlp_sample(zs, gg, coef, l):
    m = zs.shape[0]
    return pl.pallas_call(
        functools.partial(_gmlp_sample_kernel, l=l),
        out_shape=[jax.ShapeDtypeStruct((m, G_WIDTH), BF), jax.ShapeDtypeStruct((m, G_WIDTH), F32)],
        name="gmlp_sample",
    )(zs, gg, coef)


def _swa_sample_kernel(q_ref, kt_ref, vt_ref, kn_ref, vn_ref, slope_ref, sink_ref, o_ref, *, nreq):
    kvw = SWA_KV_WIDTH
    hrow = lax.broadcasted_iota(jnp.int32, (SWA_HEADS, kvw), 0) // SWA_GROUP
    lblk = lax.broadcasted_iota(jnp.int32, (SWA_HEADS, kvw), 1) // SWA_HD
    own = hrow == lblk
    kvh = lax.broadcasted_iota(jnp.int32, (SWA_HEADS, SWA_HD), 0) // SWA_GROUP
    c = lax.broadcasted_iota(jnp.int32, (SWA_HEADS, WINDOW), 1)
    bias = slope_ref[...] * (WINDOW - c).astype(F32)
    sink = sink_ref[:, 0:1]
    scale = SWA_HD ** -0.5
    q = q_ref[...]
    qm = jnp.where(own, jnp.concatenate([q] * SWA_KV, axis=2), 0.0).astype(BF)
    kt = kt_ref[...].reshape(nreq, kvw, WINDOW)
    vt = vt_ref[...].reshape(nreq, kvw, WINDOW)
    rows = lambda ref: jnp.stack([ref[r:r + 1, :] for r in range(nreq)]).astype(BF).astype(F32)
    kn, vn = rows(kn_ref), rows(vn_ref)
    s_c = jnp.einsum("rhk,rkc->rhc", qm, kt.astype(BF), preferred_element_type=F32) * scale - bias
    s_n = jnp.sum(qm.astype(F32) * kn, axis=-1, keepdims=True) * scale
    mx = jnp.maximum(jnp.maximum(jnp.max(s_c, axis=-1, keepdims=True), s_n), sink)
    p_c = jnp.exp(s_c - mx)
    p_n = jnp.exp(s_n - mx)
    den = jnp.sum(p_c, axis=-1, keepdims=True) + p_n + jnp.exp(sink - mx)
    p_c = (p_c / den).astype(BF)
    p_n = (p_n / den).astype(BF).astype(F32)
    o_all = jnp.einsum("rhc,rkc->rhk", p_c, vt.astype(BF), preferred_element_type=F32) + p_n * vn
    o = jnp.zeros((nreq, SWA_HEADS, SWA_HD), F32)
    for j in range(SWA_KV):
        o = jnp.where(kvh == j, o_all[:, :, j * SWA_HD:(j + 1) * SWA_HD], o)
    o_ref[...] = o


def _swa_sample(q3, kt, vt, zs, slope_b, sink_b, l, nreq):
    m = q3.shape[0]
    cache = pl.BlockSpec((None, nreq, SWA_KV, SWA_HD, WINDOW), lambda i: (l, i, 0, 0, 0))
    return pl.pallas_call(
        functools.partial(_swa_sample_kernel, nreq=nreq),
        grid=(m // nreq,),
        in_specs=[pl.BlockSpec((nreq, SWA_HEADS, SWA_HD), lambda i: (i, 0, 0)), cache, cache,
                  pl.BlockSpec((nreq, SWA_KV_WIDTH), lambda i: (i, C_K // SWA_KV_WIDTH)),
                  pl.BlockSpec((nreq, SWA_KV_WIDTH), lambda i: (i, C_VV // SWA_KV_WIDTH)),
                  _resident((SWA_HEADS, LANES)), pl.BlockSpec((None, SWA_HEADS, LANES), lambda i: (l, 0, 0))],
        out_specs=pl.BlockSpec((nreq, SWA_HEADS, SWA_HD), lambda i: (i, 0, 0)),
        out_shape=jax.ShapeDtypeStruct((m, SWA_HEADS, SWA_HD), F32),
        compiler_params=_params("arbitrary"),
        name="swa_sample",
    )(q3, kt, vt, zs, zs, slope_b, sink_b)


def _window_update_kernel(kt_ref, vt_ref, knc_ref, vnc_ref, nk_ref, nv_ref, *, nreq):
    last = lax.broadcasted_iota(jnp.int32, (SWA_KV_WIDTH, WINDOW), 1) == WINDOW - 1
    for src_ref, col_ref, dst_ref in ((kt_ref, knc_ref, nk_ref), (vt_ref, vnc_ref, nv_ref)):
        for l in range(DEPTH):
            for r in range(nreq):
                old = src_ref[l, r].reshape(SWA_KV_WIDTH, WINDOW)
                new = jnp.where(last, col_ref[l, :, r:r + 1], pltpu.roll(old, WINDOW - 1, axis=1))
                dst_ref[l, r] = new.reshape(SWA_KV, SWA_HD, WINDOW)


def _window_update(kt, vt, knc, vnc, nreq):
    depth, m = kt.shape[:2]
    cache = pl.BlockSpec((depth, nreq, SWA_KV, SWA_HD, WINDOW), lambda i: (0, i, 0, 0, 0))
    newcol = pl.BlockSpec((depth, None, SWA_KV_WIDTH, nreq), lambda i: (0, i, 0, 0))
    return pl.pallas_call(
        functools.partial(_window_update_kernel, nreq=nreq),
        grid=(m // nreq,),
        in_specs=[cache, cache, newcol, newcol],
        out_specs=[cache, cache],
        out_shape=[jax.ShapeDtypeStruct(kt.shape, F32), jax.ShapeDtypeStruct(vt.shape, F32)],
        compiler_params=_params("arbitrary"),
        name="window_update_sample",
    )(kt, vt, knc, vnc)


def _memkv_kernel(x_ref, g_ref, w_ref, o_ref, xn_ref, *, l):
    @pl.when(pl.program_id(0) == 0)
    def _():
        xn_ref[...] = _rms(x_ref[...], g_ref[l:l + 1, :]).astype(BF)

    o_ref[...] = _dot(xn_ref[...], w_ref[...].astype(BF))


def _memkv(mem, g, w, l, tn):
    m = mem.shape[0]
    return pl.pallas_call(
        functools.partial(_memkv_kernel, l=l),
        grid=(2 * MEM_WIDTH // tn,),
        in_specs=[_resident((m, D_MODEL)), _resident(g.shape),
                  pl.BlockSpec((None, D_MODEL, tn), lambda j: (l, 0, j))],
        out_specs=pl.BlockSpec((m, tn), lambda j: (0, j)),
        out_shape=jax.ShapeDtypeStruct((m, 2 * MEM_WIDTH), F32),
        scratch_shapes=[pltpu.VMEM((m, D_MODEL), BF)],
        compiler_params=_params("arbitrary"),
        name="mem_kv",
    )(mem, g, w)


def _mem_sample_kernel(q_ref, mk_ref, mv_ref, o_ref, *, nreq):
    nrow = MEM_LEN * MEM_HEADS
    own = (lax.broadcasted_iota(jnp.int32, (nreq, MEM_HEADS, nrow), 2) % MEM_HEADS
           == lax.broadcasted_iota(jnp.int32, (nreq, MEM_HEADS, nrow), 1))
    s = jnp.einsum("rhd,rkd->rhk", q_ref[...].astype(BF), mk_ref[...].astype(BF),
                   preferred_element_type=F32) * (MEM_HD ** -0.5)
    s = jnp.where(own, s, -jnp.inf)
    e = jnp.exp(s - jnp.max(s, axis=-1, keepdims=True))
    p = (e / jnp.sum(e, axis=-1, keepdims=True)).astype(BF)
    o_ref[...] = jnp.einsum("rhk,rkd->rhd", p, mv_ref[...].astype(BF), preferred_element_type=F32)


def _mem_sample(q3, mk, mv, l, nreq):
    m = q3.shape[0]
    qspec = pl.BlockSpec((nreq, MEM_HEADS, MEM_HD), lambda i: (i, 0, 0))
    kvspec = pl.BlockSpec((None, nreq, MEM_LEN * MEM_HEADS, MEM_HD), lambda i: (l, i, 0, 0))
    return pl.pallas_call(
        functools.partial(_mem_sample_kernel, nreq=nreq),
        grid=(m // nreq,),
        in_specs=[qspec, kvspec, kvspec],
        out_specs=qspec,
        out_shape=jax.ShapeDtypeStruct((m, MEM_HEADS, MEM_HD), F32),
        compiler_params=_params("arbitrary"),
        name="mem_attn_sample",
    )(q3, mk, mv)


def _merge_kernel(xn_ref, og_ref, oa_ref, om_ref, xns_ref, ogs_ref, oas_ref, oms_ref,
                  wg0_ref, wg1_ref, wg2_ref, wbg_ref, wba_ref, wbm_ref, out_ref, outs_ref,
                  bg0_ref, bg1_ref, bg2_ref, bbg_ref, bba_ref, bbm_ref):
    pairs = [(wg0_ref, bg0_ref), (wg1_ref, bg1_ref), (wg2_ref, bg2_ref),
             (wbg_ref, bbg_ref), (wba_ref, bba_ref), (wbm_ref, bbm_ref)]

    @pl.when(_first_inner())
    def _():
        for w_ref, b_ref in pairs:
            b_ref[...] = w_ref[...].astype(BF)

    def merged(xn, og, oa, om):
        def branch(bg_ref, o, bb_ref):
            return jax.nn.sigmoid(_dot(xn, bg_ref[...])) * _dot(o, bb_ref[...])
        return (branch(bg0_ref, og, bbg_ref) + branch(bg1_ref, oa, bba_ref) + branch(bg2_ref, om, bbm_ref)).astype(BF)

    out_ref[...] = merged(xn_ref[...], og_ref[...], oa_ref[...], om_ref[...])

    @pl.when(_last_inner())
    def _():
        outs_ref[...] = merged(xns_ref[...], ogs_ref[...], oas_ref[...], oms_ref[...])


def _merge(xn, og, oa, om, xns, ogs, oas, oms, w_in, wbg, wba, wbm, l, tm, tn):
    m, ms = xn.shape[0], xns.shape[0]
    nt = D_MODEL // tn
    row = lambda n: pl.BlockSpec((tm, n), lambda j, i: (i, 0))
    gate = lambda b: pl.BlockSpec((None, D_MODEL, tn), lambda j, i: (l, 0, C_G // tn + b * nt + j))
    col = lambda n: pl.BlockSpec((None, n, tn), lambda j, i: (l, 0, j))
    widths = (G_WIDTH, SWA_WIDTH, MEM_WIDTH)
    return pl.pallas_call(
        _merge_kernel,
        grid=(nt, m // tm),
        in_specs=[row(D_MODEL)] + [row(n) for n in widths]
                 + [_resident((ms, D_MODEL))] + [_resident((ms, n)) for n in widths]
                 + [gate(0), gate(1), gate(2)] + [col(n) for n in widths],
        out_specs=[pl.BlockSpec((tm, tn), lambda j, i: (i, j)), pl.BlockSpec((ms, tn), lambda j, i: (0, j))],
        out_shape=[jax.ShapeDtypeStruct((m, D_MODEL), BF), jax.ShapeDtypeStruct((ms, D_MODEL), BF)],
        scratch_shapes=[pltpu.VMEM((D_MODEL, tn), BF)] * 3 + [pltpu.VMEM((n, tn), BF) for n in widths],
        compiler_params=_params("arbitrary", "arbitrary"),
        name="merge",
    )(xn, og, oa, om, xns, ogs, oas, oms, w_in, w_in, w_in, wbg, wba, wbm)


def _out_kernel(x_ref, mg_ref, xs_ref, mgs_ref, w_ref, g_ref, xo_ref, xn_ref, xos_ref, xns_ref, wbf_ref,
                *, l, tk, ncast):
    s = pl.program_id(0)

    @pl.when(s < ncast)
    def _():
        wbf_ref[pl.ds(pl.multiple_of(s * tk, tk), tk), :] = w_ref[...].astype(BF)

    def tile(x_ref, mg_ref, xo_ref, xn_ref):
        x = x_ref[...] + _dot(mg_ref[...], wbf_ref[...])
        xo_ref[...] = x
        xn_ref[...] = _rms(x, g_ref[l:l + 1, :]).astype(BF)

    @pl.when(s >= ncast)
    def _():
        tile(x_ref, mg_ref, xo_ref, xn_ref)

    @pl.when(s == pl.num_programs(0) - 1)
    def _():
        tile(xs_ref, mgs_ref, xos_ref, xns_ref)


def _out_proj(x, merged, xs, mergeds, w_out, g, l, tm, tk):
    m, ms = x.shape[0], xs.shape[0]
    ncast = D_MODEL // tk
    row = pl.BlockSpec((tm, D_MODEL), lambda s: (jnp.maximum(s - ncast, 0), 0))
    srow = pl.BlockSpec((ms, D_MODEL), lambda s: (0, 0))
    return pl.pallas_call(
        functools.partial(_out_kernel, l=l, tk=tk, ncast=ncast),
        grid=(ncast + m // tm,),
        in_specs=[row, row, _resident((ms, D_MODEL)), _resident((ms, D_MODEL)),
                  pl.BlockSpec((None, tk, D_MODEL), lambda s: (l, jnp.minimum(s, ncast - 1), 0)),
                  _resident(g.shape)],
        out_specs=[row, row, srow, srow],
        out_shape=[jax.ShapeDtypeStruct((m, D_MODEL), F32), jax.ShapeDtypeStruct((m, D_MODEL), BF),
                   jax.ShapeDtypeStruct((ms, D_MODEL), F32), jax.ShapeDtypeStruct((ms, D_MODEL), BF)],
        scratch_shapes=[pltpu.VMEM((D_MODEL, D_MODEL), BF)],
        compiler_params=_params("arbitrary"),
        name="out_proj",
    )(x, merged, xs, mergeds, w_out, g)


def _up_kernel(xn_ref, xns_ref, wa_ref, wb_ref, cwa_ref, cwb_ref, cba_ref, cbb_ref, sta_ref, stb_ref,
               g_ref, ca_ref, cb_ref, gs_ref, has_ref, hbs_ref, ha_ref, hb_ref, ba_ref, bb_ref,
               *, l, tm, tiles_per_seq):
    @pl.when(_first_inner())
    def _():
        ba_ref[...] = wa_ref[...].astype(BF)
        bb_ref[...] = wb_ref[...].astype(BF)

    @pl.when(pl.program_id(1) % tiles_per_seq == 0)
    def _():
        ha_ref[...] = jnp.zeros(ha_ref.shape, F32)
        hb_ref[...] = jnp.zeros(hb_ref.shape, F32)

    top = lax.broadcasted_iota(jnp.int32, (SUBLANES, ha_ref.shape[1]), 0)
    xn = xn_ref[...]

    def conv(h, prev_ref, cw_ref, bias_ref, tail_ref):
        prev = prev_ref[...]
        acc = bias_ref[l:l + 1, :]
        for j in range(CONV_W - 1):
            back = CONV_W - 1 - j
            rolled = pltpu.roll(h, back, axis=0)
            head = jnp.where(top < back, pltpu.roll(prev, back, axis=0), rolled[0:SUBLANES])
            acc = acc + jnp.concatenate([head, rolled[SUBLANES:]], axis=0) * cw_ref[j:j + 1, :]
        tail = h[tm - SUBLANES:]
        tail_ref[...] = tail
        prev_ref[...] = tail
        return acc + h * cw_ref[CONV_W - 1:CONV_W, :]

    a = conv(_dot(xn, ba_ref[...]), ha_ref, cwa_ref, cba_ref, ca_ref)
    b = conv(_dot(xn, bb_ref[...]), hb_ref, cwb_ref, cbb_ref, cb_ref)
    g_ref[...] = (jax.nn.gelu(a) * b).astype(BF)

    @pl.when(_last_inner())
    def _():
        xns = xns_ref[...]

        def conv_s(bw_ref, cw_ref, bias_ref, st_ref, h_out_ref):
            h = _dot(xns, bw_ref[...])
            h_out_ref[...] = h
            return (bias_ref[l:l + 1, :] + st_ref[:, 0, :] * cw_ref[0:1, :] + st_ref[:, 1, :] * cw_ref[1:2, :]
                    + h * cw_ref[2:3, :])

        a_s = conv_s(ba_ref, cwa_ref, cba_ref, sta_ref, has_ref)
        b_s = conv_s(bb_ref, cwb_ref, cbb_ref, stb_ref, hbs_ref)
        gs_ref[...] = (jax.nn.gelu(a_s) * b_s).astype(BF)


def _up_conv(xn, xns, w_up, conv_w, conv_b, state, l, batch, seq, tm, tn):
    m, ms = xn.shape[0], xns.shape[0]
    nt = D_FF // tn
    nrow = m // tm
    tiles_per_seq = seq // tm
    half = lambda off: (lambda j, i: (l, 0, off * nt + j))
    wspec = lambda off: pl.BlockSpec((None, D_MODEL, tn), half(off))
    cwspec = lambda off: pl.BlockSpec((None, CONV_W, tn), half(off))
    cbspec = lambda off: pl.BlockSpec((DEPTH, tn), lambda j, i: (0, off * nt + j))
    stspec = lambda off: pl.BlockSpec((None, ms, CONV_W - 1, tn), lambda j, i: (l, 0, 0, off * nt + j))
    tail = pl.BlockSpec((SUBLANES, tn), lambda j, i: (i // tiles_per_seq, j))
    scol = pl.BlockSpec((ms, tn), lambda j, i: (0, j))
    return pl.pallas_call(
        functools.partial(_up_kernel, l=l, tm=tm, tiles_per_seq=tiles_per_seq),
        grid=(nt, nrow),
        in_specs=[pl.BlockSpec((tm, D_MODEL), lambda j, i: (i, 0)), _resident((ms, D_MODEL)),
                  wspec(0), wspec(1), cwspec(0), cwspec(1), cbspec(0), cbspec(1), stspec(0), stspec(1)],
        out_specs=[pl.BlockSpec((tm, tn), lambda j, i: (i, j)), tail, tail, scol, scol, scol],
        out_shape=[jax.ShapeDtypeStruct((m, D_FF), BF),
                   jax.ShapeDtypeStruct((batch * SUBLANES, D_FF), F32),
                   jax.ShapeDtypeStruct((batch * SUBLANES, D_FF), F32),
                   jax.ShapeDtypeStruct((ms, D_FF), BF),
                   jax.ShapeDtypeStruct((ms, D_FF), F32), jax.ShapeDtypeStruct((ms, D_FF), F32)],
        scratch_shapes=[pltpu.VMEM((SUBLANES, tn), F32)] * 2 + [pltpu.VMEM((D_MODEL, tn), BF)] * 2,
        compiler_params=_params("arbitrary", "arbitrary"),
        name="up_conv_glu",
    )(xn, xns, w_up, w_up, conv_w, conv_w, conv_b, conv_b, state, state)


def _down_kernel(x_ref, g_ref, xs_ref, gs_ref, w_ref, gain_ref, *rest, tk, ncast, emit_x):
    wbf_ref = rest[-1]
    outs = rest[:-1]
    s = pl.program_id(0)

    @pl.when(s < ncast)
    def _():
        wbf_ref[pl.ds(pl.multiple_of(s * tk, tk), tk), :] = w_ref[...].astype(BF)

    def tile(x, g, o_refs):
        x = x + _dot(g, wbf_ref[...])
        if emit_x:
            o_refs[0][...] = x
        o_refs[-1][...] = _rms(x, gain_ref[...]).astype(o_refs[-1].dtype)

    nout = len(outs) // 2

    @pl.when(s >= ncast)
    def _():
        tile(x_ref[...], g_ref[...], outs[:nout])

    @pl.when(s == pl.num_programs(0) - 1)
    def _():
        tile(xs_ref[...], gs_ref[...], outs[nout:])


def _down_proj(x, g, xs, gs, w_down, gain, l, tm, tk, norm_dtype, emit_x):
    m, ms = x.shape[0], xs.shape[0]
    ncast = D_FF // tk
    rows = lambda n: pl.BlockSpec((tm, n), lambda s: (jnp.maximum(s - ncast, 0), 0))
    fixed = lambda n: pl.BlockSpec((ms, n), lambda s: (0, 0))
    dts = ([F32] if emit_x else []) + [norm_dtype]
    return pl.pallas_call(
        functools.partial(_down_kernel, tk=tk, ncast=ncast, emit_x=emit_x),
        grid=(ncast + m // tm,),
        in_specs=[rows(D_MODEL), rows(D_FF), _resident((ms, D_MODEL)), _resident((ms, D_FF)),
                  pl.BlockSpec((None, tk, D_MODEL), lambda s: (l, jnp.minimum(s, ncast - 1), 0)),
                  _resident((1, D_MODEL))],
        out_specs=[rows(D_MODEL) for _ in dts] + [fixed(D_MODEL) for _ in dts],
        out_shape=[jax.ShapeDtypeStruct((m, D_MODEL), dt) for dt in dts]
                  + [jax.ShapeDtypeStruct((ms, D_MODEL), dt) for dt in dts],
        scratch_shapes=[pltpu.VMEM((D_FF, D_MODEL), BF)],
        compiler_params=_params("arbitrary"),
        name="down_proj",
    )(x, g, xs, gs, w_down, gain)


def _conv_state_kernel(st_ref, *rest, nt):
    o_ref = rest[-1]
    first_half = pl.program_id(0) < nt
    for l in range(DEPTH):
        ha_ref, hb_ref = rest[2 * l], rest[2 * l + 1]
        o_ref[l, :, 0, :] = st_ref[l, :, CONV_W - 2, :]
        o_ref[l, :, 1, :] = jnp.where(first_half, ha_ref[...], hb_ref[...])


def _conv_state(state, halves, tn):
    depth, ms = state.shape[:2]
    nt = D_FF // tn
    blk = pl.BlockSpec((depth, ms, CONV_W - 1, tn), lambda j: (0, 0, 0, j))
    a_spec = pl.BlockSpec((ms, tn), lambda j: (0, jnp.minimum(j, nt - 1)))
    b_spec = pl.BlockSpec((ms, tn), lambda j: (0, jnp.maximum(j - nt, 0)))
    return pl.pallas_call(
        functools.partial(_conv_state_kernel, nt=nt),
        grid=(2 * nt,),
        in_specs=[blk] + [a_spec, b_spec] * depth,
        out_specs=blk,
        out_shape=jax.ShapeDtypeStruct(state.shape, F32),
        compiler_params=_params("arbitrary"),
        name="conv_state_sample",
    )(state, *halves)


def _layer(l, last, next_gain, xp, xs, xn, xns, mem, batch, seq, kt, vt, mk4, mv4, state_conv, slope_b, sink_b,
           norm_mix_g, w_in, gmlp_norm_g, gmlp_ws, gmlp_bs, attn_sinks, mem_norm_g, w_mem_kv,
           w_br_g, w_br_a, w_br_m, w_out, norm_ffn_g, w_up, conv_w, conv_b, w_down):
    dec = xs.shape[0]
    nreq = 8
    bs = gmlp_bs[l]
    bsb = jnp.repeat(bs.T, G_GDIM, axis=1)
    mkv = _memkv(mem, mem_norm_g, w_mem_kv, l, 256)
    front = _front(attn_sinks, xp if xn is None else xn, xs if xn is None else xns, norm_mix_g, w_in,
                   gmlp_ws, gmlp_norm_g, bsb, mkv, l, seq, 512)
    kv, oa, og, vg_last, om, zs = front[:6]
    if xn is None:
        xn, xns = front[6:]

    coef = jnp.stack([jnp.repeat(gmlp_ws[l, :, 0, 0], G_GDIM), jnp.repeat(bs[:, 0], G_GDIM)])
    ogs, vgs = _gmlp_sample(zs, gmlp_norm_g, coef, l)

    q3 = zs[:, C_Q:C_K].reshape(dec, SWA_HEADS, SWA_HD)
    newcol = lambda a, b: zs[:, a:b].reshape(dec // nreq, nreq, SWA_KV_WIDTH).transpose(0, 2, 1)
    oas = _swa_sample(q3, kt, vt, zs, slope_b, sink_b, l, nreq)

    oms = _mem_sample(zs[:, C_M:C_G].reshape(dec, MEM_HEADS, MEM_HD), mk4, mv4, l, nreq)

    merged, mergeds = _merge(xn, og, oa, om, xns, ogs, oas.reshape(dec, SWA_WIDTH).astype(BF),
                             oms.reshape(dec, MEM_WIDTH).astype(BF), w_in, w_br_g, w_br_a, w_br_m, l, 1024, 256)
    xp, xn2, xs, xn2s = _out_proj(xp, merged, xs, mergeds, w_out, norm_ffn_g, l, 512, 512)
    g, ca, cb, gs, has, hbs = _up_conv(xn2, xn2s, w_up, conv_w, conv_b, state_conv, l, batch, seq, 1024, 512)
    outs = _down_proj(xp, g, xs, gs, w_down, next_gain, l, 256, 512, F32 if last else BF, not last)

    tail = lambda t: t.reshape(batch, SUBLANES, D_FF)[:, SUBLANES - (CONV_W - 1):]
    kvb = kv.reshape(batch, seq, 2 * SWA_KV_WIDTH)[:, seq - WINDOW:]
    st = dict(
        pk=kvb[:, :, :SWA_KV_WIDTH].reshape(batch, WINDOW, SWA_KV, SWA_HD),
        pv=kvb[:, :, SWA_KV_WIDTH:].reshape(batch, WINDOW, SWA_KV, SWA_HD),
        mk=mkv[:, :MEM_WIDTH].reshape(batch, MEM_LEN, MEM_HEADS, MEM_HD),
        mv=mkv[:, MEM_WIDTH:].reshape(batch, MEM_LEN, MEM_HEADS, MEM_HD),
        gvp=vg_last.reshape(batch, CHUNK, G_WIDTH),
        gvs=vgs.reshape(dec, 1, G_WIDTH),
        cvp=jnp.concatenate([tail(ca), tail(cb)], axis=-1),
        cvs=(has, hbs),
        knc=newcol(C_K, C_VV),
        vnc=newcol(C_VV, C_M),
    )
    return outs, st


def kernel(x_prompt, x_sample, cache_swa_k, cache_swa_v, cache_mem_k, cache_mem_v, state_conv, mem_prompt, norm_mix_g, w_in, gmlp_norm_g, gmlp_ws, gmlp_bs, attn_sinks, mem_norm_g, w_mem_kv, w_br_g, w_br_a, w_br_m, w_out, norm_ffn_g, w_up, conv_w, conv_b, w_down, final_norm_g):
    batch, seq, _ = x_prompt.shape
    dec = x_sample.shape[0]
    assert x_sample.shape[1] == 1 and PAST_LEN % CHUNK == 0 and PAST_LEN >= WINDOW
    assert seq % 2048 == 0 and seq >= WINDOW and (seq - 1) // CHUNK * CHUNK == seq - CHUNK
    xp = x_prompt.reshape(batch * seq, D_MODEL)
    xs = x_sample.reshape(dec, D_MODEL)
    mem = mem_prompt.reshape(batch * MEM_LEN, D_MODEL)
    kt = cache_swa_k.transpose(0, 1, 3, 4, 2)
    vt = cache_swa_v.transpose(0, 1, 3, 4, 2)
    mk4 = cache_mem_k.reshape(DEPTH, dec, MEM_LEN * MEM_HEADS, MEM_HD)
    mv4 = cache_mem_v.reshape(DEPTH, dec, MEM_LEN * MEM_HEADS, MEM_HD)
    slope_b = jnp.broadcast_to(jnp.asarray(SLOPES, F32)[:, None], (SWA_HEADS, LANES))
    sink_b = jnp.broadcast_to(attn_sinks[:, :, None], (DEPTH, SWA_HEADS, LANES))
    sts = []
    xn = xns = None
    for l in range(DEPTH):
        last = l == DEPTH - 1
        next_gain = (final_norm_g if last else norm_mix_g[l + 1]).reshape(1, D_MODEL)
        outs, st = _layer(l, last, next_gain, xp, xs, xn, xns, mem, batch, seq, kt, vt, mk4, mv4, state_conv,
                          slope_b, sink_b,
                          norm_mix_g, w_in, gmlp_norm_g, gmlp_ws, gmlp_bs, attn_sinks, mem_norm_g,
                          w_mem_kv, w_br_g, w_br_a, w_br_m, w_out, norm_ffn_g, w_up, conv_w, conv_b, w_down)
        sts.append(st)
        if last:
            y_prompt, y_sample = outs
        else:
            xp, xn, xs, xns = outs
    stack = lambda key: jnp.stack([s[key] for s in sts])
    nk, nv = _window_update(kt, vt, stack("knc"), stack("vnc"), 8)
    return (y_prompt.reshape(batch, seq, D_MODEL), y_sample.reshape(dec, 1, D_MODEL), stack("pk"), stack("pv"),
            nk.transpose(0, 1, 4, 2, 3), nv.transpose(0, 1, 4, 2, 3),
            stack("mk"), stack("mv"), stack("gvp"), stack("gvs"), stack("cvp"),
            _conv_state(state_conv, [h for s in sts for h in s["cvs"]], D_FF // 2))
```

```python
import functools
import math

import jax
import jax.numpy as jnp
from jax import lax
from jax.experimental import pallas as pl
from jax.experimental.pallas import tpu as pltpu

D_MODEL = 2048
DEPTH = 2
PAST_LEN = 8192
MEM_LEN = 256
CHUNK = 128
G_GROUPS = 8
G_WIDTH = 768
G_GDIM = G_WIDTH // G_GROUPS
SWA_HEADS = 12
SWA_KV = 4
SWA_GROUP = SWA_HEADS // SWA_KV
SWA_HD = 64
SWA_WIDTH = SWA_HEADS * SWA_HD
SWA_KV_WIDTH = SWA_KV * SWA_HD
WINDOW = 128
MEM_HEADS = 4
MEM_HD = 128
MEM_WIDTH = MEM_HEADS * MEM_HD
N_BRANCH = 3
D_FF = 5632
CONV_W = 3
EPS = 1e-6

LANES = 128
SUBLANES = 8
VMEM_LIMIT = 56 * 1024 * 1024

C_U, C_V, C_Q, C_K, C_VV, C_M, C_G = 0, 768, 1536, 2304, 2560, 2816, 3328
ZT = 256
N_ACT = C_Q // ZT
assert all(c % ZT == 0 for c in (C_V, C_Q, C_K, C_VV, C_M, C_G))

BF = jnp.bfloat16
F32 = jnp.float32
NT_DIMS = (((1,), (1,)), ((), ()))


def _alibi_slopes(n):
    p = 2 ** int(math.floor(math.log2(n)))
    base = [2.0 ** (-8.0 * (i + 1) / p) for i in range(p)]
    extra = [2.0 ** (-8.0 * (2 * i + 1) / (2 * p)) for i in range(n - p)]
    return base + extra


SLOPES = _alibi_slopes(SWA_HEADS)


def _params(*sem):
    return pltpu.CompilerParams(dimension_semantics=sem, vmem_limit_bytes=VMEM_LIMIT)


def _resident(shape):
    return pl.BlockSpec(shape, lambda *_: (0,) * len(shape), pipeline_mode=pl.Buffered(1))


def _rms(x, g):
    ms = jnp.mean(x * x, axis=-1, keepdims=True)
    return x * lax.rsqrt(ms + EPS) * g


def _dot(a, b):
    return jnp.dot(a, b, preferred_element_type=F32)


def _first_inner():
    return pl.program_id(1) == 0


def _last_inner():
    return pl.program_id(1) == pl.num_programs(1) - 1


NZ = C_G // ZT


QKV_TILES = range(C_Q // ZT, C_M // ZT)
QKV_W = C_M - C_Q


def _swa_bias_table(bias_ref):
    r = lax.broadcasted_iota(jnp.int32, (WINDOW, 2 * WINDOW), 0)
    c = lax.broadcasted_iota(jnp.int32, (WINDOW, 2 * WINDOW), 1)
    dist = r + WINDOW - c
    valid = (dist >= 0) & (dist <= WINDOW)
    distf = dist.astype(F32)
    for h in range(SWA_HEADS):
        j, g = divmod(h, SWA_GROUP)
        pen = -SLOPES[h] * distf
        bias_ref[j, g * WINDOW:(g + 1) * WINDOW, :] = jnp.where(valid, pen, -jnp.inf)
        bias_ref[SWA_KV + j, g * WINDOW:(g + 1) * WINDOW, :] = jnp.where(valid & (c >= WINDOW), pen, -jnp.inf)


def _front_kernel(sink_ref, xn_ref, xs_ref, ng_ref, w_ref, ws_ref, gg_ref, bsb_ref, mkv_ref,
                  kv_ref, oa_ref, og_ref, vg_ref, om_ref, zs_ref, *rest, l, tm, tiles_per_seq, raw, group=4):
    xno_ref, xnso_ref = rest[:2] if raw else (None, None)
    wbf_ref, qkv_ref, zt_ref, carry_ref, bias_ref = rest[-5:]
    s = pl.program_id(0)

    def normed(x_ref, out_ref):
        if not raw:
            return x_ref[...]
        x = _rms(x_ref[...], ng_ref[l:l + 1, :]).astype(BF)
        out_ref[...] = x
        return x

    rows = SWA_GROUP * WINDOW
    kcol, vcol = C_K - C_Q, C_VV - C_Q
    mcol = C_Q

    @pl.when(s < NZ)
    def _():
        wbf_ref[s] = w_ref[...].astype(BF)

    @pl.when(s == NZ)
    def _():
        _swa_bias_table(bias_ref)
        carry_ref[...] = jnp.zeros(carry_ref.shape, F32)

    def column(x, c):
        h = _dot(x, wbf_ref[c])
        return jax.nn.gelu(h) if c < N_ACT else h

    @pl.when(s >= NZ)
    def _():
        x = normed(xn_ref, xno_ref)
        for c in QKV_TILES:
            qkv_ref[:, c * ZT - C_Q:(c + 1) * ZT - C_Q] = column(x, c)
        kv_ref[...] = qkv_ref[:, kcol:]

        def project(c, col):
            def task():
                zt_ref[:, col:col + ZT] = column(x, c)
            return task

        tasks = [project(c, c * ZT) for c in range(N_ACT)]
        tasks += [project(c, mcol + (c - C_M // ZT) * ZT) for c in range(C_M // ZT, NZ)]

        r = lax.broadcasted_iota(jnp.int32, (CHUNK, CHUNK), 0)
        cc = lax.broadcasted_iota(jnp.int32, (CHUNK, CHUNK), 1)
        lane = lax.broadcasted_iota(jnp.int32, (CHUNK, LANES), 1)

        def gmlp_chunk(n):
            def task():
                wtri = [jnp.where(r >= cc, ws_ref[g], 0.0).astype(BF) for g in range(G_GROUPS)]
                crow = slice(n * CHUNK, (n + 1) * CHUNK)
                vg = _rms(zt_ref[crow, C_V:C_Q], gg_ref[l:l + 1, :])
                if n == tm // CHUNK - 1:
                    vg_ref[...] = vg
                for t in range(G_TILES):
                    cols = slice(t * LANES, (t + 1) * LANES)
                    wst = jnp.concatenate([wtri[GA[t]], wtri[GA[t] + 1]], axis=0)
                    both = _dot(wst, vg[:, cols].astype(BF))
                    mix = jnp.where(lane < BND[t], both[:CHUNK], both[CHUNK:]) + bsb_ref[:, cols]
                    og_ref[crow, cols] = (zt_ref[crow, cols] * mix).astype(BF)
            return task

        tasks += [gmlp_chunk(n) for n in range(tm // CHUNK)]

        def mem_head(h):
            def task():
                cols = slice(h * MEM_HD, (h + 1) * MEM_HD)
                q = zt_ref[:, mcol + h * MEM_HD:mcol + (h + 1) * MEM_HD].astype(BF)
                sc = lax.dot_general(q, mkv_ref[:, cols].astype(BF), NT_DIMS,
                                     preferred_element_type=F32) * (MEM_HD ** -0.5)
                e = jnp.exp(sc - jnp.max(sc, axis=-1, keepdims=True))
                p = (e / jnp.sum(e, axis=-1, keepdims=True)).astype(BF)
                vcols = slice(MEM_WIDTH + h * MEM_HD, MEM_WIDTH + (h + 1) * MEM_HD)
                om_ref[:, cols] = _dot(p, mkv_ref[:, vcols].astype(BF)).astype(BF)
            return task

        tasks += [mem_head(h) for h in range(MEM_HEADS)]

        units = [(b, j) for b in range(tm // WINDOW) for j in range(SWA_KV)]
        groups = [units[k:k + group] for k in range(0, len(units), group)]
        per = len(tasks) // len(groups)
        assert per * len(groups) == len(tasks) and per % 2 == 0
        head = lax.broadcasted_iota(jnp.int32, (rows, 1), 0) // WINDOW
        fresh = ((s - NZ) % tiles_per_seq == 0).astype(jnp.int32) * SWA_KV

        def scores(b, j):
            own = slice(b * WINDOW, (b + 1) * WINDOW)
            kvh = lambda base: slice(base + j * SWA_HD, base + (j + 1) * SWA_HD)
            if b == 0:
                kprev, vprev = carry_ref[:, kvh(0)], carry_ref[:, kvh(SWA_KV_WIDTH)]
            else:
                prev = slice((b - 1) * WINDOW, b * WINDOW)
                kprev, vprev = qkv_ref[prev, kvh(kcol)], qkv_ref[prev, kvh(vcol)]
            kj = jnp.concatenate([kprev, qkv_ref[own, kvh(kcol)]], axis=0).astype(BF)
            vj = jnp.concatenate([vprev, qkv_ref[own, kvh(vcol)]], axis=0).astype(BF)
            hs = range(j * SWA_GROUP, (j + 1) * SWA_GROUP)
            q = jnp.concatenate([qkv_ref[own, h * SWA_HD:(h + 1) * SWA_HD] for h in hs], axis=0).astype(BF)
            sink = jnp.full((rows, 1), sink_ref[l, hs[-1]], F32)
            for g in range(SWA_GROUP - 1):
                sink = jnp.where(head == g, sink_ref[l, hs[g]], sink)
            bias = bias_ref[(fresh if b == 0 else 0) + j]
            sc = lax.dot_general(q, kj, NT_DIMS, preferred_element_type=F32) * (SWA_HD ** -0.5) + bias
            return sc, sink, vj

        def softmax(sc, sink, vj):
            mx = jnp.maximum(jnp.max(sc, axis=-1, keepdims=True), sink)
            p = jnp.exp(sc - mx)
            den = jnp.sum(p, axis=-1, keepdims=True) + jnp.exp(sink - mx)
            return p.astype(BF), den, vj

        def values(b, j, p, den, vj):
            o = _dot(p, vj) / den
            own = slice(b * WINDOW, (b + 1) * WINDOW)
            for g, h in enumerate(range(j * SWA_GROUP, (j + 1) * SWA_GROUP)):
                oa_ref[own, h * SWA_HD:(h + 1) * SWA_HD] = o[g * WINDOW:(g + 1) * WINDOW].astype(BF)

        for k, grp in enumerate(groups):
            for task in tasks[per * k:per * k + per // 2]:
                task()
            stage = [scores(b, j) for b, j in grp]
            stage = [softmax(*args) for args in stage]
            for task in tasks[per * k + per // 2:per * (k + 1)]:
                task()
            for (b, j), args in zip(grp, stage):
                values(b, j, *args)
        last = slice(tm - WINDOW, tm)
        carry_ref[...] = qkv_ref[last, kcol:]

    @pl.when(s == pl.num_programs(0) - 1)
    def _():
        xs = normed(xs_ref, xnso_ref)
        for c in range(NZ):
            zs_ref[:, c * ZT:(c + 1) * ZT] = column(xs, c)


def _front(sinks, xn, xs, ng, w_in, ws, gg, bsb, mkv, l, seq, tm):
    m, ms = xn.shape[0], xs.shape[0]
    raw = xn.dtype == F32
    tiles_per_seq = seq // tm
    tile = lambda s: jnp.maximum(s - NZ, 0)
    rows = lambda n: pl.BlockSpec((tm, n), lambda s: (tile(s), 0))
    fixed = lambda n: pl.BlockSpec((ms, n), lambda s: (0, 0))
    per_seq = lambda r, n: pl.BlockSpec((r, n), lambda s: (tile(s) // tiles_per_seq, 0))
    outs = [(2 * SWA_KV_WIDTH, F32), (SWA_WIDTH, BF), (G_WIDTH, BF)]
    return pl.pallas_call(
        functools.partial(_front_kernel, l=l, tm=tm, tiles_per_seq=tiles_per_seq, raw=raw),
        grid=(NZ + m // tm,),
        in_specs=[pl.BlockSpec(memory_space=pltpu.SMEM), rows(D_MODEL), _resident((ms, D_MODEL)), _resident(ng.shape),
                  pl.BlockSpec((None, D_MODEL, ZT), lambda s: (l, 0, jnp.minimum(s, NZ - 1))),
                  pl.BlockSpec((None, G_GROUPS, CHUNK, CHUNK), lambda s: (l, 0, 0, 0)),
                  _resident(gg.shape), _resident((CHUNK, G_WIDTH)), per_seq(MEM_LEN, 2 * MEM_WIDTH)],
        out_specs=[rows(n) for n, _ in outs] + [per_seq(CHUNK, G_WIDTH), rows(MEM_WIDTH), fixed(C_G)]
                  + ([rows(D_MODEL), fixed(D_MODEL)] if raw else []),
        out_shape=[jax.ShapeDtypeStruct((m, n), dt) for n, dt in outs]
                  + [jax.ShapeDtypeStruct((m // seq * CHUNK, G_WIDTH), F32), jax.ShapeDtypeStruct((m, MEM_WIDTH), BF),
                     jax.ShapeDtypeStruct((ms, C_G), F32)]
                  + ([jax.ShapeDtypeStruct((m, D_MODEL), BF), jax.ShapeDtypeStruct((ms, D_MODEL), BF)] if raw else []),
        scratch_shapes=[pltpu.VMEM((NZ, D_MODEL, ZT), BF), pltpu.VMEM((tm, QKV_W), F32),
                        pltpu.VMEM((tm, C_Q + MEM_WIDTH), F32), pltpu.VMEM((WINDOW, 2 * SWA_KV_WIDTH), F32),
                        pltpu.VMEM((2 * SWA_KV, SWA_GROUP * WINDOW, 2 * WINDOW), F32)],
        compiler_params=_params("arbitrary"),
        name="front",
    )(sinks, xn, xs, ng, w_in, ws, gg, bsb, mkv)


G_TILES = G_WIDTH // LANES
GA = [(t * LANES) // G_GDIM for t in range(G_TILES)]
BND = [(GA[t] + 1) * G_GDIM - t * LANES for t in range(G_TILES)]
assert all(0 < b < LANES and (GA[t] + 2) * G_GDIM >= (t + 1) * LANES for t, b in enumerate(BND))


def _gmlp_sample_kernel(zs_ref, gg_ref, coef_ref, og_ref, vg_ref, *, l):
    vg = _rms(zs_ref[:, C_V:C_Q], gg_ref[l:l + 1, :])
    vg_ref[...] = vg
    og_ref[...] = (zs_ref[:, C_U:C_V] * (coef_ref[0:1, :] * vg + coef_ref[1:2, :])).astype(BF)


def _gmlp_sample(zs, gg, coef, l):
    m = zs.shape[0]
    return pl.pallas_call(
        functools.partial(_gmlp_sample_kernel, l=l),
        out_shape=[jax.ShapeDtypeStruct((m, G_WIDTH), BF), jax.ShapeDtypeStruct((m, G_WIDTH), F32)],
        name="gmlp_sample",
    )(zs, gg, coef)


def _swa_sample_kernel(q_ref, kt_ref, vt_ref, kn_ref, vn_ref, slope_ref, sink_ref, o_ref, *, nreq):
    kvw = SWA_KV_WIDTH
    hrow = lax.broadcasted_iota(jnp.int32, (SWA_HEADS, kvw), 0) // SWA_GROUP
    lblk = lax.broadcasted_iota(jnp.int32, (SWA_HEADS, kvw), 1) // SWA_HD
    own = hrow == lblk
    kvh = lax.broadcasted_iota(jnp.int32, (SWA_HEADS, SWA_HD), 0) // SWA_GROUP
    c = lax.broadcasted_iota(jnp.int32, (SWA_HEADS, WINDOW), 1)
    bias = slope_ref[...] * (WINDOW - c).astype(F32)
    sink = sink_ref[:, 0:1]
    scale = SWA_HD ** -0.5
    q = q_ref[...]
    qm = jnp.where(own, jnp.concatenate([q] * SWA_KV, axis=2), 0.0).astype(BF)
    kt = kt_ref[...].reshape(nreq, kvw, WINDOW)
    vt = vt_ref[...].reshape(nreq, kvw, WINDOW)
    rows = lambda ref: jnp.stack([ref[r:r + 1, :] for r in range(nreq)]).astype(BF).astype(F32)
    kn, vn = rows(kn_ref), rows(vn_ref)
    s_c = jnp.einsum("rhk,rkc->rhc", qm, kt.astype(BF), preferred_element_type=F32) * scale - bias
    s_n = jnp.sum(qm.astype(F32) * kn, axis=-1, keepdims=True) * scale
    mx = jnp.maximum(jnp.maximum(jnp.max(s_c, axis=-1, keepdims=True), s_n), sink)
    p_c = jnp.exp(s_c - mx)
    p_n = jnp.exp(s_n - mx)
    den = jnp.sum(p_c, axis=-1, keepdims=True) + p_n + jnp.exp(sink - mx)
    p_c = (p_c / den).astype(BF)
    p_n = (p_n / den).astype(BF).astype(F32)
    o_all = jnp.einsum("rhc,rkc->rhk", p_c, vt.astype(BF), preferred_element_type=F32) + p_n * vn
    o = jnp.zeros((nreq, SWA_HEADS, SWA_HD), F32)
    for j in range(SWA_KV):
        o = jnp.where(kvh == j, o_all[:, :, j * SWA_HD:(j + 1) * SWA_HD], o)
    o_ref[...] = o


def _swa_sample(q3, kt, vt, zs, slope_b, sink_b, l, nreq):
    m = q3.shape[0]
    cache = pl.BlockSpec((None, nreq, SWA_KV, SWA_HD, WINDOW), lambda i: (l, i, 0, 0, 0))
    return pl.pallas_call(
        functools.partial(_swa_sample_kernel, nreq=nreq),
        grid=(m // nreq,),
        in_specs=[pl.BlockSpec((nreq, SWA_HEADS, SWA_HD), lambda i: (i, 0, 0)), cache, cache,
                  pl.BlockSpec((nreq, SWA_KV_WIDTH), lambda i: (i, C_K // SWA_KV_WIDTH)),
                  pl.BlockSpec((nreq, SWA_KV_WIDTH), lambda i: (i, C_VV // SWA_KV_WIDTH)),
                  _resident((SWA_HEADS, LANES)), pl.BlockSpec((None, SWA_HEADS, LANES), lambda i: (l, 0, 0))],
        out_specs=pl.BlockSpec((nreq, SWA_HEADS, SWA_HD), lambda i: (i, 0, 0)),
        out_shape=jax.ShapeDtypeStruct((m, SWA_HEADS, SWA_HD), F32),
        compiler_params=_params("arbitrary"),
        name="swa_sample",
    )(q3, kt, vt, zs, zs, slope_b, sink_b)


def _window_update_kernel(kt_ref, vt_ref, knc_ref, vnc_ref, nk_ref, nv_ref, *, nreq):
    last = lax.broadcasted_iota(jnp.int32, (SWA_KV_WIDTH, WINDOW), 1) == WINDOW - 1
    for src_ref, col_ref, dst_ref in ((kt_ref, knc_ref, nk_ref), (vt_ref, vnc_ref, nv_ref)):
        for l in range(DEPTH):
            for r in range(nreq):
                old = src_ref[l, r].reshape(SWA_KV_WIDTH, WINDOW)
                new = jnp.where(last, col_ref[l, :, r:r + 1], pltpu.roll(old, WINDOW - 1, axis=1))
                dst_ref[l, r] = new.reshape(SWA_KV, SWA_HD, WINDOW)


def _window_update(kt, vt, knc, vnc, nreq):
    depth, m = kt.shape[:2]
    cache = pl.BlockSpec((depth, nreq, SWA_KV, SWA_HD, WINDOW), lambda i: (0, i, 0, 0, 0))
    newcol = pl.BlockSpec((depth, None, SWA_KV_WIDTH, nreq), lambda i: (0, i, 0, 0))
    return pl.pallas_call(
        functools.partial(_window_update_kernel, nreq=nreq),
        grid=(m // nreq,),
        in_specs=[cache, cache, newcol, newcol],
        out_specs=[cache, cache],
        out_shape=[jax.ShapeDtypeStruct(kt.shape, F32), jax.ShapeDtypeStruct(vt.shape, F32)],
        compiler_params=_params("arbitrary"),
        name="window_update_sample",
    )(kt, vt, knc, vnc)


def _memkv_kernel(x_ref, g_ref, w_ref, o_ref, xn_ref, *, l):
    @pl.when(pl.program_id(0) == 0)
    def _():
        xn_ref[...] = _rms(x_ref[...], g_ref[l:l + 1, :]).astype(BF)

    o_ref[...] = _dot(xn_ref[...], w_ref[...].astype(BF))


def _memkv(mem, g, w, l, tn):
    m = mem.shape[0]
    return pl.pallas_call(
        functools.partial(_memkv_kernel, l=l),
        grid=(2 * MEM_WIDTH // tn,),
        in_specs=[_resident((m, D_MODEL)), _resident(g.shape),
                  pl.BlockSpec((None, D_MODEL, tn), lambda j: (l, 0, j))],
        out_specs=pl.BlockSpec((m, tn), lambda j: (0, j)),
        out_shape=jax.ShapeDtypeStruct((m, 2 * MEM_WIDTH), F32),
        scratch_shapes=[pltpu.VMEM((m, D_MODEL), BF)],
        compiler_params=_params("arbitrary"),
        name="mem_kv",
    )(mem, g, w)


def _mem_sample_kernel(q_ref, mk_ref, mv_ref, o_ref, *, nreq):
    nrow = MEM_LEN * MEM_HEADS
    own = (lax.broadcasted_iota(jnp.int32, (nreq, MEM_HEADS, nrow), 2) % MEM_HEADS
           == lax.broadcasted_iota(jnp.int32, (nreq, MEM_HEADS, nrow), 1))
    s = jnp.einsum("rhd,rkd->rhk", q_ref[...].astype(BF), mk_ref[...].astype(BF),
                   preferred_element_type=F32) * (MEM_HD ** -0.5)
    s = jnp.where(own, s, -jnp.inf)
    e = jnp.exp(s - jnp.max(s, axis=-1, keepdims=True))
    p = (e / jnp.sum(e, axis=-1, keepdims=True)).astype(BF)
    o_ref[...] = jnp.einsum("rhk,rkd->rhd", p, mv_ref[...].astype(BF), preferred_element_type=F32)


def _mem_sample(q3, mk, mv, l, nreq):
    m = q3.shape[0]
    qspec = pl.BlockSpec((nreq, MEM_HEADS, MEM_HD), lambda i: (i, 0, 0))
    kvspec = pl.BlockSpec((None, nreq, MEM_LEN * MEM_HEADS, MEM_HD), lambda i: (l, i, 0, 0))
    return pl.pallas_call(
        functools.partial(_mem_sample_kernel, nreq=nreq),
        grid=(m // nreq,),
        in_specs=[qspec, kvspec, kvspec],
        out_specs=qspec,
        out_shape=jax.ShapeDtypeStruct((m, MEM_HEADS, MEM_HD), F32),
        compiler_params=_params("arbitrary"),
        name="mem_attn_sample",
    )(q3, mk, mv)


def _merge_kernel(xn_ref, og_ref, oa_ref, om_ref, xns_ref, ogs_ref, oas_ref, oms_ref,
                  wg0_ref, wg1_ref, wg2_ref, wbg_ref, wba_ref, wbm_ref, out_ref, outs_ref,
                  bg0_ref, bg1_ref, bg2_ref, bbg_ref, bba_ref, bbm_ref):
    pairs = [(wg0_ref, bg0_ref), (wg1_ref, bg1_ref), (wg2_ref, bg2_ref),
             (wbg_ref, bbg_ref), (wba_ref, bba_ref), (wbm_ref, bbm_ref)]

    @pl.when(_first_inner())
    def _():
        for w_ref, b_ref in pairs:
            b_ref[...] = w_ref[...].astype(BF)

    def merged(xn, og, oa, om):
        def branch(bg_ref, o, bb_ref):
            return jax.nn.sigmoid(_dot(xn, bg_ref[...])) * _dot(o, bb_ref[...])
        return (branch(bg0_ref, og, bbg_ref) + branch(bg1_ref, oa, bba_ref) + branch(bg2_ref, om, bbm_ref)).astype(BF)

    out_ref[...] = merged(xn_ref[...], og_ref[...], oa_ref[...], om_ref[...])

    @pl.when(_last_inner())
    def _():
        outs_ref[...] = merged(xns_ref[...], ogs_ref[...], oas_ref[...], oms_ref[...])


def _merge(xn, og, oa, om, xns, ogs, oas, oms, w_in, wbg, wba, wbm, l, tm, tn):
    m, ms = xn.shape[0], xns.shape[0]
    nt = D_MODEL // tn
    row = lambda n: pl.BlockSpec((tm, n), lambda j, i: (i, 0))
    gate = lambda b: pl.BlockSpec((None, D_MODEL, tn), lambda j, i: (l, 0, C_G // tn + b * nt + j))
    col = lambda n: pl.BlockSpec((None, n, tn), lambda j, i: (l, 0, j))
    widths = (G_WIDTH, SWA_WIDTH, MEM_WIDTH)
    return pl.pallas_call(
        _merge_kernel,
        grid=(nt, m // tm),
        in_specs=[row(D_MODEL)] + [row(n) for n in widths]
                 + [_resident((ms, D_MODEL))] + [_resident((ms, n)) for n in widths]
                 + [gate(0), gate(1), gate(2)] + [col(n) for n in widths],
        out_specs=[pl.BlockSpec((tm, tn), lambda j, i: (i, j)), pl.BlockSpec((ms, tn), lambda j, i: (0, j))],
        out_shape=[jax.ShapeDtypeStruct((m, D_MODEL), BF), jax.ShapeDtypeStruct((ms, D_MODEL), BF)],
        scratch_shapes=[pltpu.VMEM((D_MODEL, tn), BF)] * 3 + [pltpu.VMEM((n, tn), BF) for n in widths],
        compiler_params=_params("arbitrary", "arbitrary"),
        name="merge",
    )(xn, og, oa, om, xns, ogs, oas, oms, w_in, w_in, w_in, wbg, wba, wbm)


def _out_kernel(x_ref, mg_ref, xs_ref, mgs_ref, w_ref, g_ref, xo_ref, xn_ref, xos_ref, xns_ref, wbf_ref,
                *, l, tk, ncast):
    s = pl.program_id(0)

    @pl.when(s < ncast)
    def _():
        wbf_ref[pl.ds(pl.multiple_of(s * tk, tk), tk), :] = w_ref[...].astype(BF)

    def tile(x_ref, mg_ref, xo_ref, xn_ref):
        x = x_ref[...] + _dot(mg_ref[...], wbf_ref[...])
        xo_ref[...] = x
        xn_ref[...] = _rms(x, g_ref[l:l + 1, :]).astype(BF)

    @pl.when(s >= ncast)
    def _():
        tile(x_ref, mg_ref, xo_ref, xn_ref)

    @pl.when(s == pl.num_programs(0) - 1)
    def _():
        tile(xs_ref, mgs_ref, xos_ref, xns_ref)


def _out_proj(x, merged, xs, mergeds, w_out, g, l, tm, tk):
    m, ms = x.shape[0], xs.shape[0]
    ncast = D_MODEL // tk
    row = pl.BlockSpec((tm, D_MODEL), lambda s: (jnp.maximum(s - ncast, 0), 0))
    srow = pl.BlockSpec((ms, D_MODEL), lambda s: (0, 0))
    return pl.pallas_call(
        functools.partial(_out_kernel, l=l, tk=tk, ncast=ncast),
        grid=(ncast + m // tm,),
        in_specs=[row, row, _resident((ms, D_MODEL)), _resident((ms, D_MODEL)),
                  pl.BlockSpec((None, tk, D_MODEL), lambda s: (l, jnp.minimum(s, ncast - 1), 0)),
                  _resident(g.shape)],
        out_specs=[row, row, srow, srow],
        out_shape=[jax.ShapeDtypeStruct((m, D_MODEL), F32), jax.ShapeDtypeStruct((m, D_MODEL), BF),
                   jax.ShapeDtypeStruct((ms, D_MODEL), F32), jax.ShapeDtypeStruct((ms, D_MODEL), BF)],
        scratch_shapes=[pltpu.VMEM((D_MODEL, D_MODEL), BF)],
        compiler_params=_params("arbitrary"),
        name="out_proj",
    )(x, merged, xs, mergeds, w_out, g)


def _up_kernel(xn_ref, xns_ref, wa_ref, wb_ref, cwa_ref, cwb_ref, cba_ref, cbb_ref, sta_ref, stb_ref,
               g_ref, ca_ref, cb_ref, gs_ref, has_ref, hbs_ref, ha_ref, hb_ref, ba_ref, bb_ref,
               *, l, tm, tiles_per_seq):
    @pl.when(_first_inner())
    def _():
        ba_ref[...] = wa_ref[...].astype(BF)
        bb_ref[...] = wb_ref[...].astype(BF)

    @pl.when(pl.program_id(1) % tiles_per_seq == 0)
    def _():
        ha_ref[...] = jnp.zeros(ha_ref.shape, F32)
        hb_ref[...] = jnp.zeros(hb_ref.shape, F32)

    top = lax.broadcasted_iota(jnp.int32, (SUBLANES, ha_ref.shape[1]), 0)
    xn = xn_ref[...]

    def conv(h, prev_ref, cw_ref, bias_ref, tail_ref):
        prev = prev_ref[...]
        acc = bias_ref[l:l + 1, :]
        for j in range(CONV_W - 1):
            back = CONV_W - 1 - j
            rolled = pltpu.roll(h, back, axis=0)
            head = jnp.where(top < back, pltpu.roll(prev, back, axis=0), rolled[0:SUBLANES])
            acc = acc + jnp.concatenate([head, rolled[SUBLANES:]], axis=0) * cw_ref[j:j + 1, :]
        tail = h[tm - SUBLANES:]
        tail_ref[...] = tail
        prev_ref[...] = tail
        return acc + h * cw_ref[CONV_W - 1:CONV_W, :]

    a = conv(_dot(xn, ba_ref[...]), ha_ref, cwa_ref, cba_ref, ca_ref)
    b = conv(_dot(xn, bb_ref[...]), hb_ref, cwb_ref, cbb_ref, cb_ref)
    g_ref[...] = (jax.nn.gelu(a) * b).astype(BF)

    @pl.when(_last_inner())
    def _():
        xns = xns_ref[...]

        def conv_s(bw_ref, cw_ref, bias_ref, st_ref, h_out_ref):
            h = _dot(xns, bw_ref[...])
            h_out_ref[...] = h
            return (bias_ref[l:l + 1, :] + st_ref[:, 0, :] * cw_ref[0:1, :] + st_ref[:, 1, :] * cw_ref[1:2, :]
                    + h * cw_ref[2:3, :])

        a_s = conv_s(ba_ref, cwa_ref, cba_ref, sta_ref, has_ref)
        b_s = conv_s(bb_ref, cwb_ref, cbb_ref, stb_ref, hbs_ref)
        gs_ref[...] = (jax.nn.gelu(a_s) * b_s).astype(BF)


def _up_conv(xn, xns, w_up, conv_w, conv_b, state, l, batch, seq, tm, tn):
    m, ms = xn.shape[0], xns.shape[0]
    nt = D_FF // tn
    nrow = m // tm
    tiles_per_seq = seq // tm
    half = lambda off: (lambda j, i: (l, 0, off * nt + j))
    wspec = lambda off: pl.BlockSpec((None, D_MODEL, tn), half(off))
    cwspec = lambda off: pl.BlockSpec((None, CONV_W, tn), half(off))
    cbspec = lambda off: pl.BlockSpec((DEPTH, tn), lambda j, i: (0, off * nt + j))
    stspec = lambda off: pl.BlockSpec((None, ms, CONV_W - 1, tn), lambda j, i: (l, 0, 0, off * nt + j))
    tail = pl.BlockSpec((SUBLANES, tn), lambda j, i: (i // tiles_per_seq, j))
    scol = pl.BlockSpec((ms, tn), lambda j, i: (0, j))
    return pl.pallas_call(
        functools.partial(_up_kernel, l=l, tm=tm, tiles_per_seq=tiles_per_seq),
        grid=(nt, nrow),
        in_specs=[pl.BlockSpec((tm, D_MODEL), lambda j, i: (i, 0)), _resident((ms, D_MODEL)),
                  wspec(0), wspec(1), cwspec(0), cwspec(1), cbspec(0), cbspec(1), stspec(0), stspec(1)],
        out_specs=[pl.BlockSpec((tm, tn), lambda j, i: (i, j)), tail, tail, scol, scol, scol],
        out_shape=[jax.ShapeDtypeStruct((m, D_FF), BF),
                   jax.ShapeDtypeStruct((batch * SUBLANES, D_FF), F32),
                   jax.ShapeDtypeStruct((batch * SUBLANES, D_FF), F32),
                   jax.ShapeDtypeStruct((ms, D_FF), BF),
                   jax.ShapeDtypeStruct((ms, D_FF), F32), jax.ShapeDtypeStruct((ms, D_FF), F32)],
        scratch_shapes=[pltpu.VMEM((SUBLANES, tn), F32)] * 2 + [pltpu.VMEM((D_MODEL, tn), BF)] * 2,
        compiler_params=_params("arbitrary", "arbitrary"),
        name="up_conv_glu",
    )(xn, xns, w_up, w_up, conv_w, conv_w, conv_b, conv_b, state, state)


def _down_kernel(x_ref, g_ref, xs_ref, gs_ref, w_ref, gain_ref, *rest, tk, ncast, emit_x):
    wbf_ref = rest[-1]
    outs = rest[:-1]
    s = pl.program_id(0)

    @pl.when(s < ncast)
    def _():
        wbf_ref[pl.ds(pl.multiple_of(s * tk, tk), tk), :] = w_ref[...].astype(BF)

    def tile(x, g, o_refs):
        x = x + _dot(g, wbf_ref[...])
        if emit_x:
            o_refs[0][...] = x
        o_refs[-1][...] = _rms(x, gain_ref[...]).astype(o_refs[-1].dtype)

    nout = len(outs) // 2

    @pl.when(s >= ncast)
    def _():
        tile(x_ref[...], g_ref[...], outs[:nout])

    @pl.when(s == pl.num_programs(0) - 1)
    def _():
        tile(xs_ref[...], gs_ref[...], outs[nout:])


def _down_proj(x, g, xs, gs, w_down, gain, l, tm, tk, norm_dtype, emit_x):
    m, ms = x.shape[0], xs.shape[0]
    ncast = D_FF // tk
    rows = lambda n: pl.BlockSpec((tm, n), lambda s: (jnp.maximum(s - ncast, 0), 0))
    fixed = lambda n: pl.BlockSpec((ms, n), lambda s: (0, 0))
    dts = ([F32] if emit_x else []) + [norm_dtype]
    return pl.pallas_call(
        functools.partial(_down_kernel, tk=tk, ncast=ncast, emit_x=emit_x),
        grid=(ncast + m // tm,),
        in_specs=[rows(D_MODEL), rows(D_FF), _resident((ms, D_MODEL)), _resident((ms, D_FF)),
                  pl.BlockSpec((None, tk, D_MODEL), lambda s: (l, jnp.minimum(s, ncast - 1), 0)),
                  _resident((1, D_MODEL))],
        out_specs=[rows(D_MODEL) for _ in dts] + [fixed(D_MODEL) for _ in dts],
        out_shape=[jax.ShapeDtypeStruct((m, D_MODEL), dt) for dt in dts]
                  + [jax.ShapeDtypeStruct((ms, D_MODEL), dt) for dt in dts],
        scratch_shapes=[pltpu.VMEM((D_FF, D_MODEL), BF)],
        compiler_params=_params("arbitrary"),
        name="down_proj",
    )(x, g, xs, gs, w_down, gain)


def _conv_state_kernel(st_ref, *rest, nt):
    o_ref = rest[-1]
    first_half = pl.program_id(0) < nt
    for l in range(DEPTH):
        ha_ref, hb_ref = rest[2 * l], rest[2 * l + 1]
        o_ref[l, :, 0, :] = st_ref[l, :, CONV_W - 2, :]
        o_ref[l, :, 1, :] = jnp.where(first_half, ha_ref[...], hb_ref[...])


def _conv_state(state, halves, tn):
    depth, ms = state.shape[:2]
    nt = D_FF // tn
    blk = pl.BlockSpec((depth, ms, CONV_W - 1, tn), lambda j: (0, 0, 0, j))
    a_spec = pl.BlockSpec((ms, tn), lambda j: (0, jnp.minimum(j, nt - 1)))
    b_spec = pl.BlockSpec((ms, tn), lambda j: (0, jnp.maximum(j - nt, 0)))
    return pl.pallas_call(
        functools.partial(_conv_state_kernel, nt=nt),
        grid=(2 * nt,),
        in_specs=[blk] + [a_spec, b_spec] * depth,
        out_specs=blk,
        out_shape=jax.ShapeDtypeStruct(state.shape, F32),
        compiler_params=_params("arbitrary"),
        name="conv_state_sample",
    )(state, *halves)


def _layer(l, last, next_gain, xp, xs, xn, xns, mem, batch, seq, kt, vt, mk4, mv4, state_conv, slope_b, sink_b,
           norm_mix_g, w_in, gmlp_norm_g, gmlp_ws, gmlp_bs, attn_sinks, mem_norm_g, w_mem_kv,
           w_br_g, w_br_a, w_br_m, w_out, norm_ffn_g, w_up, conv_w, conv_b, w_down):
    dec = xs.shape[0]
    nreq = 8
    bs = gmlp_bs[l]
    bsb = jnp.repeat(bs.T, G_GDIM, axis=1)
    mkv = _memkv(mem, mem_norm_g, w_mem_kv, l, 512)
    front = _front(attn_sinks, xp if xn is None else xn, xs if xn is None else xns, norm_mix_g, w_in,
                   gmlp_ws, gmlp_norm_g, bsb, mkv, l, seq, 512)
    kv, oa, og, vg_last, om, zs = front[:6]
    if xn is None:
        xn, xns = front[6:]

    coef = jnp.stack([jnp.repeat(gmlp_ws[l, :, 0, 0], G_GDIM), jnp.repeat(bs[:, 0], G_GDIM)])
    ogs, vgs = _gmlp_sample(zs, gmlp_norm_g, coef, l)

    q3 = zs[:, C_Q:C_K].reshape(dec, SWA_HEADS, SWA_HD)
    newcol = lambda a, b: zs[:, a:b].reshape(dec // nreq, nreq, SWA_KV_WIDTH).transpose(0, 2, 1)
    oas = _swa_sample(q3, kt, vt, zs, slope_b, sink_b, l, 4 * nreq)

    oms = _mem_sample(zs[:, C_M:C_G].reshape(dec, MEM_HEADS, MEM_HD), mk4, mv4, l, 2 * nreq)

    merged, mergeds = _merge(xn, og, oa, om, xns, ogs, oas.reshape(dec, SWA_WIDTH).astype(BF),
                             oms.reshape(dec, MEM_WIDTH).astype(BF), w_in, w_br_g, w_br_a, w_br_m, l, 1024, 256)
    xp, xn2, xs, xn2s = _out_proj(xp, merged, xs, mergeds, w_out, norm_ffn_g, l, 512, 512)
    g, ca, cb, gs, has, hbs = _up_conv(xn2, xn2s, w_up, conv_w, conv_b, state_conv, l, batch, seq, 1024, 512)
    outs = _down_proj(xp, g, xs, gs, w_down, next_gain, l, 256, 512, F32 if last else BF, not last)

    tail = lambda t: t.reshape(batch, SUBLANES, D_FF)[:, SUBLANES - (CONV_W - 1):]
    kvb = kv.reshape(batch, seq, 2 * SWA_KV_WIDTH)[:, seq - WINDOW:]
    st = dict(
        pk=kvb[:, :, :SWA_KV_WIDTH].reshape(batch, WINDOW, SWA_KV, SWA_HD),
        pv=kvb[:, :, SWA_KV_WIDTH:].reshape(batch, WINDOW, SWA_KV, SWA_HD),
        mk=mkv[:, :MEM_WIDTH].reshape(batch, MEM_LEN, MEM_HEADS, MEM_HD),
        mv=mkv[:, MEM_WIDTH:].reshape(batch, MEM_LEN, MEM_HEADS, MEM_HD),
        gvp=vg_last.reshape(batch, CHUNK, G_WIDTH),
        gvs=vgs.reshape(dec, 1, G_WIDTH),
        cvp=jnp.concatenate([tail(ca), tail(cb)], axis=-1),
        cvs=(has, hbs),
        knc=newcol(C_K, C_VV),
        vnc=newcol(C_VV, C_M),
    )
    return outs, st


def kernel(x_prompt, x_sample, cache_swa_k, cache_swa_v, cache_mem_k, cache_mem_v, state_conv, mem_prompt, norm_mix_g, w_in, gmlp_norm_g, gmlp_ws, gmlp_bs, attn_sinks, mem_norm_g, w_mem_kv, w_br_g, w_br_a, w_br_m, w_out, norm_ffn_g, w_up, conv_w, conv_b, w_down, final_norm_g):
    batch, seq, _ = x_prompt.shape
    dec = x_sample.shape[0]
    assert x_sample.shape[1] == 1 and PAST_LEN % CHUNK == 0 and PAST_LEN >= WINDOW
    assert seq % 2048 == 0 and seq >= WINDOW and (seq - 1) // CHUNK * CHUNK == seq - CHUNK
    xp = x_prompt.reshape(batch * seq, D_MODEL)
    xs = x_sample.reshape(dec, D_MODEL)
    mem = mem_prompt.reshape(batch * MEM_LEN, D_MODEL)
    kt = cache_swa_k.transpose(0, 1, 3, 4, 2)
    vt = cache_swa_v.transpose(0, 1, 3, 4, 2)
    mk4 = cache_mem_k.reshape(DEPTH, dec, MEM_LEN * MEM_HEADS, MEM_HD)
    mv4 = cache_mem_v.reshape(DEPTH, dec, MEM_LEN * MEM_HEADS, MEM_HD)
    slope_b = jnp.broadcast_to(jnp.asarray(SLOPES, F32)[:, None], (SWA_HEADS, LANES))
    sink_b = jnp.broadcast_to(attn_sinks[:, :, None], (DEPTH, SWA_HEADS, LANES))
    sts = []
    xn = xns = None
    for l in range(DEPTH):
        last = l == DEPTH - 1
        next_gain = (final_norm_g if last else norm_mix_g[l + 1]).reshape(1, D_MODEL)
        outs, st = _layer(l, last, next_gain, xp, xs, xn, xns, mem, batch, seq, kt, vt, mk4, mv4, state_conv,
                          slope_b, sink_b,
                          norm_mix_g, w_in, gmlp_norm_g, gmlp_ws, gmlp_bs, attn_sinks, mem_norm_g,
                          w_mem_kv, w_br_g, w_br_a, w_br_m, w_out, norm_ffn_g, w_up, conv_w, conv_b, w_down)
        sts.append(st)
        if last:
            y_prompt, y_sample = outs
        else:
            xp, xn, xs, xns = outs
    stack = lambda key: jnp.stack([s[key] for s in sts])
    nk, nv = _window_update(kt, vt, stack("knc"), stack("vnc"), 8)
    return (y_prompt.reshape(batch, seq, D_MODEL), y_sample.reshape(dec, 1, D_MODEL), stack("pk"), stack("pv"),
            nk.transpose(0, 1, 4, 2, 3), nv.transpose(0, 1, 4, 2, 3),
            stack("mk"), stack("mv"), stack("gvp"), stack("gvs"), stack("cvp"),
            _conv_state(state_conv, [h for s in sts for h in s["cvs"]], D_FF // 2))
```

```python
import functools
import math

import jax
import jax.numpy as jnp
from jax import lax
from jax.experimental import pallas as pl
from jax.experimental.pallas import tpu as pltpu

D_MODEL = 2048
DEPTH = 2
PAST_LEN = 8192
MEM_LEN = 256
CHUNK = 128
G_GROUPS = 8
G_WIDTH = 768
G_GDIM = G_WIDTH // G_GROUPS
SWA_HEADS = 12
SWA_KV = 4
SWA_GROUP = SWA_HEADS // SWA_KV
SWA_HD = 64
SWA_WIDTH = SWA_HEADS * SWA_HD
SWA_KV_WIDTH = SWA_KV * SWA_HD
WINDOW = 128
MEM_HEADS = 4
MEM_HD = 128
MEM_WIDTH = MEM_HEADS * MEM_HD
N_BRANCH = 3
D_FF = 5632
CONV_W = 3
EPS = 1e-6

LANES = 128
SUBLANES = 8
VMEM_LIMIT = 56 * 1024 * 1024

C_U, C_V, C_Q, C_K, C_VV, C_M, C_G = 0, 768, 1536, 2304, 2560, 2816, 3328
ZT = 256
N_ACT = C_Q // ZT
assert all(c % ZT == 0 for c in (C_V, C_Q, C_K, C_VV, C_M, C_G))

BF = jnp.bfloat16
F32 = jnp.float32
NT_DIMS = (((1,), (1,)), ((), ()))


def _alibi_slopes(n):
    p = 2 ** int(math.floor(math.log2(n)))
    base = [2.0 ** (-8.0 * (i + 1) / p) for i in range(p)]
    extra = [2.0 ** (-8.0 * (2 * i + 1) / (2 * p)) for i in range(n - p)]
    return base + extra


SLOPES = _alibi_slopes(SWA_HEADS)


def _params(*sem):
    return pltpu.CompilerParams(dimension_semantics=sem, vmem_limit_bytes=VMEM_LIMIT)


def _resident(shape):
    return pl.BlockSpec(shape, lambda *_: (0,) * len(shape), pipeline_mode=pl.Buffered(1))


def _rms(x, g):
    ms = jnp.mean(x * x, axis=-1, keepdims=True)
    return x * lax.rsqrt(ms + EPS) * g


def _dot(a, b):
    return jnp.dot(a, b, preferred_element_type=F32)


def _first_inner():
    return pl.program_id(1) == 0


def _last_inner():
    return pl.program_id(1) == pl.num_programs(1) - 1


NZ = C_G // ZT


QKV_TILES = range(C_Q // ZT, C_M // ZT)
QKV_W = C_M - C_Q


def _swa_bias_table(bias_ref):
    r = lax.broadcasted_iota(jnp.int32, (WINDOW, 2 * WINDOW), 0)
    c = lax.broadcasted_iota(jnp.int32, (WINDOW, 2 * WINDOW), 1)
    dist = r + WINDOW - c
    valid = (dist >= 0) & (dist <= WINDOW)
    distf = dist.astype(F32)
    for h in range(SWA_HEADS):
        j, g = divmod(h, SWA_GROUP)
        pen = -SLOPES[h] * distf
        bias_ref[j, g * WINDOW:(g + 1) * WINDOW, :] = jnp.where(valid, pen, -jnp.inf)
        bias_ref[SWA_KV + j, g * WINDOW:(g + 1) * WINDOW, :] = jnp.where(valid & (c >= WINDOW), pen, -jnp.inf)


def _front_kernel(sink_ref, xn_ref, xs_ref, ng_ref, w_ref, ws_ref, gg_ref, bsb_ref, mkv_ref,
                  kv_ref, oa_ref, og_ref, vg_ref, om_ref, zs_ref, *rest, l, tm, tiles_per_seq, raw, group=4):
    xno_ref, xnso_ref = rest[:2] if raw else (None, None)
    wbf_ref, qkv_ref, zt_ref, carry_ref, bias_ref = rest[-5:]
    s = pl.program_id(0)

    def normed(x_ref, out_ref):
        if not raw:
            return x_ref[...]
        x = _rms(x_ref[...], ng_ref[l:l + 1, :]).astype(BF)
        out_ref[...] = x
        return x

    rows = SWA_GROUP * WINDOW
    kcol, vcol = C_K - C_Q, C_VV - C_Q
    mcol = C_Q

    @pl.when(s < NZ)
    def _():
        wbf_ref[s] = w_ref[...].astype(BF)

    @pl.when(s == NZ)
    def _():
        _swa_bias_table(bias_ref)
        carry_ref[...] = jnp.zeros(carry_ref.shape, F32)

    def column(x, c):
        h = _dot(x, wbf_ref[c])
        return jax.nn.gelu(h) if c < N_ACT else h

    @pl.when(s >= NZ)
    def _():
        x = normed(xn_ref, xno_ref)
        for c in QKV_TILES:
            qkv_ref[:, c * ZT - C_Q:(c + 1) * ZT - C_Q] = column(x, c)
        kv_ref[...] = qkv_ref[:, kcol:]

        def project(c, col):
            def task():
                zt_ref[:, col:col + ZT] = column(x, c)
            return task

        tasks = [project(c, c * ZT) for c in range(N_ACT)]
        tasks += [project(c, mcol + (c - C_M // ZT) * ZT) for c in range(C_M // ZT, NZ)]

        r = lax.broadcasted_iota(jnp.int32, (CHUNK, CHUNK), 0)
        cc = lax.broadcasted_iota(jnp.int32, (CHUNK, CHUNK), 1)
        lane = lax.broadcasted_iota(jnp.int32, (CHUNK, LANES), 1)

        def gmlp_chunk(n):
            def task():
                wtri = [jnp.where(r >= cc, ws_ref[g], 0.0).astype(BF) for g in range(G_GROUPS)]
                crow = slice(n * CHUNK, (n + 1) * CHUNK)
                vg = _rms(zt_ref[crow, C_V:C_Q], gg_ref[l:l + 1, :])
                if n == tm // CHUNK - 1:
                    vg_ref[...] = vg
                for t in range(G_TILES):
                    cols = slice(t * LANES, (t + 1) * LANES)
                    wst = jnp.concatenate([wtri[GA[t]], wtri[GA[t] + 1]], axis=0)
                    both = _dot(wst, vg[:, cols].astype(BF))
                    mix = jnp.where(lane < BND[t], both[:CHUNK], both[CHUNK:]) + bsb_ref[:, cols]
                    og_ref[crow, cols] = (zt_ref[crow, cols] * mix).astype(BF)
            return task

        tasks += [gmlp_chunk(n) for n in range(tm // CHUNK)]

        def mem_head(h):
            def task():
                cols = slice(h * MEM_HD, (h + 1) * MEM_HD)
                q = zt_ref[:, mcol + h * MEM_HD:mcol + (h + 1) * MEM_HD].astype(BF)
                sc = lax.dot_general(q, mkv_ref[:, cols].astype(BF), NT_DIMS,
                                     preferred_element_type=F32) * (MEM_HD ** -0.5)
                e = jnp.exp(sc - jnp.max(sc, axis=-1, keepdims=True))
                p = (e / jnp.sum(e, axis=-1, keepdims=True)).astype(BF)
                vcols = slice(MEM_WIDTH + h * MEM_HD, MEM_WIDTH + (h + 1) * MEM_HD)
                om_ref[:, cols] = _dot(p, mkv_ref[:, vcols].astype(BF)).astype(BF)
            return task

        tasks += [mem_head(h) for h in range(MEM_HEADS)]

        units = [(b, j) for b in range(tm // WINDOW) for j in range(SWA_KV)]
        groups = [units[k:k + group] for k in range(0, len(units), group)]
        per = len(tasks) // len(groups)
        assert per * len(groups) == len(tasks) and per % 2 == 0
        head = lax.broadcasted_iota(jnp.int32, (rows, 1), 0) // WINDOW
        fresh = ((s - NZ) % tiles_per_seq == 0).astype(jnp.int32) * SWA_KV

        def scores(b, j):
            own = slice(b * WINDOW, (b + 1) * WINDOW)
            kvh = lambda base: slice(base + j * SWA_HD, base + (j + 1) * SWA_HD)
            if b == 0:
                kprev, vprev = carry_ref[:, kvh(0)], carry_ref[:, kvh(SWA_KV_WIDTH)]
            else:
                prev = slice((b - 1) * WINDOW, b * WINDOW)
                kprev, vprev = qkv_ref[prev, kvh(kcol)], qkv_ref[prev, kvh(vcol)]
            kj = jnp.concatenate([kprev, qkv_ref[own, kvh(kcol)]], axis=0).astype(BF)
            vj = jnp.concatenate([vprev, qkv_ref[own, kvh(vcol)]], axis=0).astype(BF)
            hs = range(j * SWA_GROUP, (j + 1) * SWA_GROUP)
            q = jnp.concatenate([qkv_ref[own, h * SWA_HD:(h + 1) * SWA_HD] for h in hs], axis=0).astype(BF)
            sink = jnp.full((rows, 1), sink_ref[l, hs[-1]], F32)
            for g in range(SWA_GROUP - 1):
                sink = jnp.where(head == g, sink_ref[l, hs[g]], sink)
            bias = bias_ref[(fresh if b == 0 else 0) + j]
            sc = lax.dot_general(q, kj, NT_DIMS, preferred_element_type=F32) * (SWA_HD ** -0.5) + bias
            return sc, sink, vj

        def softmax(sc, sink, vj):
            mx = jnp.maximum(jnp.max(sc, axis=-1, keepdims=True), sink)
            p = jnp.exp(sc - mx)
            den = jnp.sum(p, axis=-1, keepdims=True) + jnp.exp(sink - mx)
            return p.astype(BF), den, vj

        def values(b, j, p, den, vj):
            o = _dot(p, vj) / den
            own = slice(b * WINDOW, (b + 1) * WINDOW)
            for g, h in enumerate(range(j * SWA_GROUP, (j + 1) * SWA_GROUP)):
                oa_ref[own, h * SWA_HD:(h + 1) * SWA_HD] = o[g * WINDOW:(g + 1) * WINDOW].astype(BF)

        for k, grp in enumerate(groups):
            for task in tasks[per * k:per * k + per // 2]:
                task()
            stage = [scores(b, j) for b, j in grp]
            stage = [softmax(*args) for args in stage]
            for task in tasks[per * k + per // 2:per * (k + 1)]:
                task()
            for (b, j), args in zip(grp, stage):
                values(b, j, *args)
        last = slice(tm - WINDOW, tm)
        carry_ref[...] = qkv_ref[last, kcol:]

    @pl.when(s == pl.num_programs(0) - 1)
    def _():
        xs = normed(xs_ref, xnso_ref)
        for c in range(NZ):
            zs_ref[:, c * ZT:(c + 1) * ZT] = column(xs, c)


def _front(sinks, xn, xs, ng, w_in, ws, gg, bsb, mkv, l, seq, tm):
    m, ms = xn.shape[0], xs.shape[0]
    raw = xn.dtype == F32
    tiles_per_seq = seq // tm
    tile = lambda s: jnp.maximum(s - NZ, 0)
    rows = lambda n: pl.BlockSpec((tm, n), lambda s: (tile(s), 0))
    fixed = lambda n: pl.BlockSpec((ms, n), lambda s: (0, 0))
    per_seq = lambda r, n: pl.BlockSpec((r, n), lambda s: (tile(s) // tiles_per_seq, 0))
    outs = [(2 * SWA_KV_WIDTH, F32), (SWA_WIDTH, BF), (G_WIDTH, BF)]
    return pl.pallas_call(
        functools.partial(_front_kernel, l=l, tm=tm, tiles_per_seq=tiles_per_seq, raw=raw),
        grid=(NZ + m // tm,),
        in_specs=[pl.BlockSpec(memory_space=pltpu.SMEM), rows(D_MODEL), _resident((ms, D_MODEL)), _resident(ng.shape),
                  pl.BlockSpec((None, D_MODEL, ZT), lambda s: (l, 0, jnp.minimum(s, NZ - 1))),
                  pl.BlockSpec((None, G_GROUPS, CHUNK, CHUNK), lambda s: (l, 0, 0, 0)),
                  _resident(gg.shape), _resident((CHUNK, G_WIDTH)), per_seq(MEM_LEN, 2 * MEM_WIDTH)],
        out_specs=[rows(n) for n, _ in outs] + [per_seq(CHUNK, G_WIDTH), rows(MEM_WIDTH), fixed(C_G)]
                  + ([rows(D_MODEL), fixed(D_MODEL)] if raw else []),
        out_shape=[jax.ShapeDtypeStruct((m, n), dt) for n, dt in outs]
                  + [jax.ShapeDtypeStruct((m // seq * CHUNK, G_WIDTH), F32), jax.ShapeDtypeStruct((m, MEM_WIDTH), BF),
                     jax.ShapeDtypeStruct((ms, C_G), F32)]
                  + ([jax.ShapeDtypeStruct((m, D_MODEL), BF), jax.ShapeDtypeStruct((ms, D_MODEL), BF)] if raw else []),
        scratch_shapes=[pltpu.VMEM((NZ, D_MODEL, ZT), BF), pltpu.VMEM((tm, QKV_W), F32),
                        pltpu.VMEM((tm, C_Q + MEM_WIDTH), F32), pltpu.VMEM((WINDOW, 2 * SWA_KV_WIDTH), F32),
                        pltpu.VMEM((2 * SWA_KV, SWA_GROUP * WINDOW, 2 * WINDOW), F32)],
        compiler_params=_params("arbitrary"),
        name="front",
    )(sinks, xn, xs, ng, w_in, ws, gg, bsb, mkv)


G_TILES = G_WIDTH // LANES
GA = [(t * LANES) // G_GDIM for t in range(G_TILES)]
BND = [(GA[t] + 1) * G_GDIM - t * LANES for t in range(G_TILES)]
assert all(0 < b < LANES and (GA[t] + 2) * G_GDIM >= (t + 1) * LANES for t, b in enumerate(BND))


def _gmlp_sample_kernel(zs_ref, gg_ref, coef_ref, og_ref, vg_ref, *, l):
    vg = _rms(zs_ref[:, C_V:C_Q], gg_ref[l:l + 1, :])
    vg_ref[...] = vg
    og_ref[...] = (zs_ref[:, C_U:C_V] * (coef_ref[0:1, :] * vg + coef_ref[1:2, :])).astype(BF)


def _gmlp_sample(zs, gg, coef, l):
    m = zs.shape[0]
    return pl.pallas_call(
        functools.partial(_gmlp_sample_kernel, l=l),
        out_shape=[jax.ShapeDtypeStruct((m, G_WIDTH), BF), jax.ShapeDtypeStruct((m, G_WIDTH), F32)],
        name="gmlp_sample",
    )(zs, gg, coef)


def _swa_sample_kernel(q_ref, kt_ref, vt_ref, kn_ref, vn_ref, slope_ref, sink_ref, o_ref, *, nreq):
    kvw = SWA_KV_WIDTH
    hrow = lax.broadcasted_iota(jnp.int32, (SWA_HEADS, kvw), 0) // SWA_GROUP
    lblk = lax.broadcasted_iota(jnp.int32, (SWA_HEADS, kvw), 1) // SWA_HD
    own = hrow == lblk
    kvh = lax.broadcasted_iota(jnp.int32, (SWA_HEADS, SWA_HD), 0) // SWA_GROUP
    c = lax.broadcasted_iota(jnp.int32, (SWA_HEADS, WINDOW), 1)
    bias = slope_ref[...] * (WINDOW - c).astype(F32)
    sink = sink_ref[:, 0:1]
    scale = SWA_HD ** -0.5
    q = q_ref[...]
    qm = jnp.where(own, jnp.concatenate([q] * SWA_KV, axis=2), 0.0).astype(BF)
    kt = kt_ref[...].reshape(nreq, kvw, WINDOW)
    vt = vt_ref[...].reshape(nreq, kvw, WINDOW)
    rows = lambda ref: jnp.stack([ref[r:r + 1, :] for r in range(nreq)]).astype(BF).astype(F32)
    kn, vn = rows(kn_ref), rows(vn_ref)
    s_c = jnp.einsum("rhk,rkc->rhc", qm, kt.astype(BF), preferred_element_type=F32) * scale - bias
    s_n = jnp.sum(qm.astype(F32) * kn, axis=-1, keepdims=True) * scale
    mx = jnp.maximum(jnp.maximum(jnp.max(s_c, axis=-1, keepdims=True), s_n), sink)
    p_c = jnp.exp(s_c - mx)
    p_n = jnp.exp(s_n - mx)
    den = jnp.sum(p_c, axis=-1, keepdims=True) + p_n + jnp.exp(sink - mx)
    p_c = (p_c / den).astype(BF)
    p_n = (p_n / den).astype(BF).astype(F32)
    o_all = jnp.einsum("rhc,rkc->rhk", p_c, vt.astype(BF), preferred_element_type=F32) + p_n * vn
    o = jnp.zeros((nreq, SWA_HEADS, SWA_HD), F32)
    for j in range(SWA_KV):
        o = jnp.where(kvh == j, o_all[:, :, j * SWA_HD:(j + 1) * SWA_HD], o)
    o_ref[...] = o


def _swa_sample(q3, kt, vt, zs, slope_b, sink_b, l, nreq):
    m = q3.shape[0]
    cache = pl.BlockSpec((None, nreq, SWA_KV, SWA_HD, WINDOW), lambda i: (l, i, 0, 0, 0))
    return pl.pallas_call(
        functools.partial(_swa_sample_kernel, nreq=nreq),
        grid=(m // nreq,),
        in_specs=[pl.BlockSpec((nreq, SWA_HEADS, SWA_HD), lambda i: (i, 0, 0)), cache, cache,
                  pl.BlockSpec((nreq, SWA_KV_WIDTH), lambda i: (i, C_K // SWA_KV_WIDTH)),
                  pl.BlockSpec((nreq, SWA_KV_WIDTH), lambda i: (i, C_VV // SWA_KV_WIDTH)),
                  _resident((SWA_HEADS, LANES)), pl.BlockSpec((None, SWA_HEADS, LANES), lambda i: (l, 0, 0))],
        out_specs=pl.BlockSpec((nreq, SWA_HEADS, SWA_HD), lambda i: (i, 0, 0)),
        out_shape=jax.ShapeDtypeStruct((m, SWA_HEADS, SWA_HD), F32),
        compiler_params=_params("arbitrary"),
        name="swa_sample",
    )(q3, kt, vt, zs, zs, slope_b, sink_b)


def _shift_windows(kt_ref, vt_ref, knc_ref, vnc_ref, nk_ref, nv_ref, nreq):
    last = lax.broadcasted_iota(jnp.int32, (SWA_KV_WIDTH, WINDOW), 1) == WINDOW - 1
    for src_ref, col_ref, dst_ref in ((kt_ref, knc_ref, nk_ref), (vt_ref, vnc_ref, nv_ref)):
        for l in range(DEPTH):
            for r in range(nreq):
                old = src_ref[l, r].reshape(SWA_KV_WIDTH, WINDOW)
                new = jnp.where(last, col_ref[l, :, r:r + 1], pltpu.roll(old, WINDOW - 1, axis=1))
                dst_ref[l, r] = new.reshape(SWA_KV, SWA_HD, WINDOW)


def _memkv_kernel(x_ref, g_ref, w_ref, o_ref, xn_ref, *, l):
    @pl.when(pl.program_id(0) == 0)
    def _():
        xn_ref[...] = _rms(x_ref[...], g_ref[l:l + 1, :]).astype(BF)

    o_ref[...] = _dot(xn_ref[...], w_ref[...].astype(BF))


def _memkv(mem, g, w, l, tn):
    m = mem.shape[0]
    return pl.pallas_call(
        functools.partial(_memkv_kernel, l=l),
        grid=(2 * MEM_WIDTH // tn,),
        in_specs=[_resident((m, D_MODEL)), _resident(g.shape),
                  pl.BlockSpec((None, D_MODEL, tn), lambda j: (l, 0, j))],
        out_specs=pl.BlockSpec((m, tn), lambda j: (0, j)),
        out_shape=jax.ShapeDtypeStruct((m, 2 * MEM_WIDTH), F32),
        scratch_shapes=[pltpu.VMEM((m, D_MODEL), BF)],
        compiler_params=_params("arbitrary"),
        name="mem_kv",
    )(mem, g, w)


def _mem_sample_kernel(q_ref, mk_ref, mv_ref, *rest, nreq):
    o_ref = rest[0] if len(rest) == 1 else rest[4]
    nrow = MEM_LEN * MEM_HEADS
    own = (lax.broadcasted_iota(jnp.int32, (nreq, MEM_HEADS, nrow), 2) % MEM_HEADS
           == lax.broadcasted_iota(jnp.int32, (nreq, MEM_HEADS, nrow), 1))
    s = jnp.einsum("rhd,rkd->rhk", q_ref[...].astype(BF), mk_ref[...].astype(BF),
                   preferred_element_type=F32) * (MEM_HD ** -0.5)
    s = jnp.where(own, s, -jnp.inf)
    e = jnp.exp(s - jnp.max(s, axis=-1, keepdims=True))
    p = (e / jnp.sum(e, axis=-1, keepdims=True)).astype(BF)
    o_ref[...] = jnp.einsum("rhk,rkd->rhd", p, mv_ref[...].astype(BF), preferred_element_type=F32)
    if len(rest) > 1:
        _shift_windows(*rest[:4], *rest[5:], nreq)


def _mem_sample(q3, mk, mv, l, nreq, window=None):
    m = q3.shape[0]
    qspec = pl.BlockSpec((nreq, MEM_HEADS, MEM_HD), lambda i: (i, 0, 0))
    kvspec = pl.BlockSpec((None, nreq, MEM_LEN * MEM_HEADS, MEM_HD), lambda i: (l, i, 0, 0))
    in_specs, out_specs = [qspec, kvspec, kvspec], [qspec]
    out_shape = [jax.ShapeDtypeStruct((m, MEM_HEADS, MEM_HD), F32)]
    args = [q3, mk, mv]
    if window is not None:
        kt, vt = window[:2]
        cache = pl.BlockSpec((DEPTH, nreq, SWA_KV, SWA_HD, WINDOW), lambda i: (0, i, 0, 0, 0))
        newcol = pl.BlockSpec((DEPTH, None, SWA_KV_WIDTH, nreq), lambda i: (0, i, 0, 0))
        in_specs += [cache, cache, newcol, newcol]
        out_specs += [cache, cache]
        out_shape += [jax.ShapeDtypeStruct(kt.shape, F32), jax.ShapeDtypeStruct(vt.shape, F32)]
        args += list(window)
    return pl.pallas_call(
        functools.partial(_mem_sample_kernel, nreq=nreq),
        grid=(m // nreq,),
        in_specs=in_specs,
        out_specs=out_specs,
        out_shape=out_shape,
        compiler_params=_params("arbitrary"),
        name="mem_attn_sample",
    )(*args)


def _merge_kernel(xn_ref, og_ref, oa_ref, om_ref, xns_ref, ogs_ref, oas_ref, oms_ref,
                  wg0_ref, wg1_ref, wg2_ref, wbg_ref, wba_ref, wbm_ref, out_ref, outs_ref,
                  bg0_ref, bg1_ref, bg2_ref, bbg_ref, bba_ref, bbm_ref):
    pairs = [(wg0_ref, bg0_ref), (wg1_ref, bg1_ref), (wg2_ref, bg2_ref),
             (wbg_ref, bbg_ref), (wba_ref, bba_ref), (wbm_ref, bbm_ref)]

    @pl.when(_first_inner())
    def _():
        for w_ref, b_ref in pairs:
            b_ref[...] = w_ref[...].astype(BF)

    def merged(xn, og, oa, om):
        def branch(bg_ref, o, bb_ref):
            return jax.nn.sigmoid(_dot(xn, bg_ref[...])) * _dot(o, bb_ref[...])
        return (branch(bg0_ref, og, bbg_ref) + branch(bg1_ref, oa, bba_ref) + branch(bg2_ref, om, bbm_ref)).astype(BF)

    out_ref[...] = merged(xn_ref[...], og_ref[...], oa_ref[...], om_ref[...])

    @pl.when(_last_inner())
    def _():
        outs_ref[...] = merged(xns_ref[...], ogs_ref[...], oas_ref[...], oms_ref[...])


def _merge(xn, og, oa, om, xns, ogs, oas, oms, w_in, wbg, wba, wbm, l, tm, tn):
    m, ms = xn.shape[0], xns.shape[0]
    nt = D_MODEL // tn
    row = lambda n: pl.BlockSpec((tm, n), lambda j, i: (i, 0))
    gate = lambda b: pl.BlockSpec((None, D_MODEL, tn), lambda j, i: (l, 0, C_G // tn + b * nt + j))
    col = lambda n: pl.BlockSpec((None, n, tn), lambda j, i: (l, 0, j))
    widths = (G_WIDTH, SWA_WIDTH, MEM_WIDTH)
    return pl.pallas_call(
        _merge_kernel,
        grid=(nt, m // tm),
        in_specs=[row(D_MODEL)] + [row(n) for n in widths]
                 + [_resident((ms, D_MODEL))] + [_resident((ms, n)) for n in widths]
                 + [gate(0), gate(1), gate(2)] + [col(n) for n in widths],
        out_specs=[pl.BlockSpec((tm, tn), lambda j, i: (i, j)), pl.BlockSpec((ms, tn), lambda j, i: (0, j))],
        out_shape=[jax.ShapeDtypeStruct((m, D_MODEL), BF), jax.ShapeDtypeStruct((ms, D_MODEL), BF)],
        scratch_shapes=[pltpu.VMEM((D_MODEL, tn), BF)] * 3 + [pltpu.VMEM((n, tn), BF) for n in widths],
        compiler_params=_params("arbitrary", "arbitrary"),
        name="merge",
    )(xn, og, oa, om, xns, ogs, oas, oms, w_in, w_in, w_in, wbg, wba, wbm)


def _out_kernel(x_ref, mg_ref, xs_ref, mgs_ref, w_ref, g_ref, xo_ref, xn_ref, xos_ref, xns_ref, wbf_ref,
                *, l, tk, ncast):
    s = pl.program_id(0)

    @pl.when(s < ncast)
    def _():
        wbf_ref[pl.ds(pl.multiple_of(s * tk, tk), tk), :] = w_ref[...].astype(BF)

    def tile(x_ref, mg_ref, xo_ref, xn_ref):
        x = x_ref[...] + _dot(mg_ref[...], wbf_ref[...])
        xo_ref[...] = x
        xn_ref[...] = _rms(x, g_ref[l:l + 1, :]).astype(BF)

    @pl.when(s >= ncast)
    def _():
        tile(x_ref, mg_ref, xo_ref, xn_ref)

    @pl.when(s == pl.num_programs(0) - 1)
    def _():
        tile(xs_ref, mgs_ref, xos_ref, xns_ref)


def _out_proj(x, merged, xs, mergeds, w_out, g, l, tm, tk):
    m, ms = x.shape[0], xs.shape[0]
    ncast = D_MODEL // tk
    row = pl.BlockSpec((tm, D_MODEL), lambda s: (jnp.maximum(s - ncast, 0), 0))
    srow = pl.BlockSpec((ms, D_MODEL), lambda s: (0, 0))
    return pl.pallas_call(
        functools.partial(_out_kernel, l=l, tk=tk, ncast=ncast),
        grid=(ncast + m // tm,),
        in_specs=[row, row, _resident((ms, D_MODEL)), _resident((ms, D_MODEL)),
                  pl.BlockSpec((None, tk, D_MODEL), lambda s: (l, jnp.minimum(s, ncast - 1), 0)),
                  _resident(g.shape)],
        out_specs=[row, row, srow, srow],
        out_shape=[jax.ShapeDtypeStruct((m, D_MODEL), F32), jax.ShapeDtypeStruct((m, D_MODEL), BF),
                   jax.ShapeDtypeStruct((ms, D_MODEL), F32), jax.ShapeDtypeStruct((ms, D_MODEL), BF)],
        scratch_shapes=[pltpu.VMEM((D_MODEL, D_MODEL), BF)],
        compiler_params=_params("arbitrary"),
        name="out_proj",
    )(x, merged, xs, mergeds, w_out, g)


def _up_kernel(xn_ref, xns_ref, wa_ref, wb_ref, cwa_ref, cwb_ref, cba_ref, cbb_ref, sta_ref, stb_ref,
               g_ref, ca_ref, cb_ref, gs_ref, has_ref, hbs_ref, ha_ref, hb_ref, ba_ref, bb_ref,
               *, l, tm, tiles_per_seq):
    @pl.when(_first_inner())
    def _():
        ba_ref[...] = wa_ref[...].astype(BF)
        bb_ref[...] = wb_ref[...].astype(BF)

    @pl.when(pl.program_id(1) % tiles_per_seq == 0)
    def _():
        ha_ref[...] = jnp.zeros(ha_ref.shape, F32)
        hb_ref[...] = jnp.zeros(hb_ref.shape, F32)

    top = lax.broadcasted_iota(jnp.int32, (SUBLANES, ha_ref.shape[1]), 0)
    xn = xn_ref[...]

    def conv(h, prev_ref, cw_ref, bias_ref, tail_ref):
        prev = prev_ref[...]
        acc = bias_ref[l:l + 1, :]
        for j in range(CONV_W - 1):
            back = CONV_W - 1 - j
            rolled = pltpu.roll(h, back, axis=0)
            head = jnp.where(top < back, pltpu.roll(prev, back, axis=0), rolled[0:SUBLANES])
            acc = acc + jnp.concatenate([head, rolled[SUBLANES:]], axis=0) * cw_ref[j:j + 1, :]
        tail = h[tm - SUBLANES:]
        tail_ref[...] = tail
        prev_ref[...] = tail
        return acc + h * cw_ref[CONV_W - 1:CONV_W, :]

    a = conv(_dot(xn, ba_ref[...]), ha_ref, cwa_ref, cba_ref, ca_ref)
    b = conv(_dot(xn, bb_ref[...]), hb_ref, cwb_ref, cbb_ref, cb_ref)
    g_ref[...] = (jax.nn.gelu(a) * b).astype(BF)

    @pl.when(_last_inner())
    def _():
        xns = xns_ref[...]

        def conv_s(bw_ref, cw_ref, bias_ref, st_ref, h_out_ref):
            h = _dot(xns, bw_ref[...])
            h_out_ref[...] = h
            return (bias_ref[l:l + 1, :] + st_ref[:, 0, :] * cw_ref[0:1, :] + st_ref[:, 1, :] * cw_ref[1:2, :]
                    + h * cw_ref[2:3, :])

        a_s = conv_s(ba_ref, cwa_ref, cba_ref, sta_ref, has_ref)
        b_s = conv_s(bb_ref, cwb_ref, cbb_ref, stb_ref, hbs_ref)
        gs_ref[...] = (jax.nn.gelu(a_s) * b_s).astype(BF)


def _up_conv(xn, xns, w_up, conv_w, conv_b, state, l, batch, seq, tm, tn):
    m, ms = xn.shape[0], xns.shape[0]
    nt = D_FF // tn
    nrow = m // tm
    tiles_per_seq = seq // tm
    half = lambda off: (lambda j, i: (l, 0, off * nt + j))
    wspec = lambda off: pl.BlockSpec((None, D_MODEL, tn), half(off))
    cwspec = lambda off: pl.BlockSpec((None, CONV_W, tn), half(off))
    cbspec = lambda off: pl.BlockSpec((DEPTH, tn), lambda j, i: (0, off * nt + j))
    stspec = lambda off: pl.BlockSpec((None, ms, CONV_W - 1, tn), lambda j, i: (l, 0, 0, off * nt + j))
    tail = pl.BlockSpec((SUBLANES, tn), lambda j, i: (i // tiles_per_seq, j))
    scol = pl.BlockSpec((ms, tn), lambda j, i: (0, j))
    return pl.pallas_call(
        functools.partial(_up_kernel, l=l, tm=tm, tiles_per_seq=tiles_per_seq),
        grid=(nt, nrow),
        in_specs=[pl.BlockSpec((tm, D_MODEL), lambda j, i: (i, 0)), _resident((ms, D_MODEL)),
                  wspec(0), wspec(1), cwspec(0), cwspec(1), cbspec(0), cbspec(1), stspec(0), stspec(1)],
        out_specs=[pl.BlockSpec((tm, tn), lambda j, i: (i, j)), tail, tail, scol, scol, scol],
        out_shape=[jax.ShapeDtypeStruct((m, D_FF), BF),
                   jax.ShapeDtypeStruct((batch * SUBLANES, D_FF), F32),
                   jax.ShapeDtypeStruct((batch * SUBLANES, D_FF), F32),
                   jax.ShapeDtypeStruct((ms, D_FF), BF),
                   jax.ShapeDtypeStruct((ms, D_FF), F32), jax.ShapeDtypeStruct((ms, D_FF), F32)],
        scratch_shapes=[pltpu.VMEM((SUBLANES, tn), F32)] * 2 + [pltpu.VMEM((D_MODEL, tn), BF)] * 2,
        compiler_params=_params("arbitrary", "arbitrary"),
        name="up_conv_glu",
    )(xn, xns, w_up, w_up, conv_w, conv_w, conv_b, conv_b, state, state)


def _down_kernel(x_ref, g_ref, xs_ref, gs_ref, w_ref, gain_ref, *rest, tk, ncast, emit_x):
    wbf_ref = rest[-1]
    outs = rest[:-1]
    s = pl.program_id(0)

    @pl.when(s < ncast)
    def _():
        wbf_ref[pl.ds(pl.multiple_of(s * tk, tk), tk), :] = w_ref[...].astype(BF)

    def tile(x, g, o_refs):
        x = x + _dot(g, wbf_ref[...])
        if emit_x:
            o_refs[0][...] = x
        o_refs[-1][...] = _rms(x, gain_ref[...]).astype(o_refs[-1].dtype)

    nout = len(outs) // 2

    @pl.when(s >= ncast)
    def _():
        tile(x_ref[...], g_ref[...], outs[:nout])

    @pl.when(s == pl.num_programs(0) - 1)
    def _():
        tile(xs_ref[...], gs_ref[...], outs[nout:])


def _down_proj(x, g, xs, gs, w_down, gain, l, tm, tk, norm_dtype, emit_x):
    m, ms = x.shape[0], xs.shape[0]
    ncast = D_FF // tk
    rows = lambda n: pl.BlockSpec((tm, n), lambda s: (jnp.maximum(s - ncast, 0), 0))
    fixed = lambda n: pl.BlockSpec((ms, n), lambda s: (0, 0))
    dts = ([F32] if emit_x else []) + [norm_dtype]
    return pl.pallas_call(
        functools.partial(_down_kernel, tk=tk, ncast=ncast, emit_x=emit_x),
        grid=(ncast + m // tm,),
        in_specs=[rows(D_MODEL), rows(D_FF), _resident((ms, D_MODEL)), _resident((ms, D_FF)),
                  pl.BlockSpec((None, tk, D_MODEL), lambda s: (l, jnp.minimum(s, ncast - 1), 0)),
                  _resident((1, D_MODEL))],
        out_specs=[rows(D_MODEL) for _ in dts] + [fixed(D_MODEL) for _ in dts],
        out_shape=[jax.ShapeDtypeStruct((m, D_MODEL), dt) for dt in dts]
                  + [jax.ShapeDtypeStruct((ms, D_MODEL), dt) for dt in dts],
        scratch_shapes=[pltpu.VMEM((D_FF, D_MODEL), BF)],
        compiler_params=_params("arbitrary"),
        name="down_proj",
    )(x, g, xs, gs, w_down, gain)


def _conv_state_kernel(st_ref, *rest, nt):
    o_ref = rest[-1]
    first_half = pl.program_id(0) < nt
    for l in range(DEPTH):
        ha_ref, hb_ref = rest[2 * l], rest[2 * l + 1]
        o_ref[l, :, 0, :] = st_ref[l, :, CONV_W - 2, :]
        o_ref[l, :, 1, :] = jnp.where(first_half, ha_ref[...], hb_ref[...])


def _conv_state(state, halves, tn):
    depth, ms = state.shape[:2]
    nt = D_FF // tn
    blk = pl.BlockSpec((depth, ms, CONV_W - 1, tn), lambda j: (0, 0, 0, j))
    a_spec = pl.BlockSpec((ms, tn), lambda j: (0, jnp.minimum(j, nt - 1)))
    b_spec = pl.BlockSpec((ms, tn), lambda j: (0, jnp.maximum(j - nt, 0)))
    return pl.pallas_call(
        functools.partial(_conv_state_kernel, nt=nt),
        grid=(2 * nt,),
        in_specs=[blk] + [a_spec, b_spec] * depth,
        out_specs=blk,
        out_shape=jax.ShapeDtypeStruct(state.shape, F32),
        compiler_params=_params("arbitrary"),
        name="conv_state_sample",
    )(state, *halves)


def _layer(l, last, next_gain, xp, xs, xn, xns, mem, batch, seq, kt, vt, mk4, mv4, state_conv, slope_b, sink_b,
           prev_knc, prev_vnc,
           norm_mix_g, w_in, gmlp_norm_g, gmlp_ws, gmlp_bs, attn_sinks, mem_norm_g, w_mem_kv,
           w_br_g, w_br_a, w_br_m, w_out, norm_ffn_g, w_up, conv_w, conv_b, w_down):
    dec = xs.shape[0]
    nreq = 8
    bs = gmlp_bs[l]
    bsb = jnp.repeat(bs.T, G_GDIM, axis=1)
    mkv = _memkv(mem, mem_norm_g, w_mem_kv, l, 512)
    front = _front(attn_sinks, xp if xn is None else xn, xs if xn is None else xns, norm_mix_g, w_in,
                   gmlp_ws, gmlp_norm_g, bsb, mkv, l, seq, 512)
    kv, oa, og, vg_last, om, zs = front[:6]
    if xn is None:
        xn, xns = front[6:]

    coef = jnp.stack([jnp.repeat(gmlp_ws[l, :, 0, 0], G_GDIM), jnp.repeat(bs[:, 0], G_GDIM)])
    ogs, vgs = _gmlp_sample(zs, gmlp_norm_g, coef, l)

    q3 = zs[:, C_Q:C_K].reshape(dec, SWA_HEADS, SWA_HD)
    newcol = lambda a, b: zs[:, a:b].reshape(dec // nreq, nreq, SWA_KV_WIDTH).transpose(0, 2, 1)
    oas = _swa_sample(q3, kt, vt, zs, slope_b, sink_b, l, 4 * nreq)

    knc, vnc = newcol(C_K, C_VV), newcol(C_VV, C_M)
    qm3 = zs[:, C_M:C_G].reshape(dec, MEM_HEADS, MEM_HD)
    nk = nv = None
    if last:
        window = (kt, vt, jnp.stack([*prev_knc, knc]), jnp.stack([*prev_vnc, vnc]))
        oms, nk, nv = _mem_sample(qm3, mk4, mv4, l, nreq, window)
    else:
        (oms,) = _mem_sample(qm3, mk4, mv4, l, 2 * nreq)

    merged, mergeds = _merge(xn, og, oa, om, xns, ogs, oas.reshape(dec, SWA_WIDTH).astype(BF),
                             oms.reshape(dec, MEM_WIDTH).astype(BF), w_in, w_br_g, w_br_a, w_br_m, l, 1024, 256)
    xp, xn2, xs, xn2s = _out_proj(xp, merged, xs, mergeds, w_out, norm_ffn_g, l, 512, 512)
    g, ca, cb, gs, has, hbs = _up_conv(xn2, xn2s, w_up, conv_w, conv_b, state_conv, l, batch, seq, 1024, 512)
    outs = _down_proj(xp, g, xs, gs, w_down, next_gain, l, 256, 512, F32 if last else BF, not last)

    tail = lambda t: t.reshape(batch, SUBLANES, D_FF)[:, SUBLANES - (CONV_W - 1):]
    kvb = kv.reshape(batch, seq, 2 * SWA_KV_WIDTH)[:, seq - WINDOW:]
    st = dict(
        pk=kvb[:, :, :SWA_KV_WIDTH].reshape(batch, WINDOW, SWA_KV, SWA_HD),
        pv=kvb[:, :, SWA_KV_WIDTH:].reshape(batch, WINDOW, SWA_KV, SWA_HD),
        mk=mkv[:, :MEM_WIDTH].reshape(batch, MEM_LEN, MEM_HEADS, MEM_HD),
        mv=mkv[:, MEM_WIDTH:].reshape(batch, MEM_LEN, MEM_HEADS, MEM_HD),
        gvp=vg_last.reshape(batch, CHUNK, G_WIDTH),
        gvs=vgs.reshape(dec, 1, G_WIDTH),
        cvp=jnp.concatenate([tail(ca), tail(cb)], axis=-1),
        cvs=(has, hbs),
        knc=knc,
        vnc=vnc,
        nk=nk,
        nv=nv,
    )
    return outs, st


def kernel(x_prompt, x_sample, cache_swa_k, cache_swa_v, cache_mem_k, cache_mem_v, state_conv, mem_prompt, norm_mix_g, w_in, gmlp_norm_g, gmlp_ws, gmlp_bs, attn_sinks, mem_norm_g, w_mem_kv, w_br_g, w_br_a, w_br_m, w_out, norm_ffn_g, w_up, conv_w, conv_b, w_down, final_norm_g):
    batch, seq, _ = x_prompt.shape
    dec = x_sample.shape[0]
    assert x_sample.shape[1] == 1 and PAST_LEN % CHUNK == 0 and PAST_LEN >= WINDOW
    assert seq % 2048 == 0 and seq >= WINDOW and (seq - 1) // CHUNK * CHUNK == seq - CHUNK
    xp = x_prompt.reshape(batch * seq, D_MODEL)
    xs = x_sample.reshape(dec, D_MODEL)
    mem = mem_prompt.reshape(batch * MEM_LEN, D_MODEL)
    kt = cache_swa_k.transpose(0, 1, 3, 4, 2)
    vt = cache_swa_v.transpose(0, 1, 3, 4, 2)
    mk4 = cache_mem_k.reshape(DEPTH, dec, MEM_LEN * MEM_HEADS, MEM_HD)
    mv4 = cache_mem_v.reshape(DEPTH, dec, MEM_LEN * MEM_HEADS, MEM_HD)
    slope_b = jnp.broadcast_to(jnp.asarray(SLOPES, F32)[:, None], (SWA_HEADS, LANES))
    sink_b = jnp.broadcast_to(attn_sinks[:, :, None], (DEPTH, SWA_HEADS, LANES))
    sts = []
    xn = xns = None
    for l in range(DEPTH):
        last = l == DEPTH - 1
        next_gain = (final_norm_g if last else norm_mix_g[l + 1]).reshape(1, D_MODEL)
        outs, st = _layer(l, last, next_gain, xp, xs, xn, xns, mem, batch, seq, kt, vt, mk4, mv4, state_conv,
                          slope_b, sink_b, [t["knc"] for t in sts], [t["vnc"] for t in sts],
                          norm_mix_g, w_in, gmlp_norm_g, gmlp_ws, gmlp_bs, attn_sinks, mem_norm_g,
                          w_mem_kv, w_br_g, w_br_a, w_br_m, w_out, norm_ffn_g, w_up, conv_w, conv_b, w_down)
        sts.append(st)
        if last:
            y_prompt, y_sample = outs
        else:
            xp, xn, xs, xns = outs
    stack = lambda key: jnp.stack([s[key] for s in sts])
    nk, nv = sts[-1]["nk"], sts[-1]["nv"]
    return (y_prompt.reshape(batch, seq, D_MODEL), y_sample.reshape(dec, 1, D_MODEL), stack("pk"), stack("pv"),
            nk.transpose(0, 1, 4, 2, 3), nv.transpose(0, 1, 4, 2, 3),
            stack("mk"), stack("mv"), stack("gvp"), stack("gvs"), stack("cvp"),
            _conv_state(state_conv, [h for s in sts for h in s["cvs"]], D_FF // 2))
```

```python
import functools
import math

import jax
import jax.numpy as jnp
from jax import lax
from jax.experimental import pallas as pl
from jax.experimental.pallas import tpu as pltpu

D_MODEL = 2048
DEPTH = 2
PAST_LEN = 8192
MEM_LEN = 256
CHUNK = 128
G_GROUPS = 8
G_WIDTH = 768
G_GDIM = G_WIDTH // G_GROUPS
SWA_HEADS = 12
SWA_KV = 4
SWA_GROUP = SWA_HEADS // SWA_KV
SWA_HD = 64
SWA_WIDTH = SWA_HEADS * SWA_HD
SWA_KV_WIDTH = SWA_KV * SWA_HD
WINDOW = 128
MEM_HEADS = 4
MEM_HD = 128
MEM_WIDTH = MEM_HEADS * MEM_HD
N_BRANCH = 3
D_FF = 5632
CONV_W = 3
EPS = 1e-6

LANES = 128
SUBLANES = 8
VMEM_LIMIT = 56 * 1024 * 1024

C_U, C_V, C_Q, C_K, C_VV, C_M, C_G = 0, 768, 1536, 2304, 2560, 2816, 3328
ZT = 256
N_ACT = C_Q // ZT
assert all(c % ZT == 0 for c in (C_V, C_Q, C_K, C_VV, C_M, C_G))

BF = jnp.bfloat16
F32 = jnp.float32
NT_DIMS = (((1,), (1,)), ((), ()))


def _alibi_slopes(n):
    p = 2 ** int(math.floor(math.log2(n)))
    base = [2.0 ** (-8.0 * (i + 1) / p) for i in range(p)]
    extra = [2.0 ** (-8.0 * (2 * i + 1) / (2 * p)) for i in range(n - p)]
    return base + extra


SLOPES = _alibi_slopes(SWA_HEADS)


def _params(*sem):
    return pltpu.CompilerParams(dimension_semantics=sem, vmem_limit_bytes=VMEM_LIMIT)


def _resident(shape):
    return pl.BlockSpec(shape, lambda *_: (0,) * len(shape), pipeline_mode=pl.Buffered(1))


def _rms(x, g):
    ms = jnp.mean(x * x, axis=-1, keepdims=True)
    return x * lax.rsqrt(ms + EPS) * g


def _dot(a, b):
    return jnp.dot(a, b, preferred_element_type=F32)


def _first_inner():
    return pl.program_id(1) == 0


def _last_inner():
    return pl.program_id(1) == pl.num_programs(1) - 1


NZ = C_G // ZT


QKV_TILES = range(C_Q // ZT, C_M // ZT)
QKV_W = C_M - C_Q


def _swa_bias_table(bias_ref):
    r = lax.broadcasted_iota(jnp.int32, (WINDOW, 2 * WINDOW), 0)
    c = lax.broadcasted_iota(jnp.int32, (WINDOW, 2 * WINDOW), 1)
    dist = r + WINDOW - c
    valid = (dist >= 0) & (dist <= WINDOW)
    distf = dist.astype(F32)
    for h in range(SWA_HEADS):
        j, g = divmod(h, SWA_GROUP)
        pen = -SLOPES[h] * distf
        bias_ref[j, g * WINDOW:(g + 1) * WINDOW, :] = jnp.where(valid, pen, -jnp.inf)
        bias_ref[SWA_KV + j, g * WINDOW:(g + 1) * WINDOW, :] = jnp.where(valid & (c >= WINDOW), pen, -jnp.inf)


def _front_kernel(sink_ref, xn_ref, xs_ref, ng_ref, w_ref, ws_ref, gg_ref, bsb_ref, mkv_ref,
                  kv_ref, oa_ref, og_ref, vg_ref, om_ref, zs_ref, *rest, l, tm, tiles_per_seq, raw, group=4):
    xno_ref, xnso_ref = rest[:2] if raw else (None, None)
    wbf_ref, qkv_ref, zt_ref, carry_ref, bias_ref = rest[-5:]
    s = pl.program_id(0)

    def normed(x_ref, out_ref):
        if not raw:
            return x_ref[...]
        x = _rms(x_ref[...], ng_ref[l:l + 1, :]).astype(BF)
        out_ref[...] = x
        return x

    rows = SWA_GROUP * WINDOW
    kcol, vcol = C_K - C_Q, C_VV - C_Q
    mcol = C_Q

    @pl.when(s < NZ)
    def _():
        wbf_ref[s] = w_ref[...].astype(BF)

    @pl.when(s == NZ)
    def _():
        _swa_bias_table(bias_ref)
        carry_ref[...] = jnp.zeros(carry_ref.shape, F32)

    def column(x, c):
        h = _dot(x, wbf_ref[c])
        return jax.nn.gelu(h) if c < N_ACT else h

    @pl.when(s >= NZ)
    def _():
        x = normed(xn_ref, xno_ref)
        for c in QKV_TILES:
            qkv_ref[:, c * ZT - C_Q:(c + 1) * ZT - C_Q] = column(x, c)
        kv_ref[...] = qkv_ref[:, kcol:]

        def project(c, col):
            def task():
                zt_ref[:, col:col + ZT] = column(x, c)
            return task

        tasks = [project(c, c * ZT) for c in range(N_ACT)]
        tasks += [project(c, mcol + (c - C_M // ZT) * ZT) for c in range(C_M // ZT, NZ)]

        r = lax.broadcasted_iota(jnp.int32, (CHUNK, CHUNK), 0)
        cc = lax.broadcasted_iota(jnp.int32, (CHUNK, CHUNK), 1)
        lane = lax.broadcasted_iota(jnp.int32, (CHUNK, LANES), 1)

        def gmlp_chunk(n):
            def task():
                wtri = [jnp.where(r >= cc, ws_ref[g], 0.0).astype(BF) for g in range(G_GROUPS)]
                crow = slice(n * CHUNK, (n + 1) * CHUNK)
                vg = _rms(zt_ref[crow, C_V:C_Q], gg_ref[l:l + 1, :])
                if n == tm // CHUNK - 1:
                    vg_ref[...] = vg
                for t in range(G_TILES):
                    cols = slice(t * LANES, (t + 1) * LANES)
                    wst = jnp.concatenate([wtri[GA[t]], wtri[GA[t] + 1]], axis=0)
                    both = _dot(wst, vg[:, cols].astype(BF))
                    mix = jnp.where(lane < BND[t], both[:CHUNK], both[CHUNK:]) + bsb_ref[:, cols]
                    og_ref[crow, cols] = (zt_ref[crow, cols] * mix).astype(BF)
            return task

        tasks += [gmlp_chunk(n) for n in range(tm // CHUNK)]

        def mem_head(h):
            def task():
                cols = slice(h * MEM_HD, (h + 1) * MEM_HD)
                q = zt_ref[:, mcol + h * MEM_HD:mcol + (h + 1) * MEM_HD].astype(BF)
                sc = lax.dot_general(q, mkv_ref[:, cols].astype(BF), NT_DIMS,
                                     preferred_element_type=F32) * (MEM_HD ** -0.5)
                e = jnp.exp(sc - jnp.max(sc, axis=-1, keepdims=True))
                p = (e / jnp.sum(e, axis=-1, keepdims=True)).astype(BF)
                vcols = slice(MEM_WIDTH + h * MEM_HD, MEM_WIDTH + (h + 1) * MEM_HD)
                om_ref[:, cols] = _dot(p, mkv_ref[:, vcols].astype(BF)).astype(BF)
            return task

        tasks += [mem_head(h) for h in range(MEM_HEADS)]

        units = [(b, j) for b in range(tm // WINDOW) for j in range(SWA_KV)]
        groups = [units[k:k + group] for k in range(0, len(units), group)]
        per = len(tasks) // len(groups)
        assert per * len(groups) == len(tasks) and per % 2 == 0
        head = lax.broadcasted_iota(jnp.int32, (rows, 1), 0) // WINDOW
        fresh = ((s - NZ) % tiles_per_seq == 0).astype(jnp.int32) * SWA_KV

        def scores(b, j):
            own = slice(b * WINDOW, (b + 1) * WINDOW)
            kvh = lambda base: slice(base + j * SWA_HD, base + (j + 1) * SWA_HD)
            if b == 0:
                kprev, vprev = carry_ref[:, kvh(0)], carry_ref[:, kvh(SWA_KV_WIDTH)]
            else:
                prev = slice((b - 1) * WINDOW, b * WINDOW)
                kprev, vprev = qkv_ref[prev, kvh(kcol)], qkv_ref[prev, kvh(vcol)]
            kj = jnp.concatenate([kprev, qkv_ref[own, kvh(kcol)]], axis=0).astype(BF)
            vj = jnp.concatenate([vprev, qkv_ref[own, kvh(vcol)]], axis=0).astype(BF)
            hs = range(j * SWA_GROUP, (j + 1) * SWA_GROUP)
            q = jnp.concatenate([qkv_ref[own, h * SWA_HD:(h + 1) * SWA_HD] for h in hs], axis=0).astype(BF)
            sink = jnp.full((rows, 1), sink_ref[l, hs[-1]], F32)
            for g in range(SWA_GROUP - 1):
                sink = jnp.where(head == g, sink_ref[l, hs[g]], sink)
            bias = bias_ref[(fresh if b == 0 else 0) + j]
            sc = lax.dot_general(q, kj, NT_DIMS, preferred_element_type=F32) * (SWA_HD ** -0.5) + bias
            return sc, sink, vj

        def softmax(sc, sink, vj):
            mx = jnp.maximum(jnp.max(sc, axis=-1, keepdims=True), sink)
            p = jnp.exp(sc - mx)
            den = jnp.sum(p, axis=-1, keepdims=True) + jnp.exp(sink - mx)
            return p.astype(BF), den, vj

        def values(b, j, p, den, vj):
            o = _dot(p, vj) / den
            own = slice(b * WINDOW, (b + 1) * WINDOW)
            for g, h in enumerate(range(j * SWA_GROUP, (j + 1) * SWA_GROUP)):
                oa_ref[own, h * SWA_HD:(h + 1) * SWA_HD] = o[g * WINDOW:(g + 1) * WINDOW].astype(BF)

        for k, grp in enumerate(groups):
            for task in tasks[per * k:per * k + per // 2]:
                task()
            stage = [scores(b, j) for b, j in grp]
            stage = [softmax(*args) for args in stage]
            for task in tasks[per * k + per // 2:per * (k + 1)]:
                task()
            for (b, j), args in zip(grp, stage):
                values(b, j, *args)
        last = slice(tm - WINDOW, tm)
        carry_ref[...] = qkv_ref[last, kcol:]

    @pl.when(s == pl.num_programs(0) - 1)
    def _():
        xs = normed(xs_ref, xnso_ref)
        for c in range(NZ):
            zs_ref[:, c * ZT:(c + 1) * ZT] = column(xs, c)


def _front(sinks, xn, xs, ng, w_in, ws, gg, bsb, mkv, l, seq, tm):
    m, ms = xn.shape[0], xs.shape[0]
    raw = xn.dtype == F32
    tiles_per_seq = seq // tm
    tile = lambda s: jnp.maximum(s - NZ, 0)
    rows = lambda n: pl.BlockSpec((tm, n), lambda s: (tile(s), 0))
    fixed = lambda n: pl.BlockSpec((ms, n), lambda s: (0, 0))
    per_seq = lambda r, n: pl.BlockSpec((r, n), lambda s: (tile(s) // tiles_per_seq, 0))
    mkv_spec = pl.BlockSpec((None, MEM_LEN, 2 * MEM_WIDTH), lambda s: (l, tile(s) // tiles_per_seq, 0))
    outs = [(2 * SWA_KV_WIDTH, F32), (SWA_WIDTH, BF), (G_WIDTH, BF)]
    return pl.pallas_call(
        functools.partial(_front_kernel, l=l, tm=tm, tiles_per_seq=tiles_per_seq, raw=raw),
        grid=(NZ + m // tm,),
        in_specs=[pl.BlockSpec(memory_space=pltpu.SMEM), rows(D_MODEL), _resident((ms, D_MODEL)), _resident(ng.shape),
                  pl.BlockSpec((None, D_MODEL, ZT), lambda s: (l, 0, jnp.minimum(s, NZ - 1))),
                  pl.BlockSpec((None, G_GROUPS, CHUNK, CHUNK), lambda s: (l, 0, 0, 0)),
                  _resident(gg.shape), _resident((CHUNK, G_WIDTH)), mkv_spec],
        out_specs=[rows(n) for n, _ in outs] + [per_seq(CHUNK, G_WIDTH), rows(MEM_WIDTH), fixed(C_G)]
                  + ([rows(D_MODEL), fixed(D_MODEL)] if raw else []),
        out_shape=[jax.ShapeDtypeStruct((m, n), dt) for n, dt in outs]
                  + [jax.ShapeDtypeStruct((m // seq * CHUNK, G_WIDTH), F32), jax.ShapeDtypeStruct((m, MEM_WIDTH), BF),
                     jax.ShapeDtypeStruct((ms, C_G), F32)]
                  + ([jax.ShapeDtypeStruct((m, D_MODEL), BF), jax.ShapeDtypeStruct((ms, D_MODEL), BF)] if raw else []),
        scratch_shapes=[pltpu.VMEM((NZ, D_MODEL, ZT), BF), pltpu.VMEM((tm, QKV_W), F32),
                        pltpu.VMEM((tm, C_Q + MEM_WIDTH), F32), pltpu.VMEM((WINDOW, 2 * SWA_KV_WIDTH), F32),
                        pltpu.VMEM((2 * SWA_KV, SWA_GROUP * WINDOW, 2 * WINDOW), F32)],
        compiler_params=_params("arbitrary"),
        name="front",
    )(sinks, xn, xs, ng, w_in, ws, gg, bsb, mkv)


G_TILES = G_WIDTH // LANES
GA = [(t * LANES) // G_GDIM for t in range(G_TILES)]
BND = [(GA[t] + 1) * G_GDIM - t * LANES for t in range(G_TILES)]
assert all(0 < b < LANES and (GA[t] + 2) * G_GDIM >= (t + 1) * LANES for t, b in enumerate(BND))


def _gmlp_sample_kernel(zs_ref, gg_ref, coef_ref, og_ref, vg_ref, *, l):
    vg = _rms(zs_ref[:, C_V:C_Q], gg_ref[l:l + 1, :])
    vg_ref[...] = vg
    og_ref[...] = (zs_ref[:, C_U:C_V] * (coef_ref[0:1, :] * vg + coef_ref[1:2, :])).astype(BF)


def _gmlp_sample(zs, gg, coef, l):
    m = zs.shape[0]
    return pl.pallas_call(
        functools.partial(_gmlp_sample_kernel, l=l),
        out_shape=[jax.ShapeDtypeStruct((m, G_WIDTH), BF), jax.ShapeDtypeStruct((m, G_WIDTH), F32)],
        name="gmlp_sample",
    )(zs, gg, coef)


def _swa_sample_kernel(q_ref, kt_ref, vt_ref, kn_ref, vn_ref, slope_ref, sink_ref, o_ref, *, nreq):
    kvw = SWA_KV_WIDTH
    hrow = lax.broadcasted_iota(jnp.int32, (SWA_HEADS, kvw), 0) // SWA_GROUP
    lblk = lax.broadcasted_iota(jnp.int32, (SWA_HEADS, kvw), 1) // SWA_HD
    own = hrow == lblk
    kvh = lax.broadcasted_iota(jnp.int32, (SWA_HEADS, SWA_HD), 0) // SWA_GROUP
    c = lax.broadcasted_iota(jnp.int32, (SWA_HEADS, WINDOW), 1)
    bias = slope_ref[...] * (WINDOW - c).astype(F32)
    sink = sink_ref[:, 0:1]
    scale = SWA_HD ** -0.5
    q = q_ref[...]
    qm = jnp.where(own, jnp.concatenate([q] * SWA_KV, axis=2), 0.0).astype(BF)
    kt = kt_ref[...].reshape(nreq, kvw, WINDOW)
    vt = vt_ref[...].reshape(nreq, kvw, WINDOW)
    rows = lambda ref: jnp.stack([ref[r:r + 1, :] for r in range(nreq)]).astype(BF).astype(F32)
    kn, vn = rows(kn_ref), rows(vn_ref)
    s_c = jnp.einsum("rhk,rkc->rhc", qm, kt.astype(BF), preferred_element_type=F32) * scale - bias
    s_n = jnp.sum(qm.astype(F32) * kn, axis=-1, keepdims=True) * scale
    mx = jnp.maximum(jnp.maximum(jnp.max(s_c, axis=-1, keepdims=True), s_n), sink)
    p_c = jnp.exp(s_c - mx)
    p_n = jnp.exp(s_n - mx)
    den = jnp.sum(p_c, axis=-1, keepdims=True) + p_n + jnp.exp(sink - mx)
    p_c = (p_c / den).astype(BF)
    p_n = (p_n / den).astype(BF).astype(F32)
    o_all = jnp.einsum("rhc,rkc->rhk", p_c, vt.astype(BF), preferred_element_type=F32) + p_n * vn
    o = jnp.zeros((nreq, SWA_HEADS, SWA_HD), F32)
    for j in range(SWA_KV):
        o = jnp.where(kvh == j, o_all[:, :, j * SWA_HD:(j + 1) * SWA_HD], o)
    o_ref[...] = o


def _swa_sample(q3, kt, vt, zs, slope_b, sink_b, l, nreq):
    m = q3.shape[0]
    cache = pl.BlockSpec((None, nreq, SWA_KV, SWA_HD, WINDOW), lambda i: (l, i, 0, 0, 0))
    return pl.pallas_call(
        functools.partial(_swa_sample_kernel, nreq=nreq),
        grid=(m // nreq,),
        in_specs=[pl.BlockSpec((nreq, SWA_HEADS, SWA_HD), lambda i: (i, 0, 0)), cache, cache,
                  pl.BlockSpec((nreq, SWA_KV_WIDTH), lambda i: (i, C_K // SWA_KV_WIDTH)),
                  pl.BlockSpec((nreq, SWA_KV_WIDTH), lambda i: (i, C_VV // SWA_KV_WIDTH)),
                  _resident((SWA_HEADS, LANES)), pl.BlockSpec((None, SWA_HEADS, LANES), lambda i: (l, 0, 0))],
        out_specs=pl.BlockSpec((nreq, SWA_HEADS, SWA_HD), lambda i: (i, 0, 0)),
        out_shape=jax.ShapeDtypeStruct((m, SWA_HEADS, SWA_HD), F32),
        compiler_params=_params("arbitrary"),
        name="swa_sample",
    )(q3, kt, vt, zs, zs, slope_b, sink_b)


def _shift_windows(kt_ref, vt_ref, knc_ref, vnc_ref, nk_ref, nv_ref, nreq):
    last = lax.broadcasted_iota(jnp.int32, (SWA_KV_WIDTH, WINDOW), 1) == WINDOW - 1
    for src_ref, col_ref, dst_ref in ((kt_ref, knc_ref, nk_ref), (vt_ref, vnc_ref, nv_ref)):
        for l in range(DEPTH):
            for r in range(nreq):
                old = src_ref[l, r].reshape(SWA_KV_WIDTH, WINDOW)
                new = jnp.where(last, col_ref[l, :, r:r + 1], pltpu.roll(old, WINDOW - 1, axis=1))
                dst_ref[l, r] = new.reshape(SWA_KV, SWA_HD, WINDOW)


def _memkv_kernel(x_ref, g_ref, w_ref, o_ref, xn_ref):
    @pl.when(pl.program_id(1) == 0)
    def _():
        xn_ref[...] = _rms(x_ref[...], g_ref[...]).astype(BF)

    o_ref[...] = _dot(xn_ref[...], w_ref[...].astype(BF))


def _memkv(mem, g, w, tn):
    m = mem.shape[0]
    return pl.pallas_call(
        _memkv_kernel,
        grid=(DEPTH, 2 * MEM_WIDTH // tn),
        in_specs=[_resident((m, D_MODEL)), pl.BlockSpec((None, 1, D_MODEL), lambda l, j: (l, 0, 0)),
                  pl.BlockSpec((None, D_MODEL, tn), lambda l, j: (l, 0, j))],
        out_specs=pl.BlockSpec((None, m, tn), lambda l, j: (l, 0, j)),
        out_shape=jax.ShapeDtypeStruct((DEPTH, m, 2 * MEM_WIDTH), F32),
        scratch_shapes=[pltpu.VMEM((m, D_MODEL), BF)],
        compiler_params=_params("arbitrary", "arbitrary"),
        name="mem_kv",
    )(mem, g.reshape(DEPTH, 1, D_MODEL), w)


def _mem_sample_kernel(q_ref, mk_ref, mv_ref, *rest, nreq):
    o_ref = rest[0] if len(rest) == 1 else rest[4]
    nrow = MEM_LEN * MEM_HEADS
    own = (lax.broadcasted_iota(jnp.int32, (nreq, MEM_HEADS, nrow), 2) % MEM_HEADS
           == lax.broadcasted_iota(jnp.int32, (nreq, MEM_HEADS, nrow), 1))
    s = jnp.einsum("rhd,rkd->rhk", q_ref[...].astype(BF), mk_ref[...].astype(BF),
                   preferred_element_type=F32) * (MEM_HD ** -0.5)
    s = jnp.where(own, s, -jnp.inf)
    e = jnp.exp(s - jnp.max(s, axis=-1, keepdims=True))
    p = (e / jnp.sum(e, axis=-1, keepdims=True)).astype(BF)
    o_ref[...] = jnp.einsum("rhk,rkd->rhd", p, mv_ref[...].astype(BF), preferred_element_type=F32)
    if len(rest) > 1:
        _shift_windows(*rest[:4], *rest[5:], nreq)


def _mem_sample(q3, mk, mv, l, nreq, window=None):
    m = q3.shape[0]
    qspec = pl.BlockSpec((nreq, MEM_HEADS, MEM_HD), lambda i: (i, 0, 0))
    kvspec = pl.BlockSpec((None, nreq, MEM_LEN * MEM_HEADS, MEM_HD), lambda i: (l, i, 0, 0))
    in_specs, out_specs = [qspec, kvspec, kvspec], [qspec]
    out_shape = [jax.ShapeDtypeStruct((m, MEM_HEADS, MEM_HD), F32)]
    args = [q3, mk, mv]
    if window is not None:
        kt, vt = window[:2]
        cache = pl.BlockSpec((DEPTH, nreq, SWA_KV, SWA_HD, WINDOW), lambda i: (0, i, 0, 0, 0))
        newcol = pl.BlockSpec((DEPTH, None, SWA_KV_WIDTH, nreq), lambda i: (0, i, 0, 0))
        in_specs += [cache, cache, newcol, newcol]
        out_specs += [cache, cache]
        out_shape += [jax.ShapeDtypeStruct(kt.shape, F32), jax.ShapeDtypeStruct(vt.shape, F32)]
        args += list(window)
    return pl.pallas_call(
        functools.partial(_mem_sample_kernel, nreq=nreq),
        grid=(m // nreq,),
        in_specs=in_specs,
        out_specs=out_specs,
        out_shape=out_shape,
        compiler_params=_params("arbitrary"),
        name="mem_attn_sample",
    )(*args)


def _merge_kernel(xn_ref, og_ref, oa_ref, om_ref, xns_ref, ogs_ref, oas_ref, oms_ref,
                  wg0_ref, wg1_ref, wg2_ref, wbg_ref, wba_ref, wbm_ref, out_ref, outs_ref,
                  bg0_ref, bg1_ref, bg2_ref, bbg_ref, bba_ref, bbm_ref):
    pairs = [(wg0_ref, bg0_ref), (wg1_ref, bg1_ref), (wg2_ref, bg2_ref),
             (wbg_ref, bbg_ref), (wba_ref, bba_ref), (wbm_ref, bbm_ref)]

    @pl.when(_first_inner())
    def _():
        for w_ref, b_ref in pairs:
            b_ref[...] = w_ref[...].astype(BF)

    def merged(xn, og, oa, om):
        def branch(bg_ref, o, bb_ref):
            return jax.nn.sigmoid(_dot(xn, bg_ref[...])) * _dot(o, bb_ref[...])
        return (branch(bg0_ref, og, bbg_ref) + branch(bg1_ref, oa, bba_ref) + branch(bg2_ref, om, bbm_ref)).astype(BF)

    out_ref[...] = merged(xn_ref[...], og_ref[...], oa_ref[...], om_ref[...])

    @pl.when(_last_inner())
    def _():
        outs_ref[...] = merged(xns_ref[...], ogs_ref[...], oas_ref[...], oms_ref[...])


def _merge(xn, og, oa, om, xns, ogs, oas, oms, w_in, wbg, wba, wbm, l, tm, tn):
    m, ms = xn.shape[0], xns.shape[0]
    nt = D_MODEL // tn
    row = lambda n: pl.BlockSpec((tm, n), lambda j, i: (i, 0))
    gate = lambda b: pl.BlockSpec((None, D_MODEL, tn), lambda j, i: (l, 0, C_G // tn + b * nt + j))
    col = lambda n: pl.BlockSpec((None, n, tn), lambda j, i: (l, 0, j))
    widths = (G_WIDTH, SWA_WIDTH, MEM_WIDTH)
    return pl.pallas_call(
        _merge_kernel,
        grid=(nt, m // tm),
        in_specs=[row(D_MODEL)] + [row(n) for n in widths]
                 + [_resident((ms, D_MODEL))] + [_resident((ms, n)) for n in widths]
                 + [gate(0), gate(1), gate(2)] + [col(n) for n in widths],
        out_specs=[pl.BlockSpec((tm, tn), lambda j, i: (i, j)), pl.BlockSpec((ms, tn), lambda j, i: (0, j))],
        out_shape=[jax.ShapeDtypeStruct((m, D_MODEL), BF), jax.ShapeDtypeStruct((ms, D_MODEL), BF)],
        scratch_shapes=[pltpu.VMEM((D_MODEL, tn), BF)] * 3 + [pltpu.VMEM((n, tn), BF) for n in widths],
        compiler_params=_params("arbitrary", "arbitrary"),
        name="merge",
    )(xn, og, oa, om, xns, ogs, oas, oms, w_in, w_in, w_in, wbg, wba, wbm)


def _out_kernel(x_ref, mg_ref, xs_ref, mgs_ref, w_ref, g_ref, xo_ref, xn_ref, xos_ref, xns_ref, wbf_ref,
                *, l, tk, ncast):
    s = pl.program_id(0)

    @pl.when(s < ncast)
    def _():
        wbf_ref[pl.ds(pl.multiple_of(s * tk, tk), tk), :] = w_ref[...].astype(BF)

    def tile(x_ref, mg_ref, xo_ref, xn_ref):
        x = x_ref[...] + _dot(mg_ref[...], wbf_ref[...])
        xo_ref[...] = x
        xn_ref[...] = _rms(x, g_ref[l:l + 1, :]).astype(BF)

    @pl.when(s >= ncast)
    def _():
        tile(x_ref, mg_ref, xo_ref, xn_ref)

    @pl.when(s == pl.num_programs(0) - 1)
    def _():
        tile(xs_ref, mgs_ref, xos_ref, xns_ref)


def _out_proj(x, merged, xs, mergeds, w_out, g, l, tm, tk):
    m, ms = x.shape[0], xs.shape[0]
    ncast = D_MODEL // tk
    row = pl.BlockSpec((tm, D_MODEL), lambda s: (jnp.maximum(s - ncast, 0), 0))
    srow = pl.BlockSpec((ms, D_MODEL), lambda s: (0, 0))
    return pl.pallas_call(
        functools.partial(_out_kernel, l=l, tk=tk, ncast=ncast),
        grid=(ncast + m // tm,),
        in_specs=[row, row, _resident((ms, D_MODEL)), _resident((ms, D_MODEL)),
                  pl.BlockSpec((None, tk, D_MODEL), lambda s: (l, jnp.minimum(s, ncast - 1), 0)),
                  _resident(g.shape)],
        out_specs=[row, row, srow, srow],
        out_shape=[jax.ShapeDtypeStruct((m, D_MODEL), F32), jax.ShapeDtypeStruct((m, D_MODEL), BF),
                   jax.ShapeDtypeStruct((ms, D_MODEL), F32), jax.ShapeDtypeStruct((ms, D_MODEL), BF)],
        scratch_shapes=[pltpu.VMEM((D_MODEL, D_MODEL), BF)],
        compiler_params=_params("arbitrary"),
        name="out_proj",
    )(x, merged, xs, mergeds, w_out, g)


def _up_kernel(xn_ref, xns_ref, wa_ref, wb_ref, cwa_ref, cwb_ref, cba_ref, cbb_ref, sta_ref, stb_ref,
               g_ref, ca_ref, cb_ref, gs_ref, has_ref, hbs_ref, ha_ref, hb_ref, ba_ref, bb_ref,
               *, l, tm, tiles_per_seq):
    @pl.when(_first_inner())
    def _():
        ba_ref[...] = wa_ref[...].astype(BF)
        bb_ref[...] = wb_ref[...].astype(BF)

    @pl.when(pl.program_id(1) % tiles_per_seq == 0)
    def _():
        ha_ref[...] = jnp.zeros(ha_ref.shape, F32)
        hb_ref[...] = jnp.zeros(hb_ref.shape, F32)

    top = lax.broadcasted_iota(jnp.int32, (SUBLANES, ha_ref.shape[1]), 0)
    xn = xn_ref[...]

    def conv(h, prev_ref, cw_ref, bias_ref, tail_ref):
        prev = prev_ref[...]
        acc = bias_ref[l:l + 1, :]
        for j in range(CONV_W - 1):
            back = CONV_W - 1 - j
            rolled = pltpu.roll(h, back, axis=0)
            head = jnp.where(top < back, pltpu.roll(prev, back, axis=0), rolled[0:SUBLANES])
            acc = acc + jnp.concatenate([head, rolled[SUBLANES:]], axis=0) * cw_ref[j:j + 1, :]
        tail = h[tm - SUBLANES:]
        tail_ref[...] = tail
        prev_ref[...] = tail
        return acc + h * cw_ref[CONV_W - 1:CONV_W, :]

    a = conv(_dot(xn, ba_ref[...]), ha_ref, cwa_ref, cba_ref, ca_ref)
    b = conv(_dot(xn, bb_ref[...]), hb_ref, cwb_ref, cbb_ref, cb_ref)
    g_ref[...] = (jax.nn.gelu(a) * b).astype(BF)

    @pl.when(_last_inner())
    def _():
        xns = xns_ref[...]

        def conv_s(bw_ref, cw_ref, bias_ref, st_ref, h_out_ref):
            h = _dot(xns, bw_ref[...])
            h_out_ref[...] = h
            return (bias_ref[l:l + 1, :] + st_ref[:, 0, :] * cw_ref[0:1, :] + st_ref[:, 1, :] * cw_ref[1:2, :]
                    + h * cw_ref[2:3, :])

        a_s = conv_s(ba_ref, cwa_ref, cba_ref, sta_ref, has_ref)
        b_s = conv_s(bb_ref, cwb_ref, cbb_ref, stb_ref, hbs_ref)
        gs_ref[...] = (jax.nn.gelu(a_s) * b_s).astype(BF)


def _up_conv(xn, xns, w_up, conv_w, conv_b, state, l, batch, seq, tm, tn):
    m, ms = xn.shape[0], xns.shape[0]
    nt = D_FF // tn
    nrow = m // tm
    tiles_per_seq = seq // tm
    half = lambda off: (lambda j, i: (l, 0, off * nt + j))
    wspec = lambda off: pl.BlockSpec((None, D_MODEL, tn), half(off))
    cwspec = lambda off: pl.BlockSpec((None, CONV_W, tn), half(off))
    cbspec = lambda off: pl.BlockSpec((DEPTH, tn), lambda j, i: (0, off * nt + j))
    stspec = lambda off: pl.BlockSpec((None, ms, CONV_W - 1, tn), lambda j, i: (l, 0, 0, off * nt + j))
    tail = pl.BlockSpec((SUBLANES, tn), lambda j, i: (i // tiles_per_seq, j))
    scol = pl.BlockSpec((ms, tn), lambda j, i: (0, j))
    return pl.pallas_call(
        functools.partial(_up_kernel, l=l, tm=tm, tiles_per_seq=tiles_per_seq),
        grid=(nt, nrow),
        in_specs=[pl.BlockSpec((tm, D_MODEL), lambda j, i: (i, 0)), _resident((ms, D_MODEL)),
                  wspec(0), wspec(1), cwspec(0), cwspec(1), cbspec(0), cbspec(1), stspec(0), stspec(1)],
        out_specs=[pl.BlockSpec((tm, tn), lambda j, i: (i, j)), tail, tail, scol, scol, scol],
        out_shape=[jax.ShapeDtypeStruct((m, D_FF), BF),
                   jax.ShapeDtypeStruct((batch * SUBLANES, D_FF), F32),
                   jax.ShapeDtypeStruct((batch * SUBLANES, D_FF), F32),
                   jax.ShapeDtypeStruct((ms, D_FF), BF),
                   jax.ShapeDtypeStruct((ms, D_FF), F32), jax.ShapeDtypeStruct((ms, D_FF), F32)],
        scratch_shapes=[pltpu.VMEM((SUBLANES, tn), F32)] * 2 + [pltpu.VMEM((D_MODEL, tn), BF)] * 2,
        compiler_params=_params("arbitrary", "arbitrary"),
        name="up_conv_glu",
    )(xn, xns, w_up, w_up, conv_w, conv_w, conv_b, conv_b, state, state)


def _down_kernel(x_ref, g_ref, xs_ref, gs_ref, w_ref, gain_ref, *rest, tk, ncast, emit_x):
    wbf_ref = rest[-1]
    outs = rest[:-1]
    s = pl.program_id(0)

    @pl.when(s < ncast)
    def _():
        wbf_ref[pl.ds(pl.multiple_of(s * tk, tk), tk), :] = w_ref[...].astype(BF)

    def tile(x, g, o_refs):
        x = x + _dot(g, wbf_ref[...])
        if emit_x:
            o_refs[0][...] = x
        o_refs[-1][...] = _rms(x, gain_ref[...]).astype(o_refs[-1].dtype)

    nout = len(outs) // 2

    @pl.when(s >= ncast)
    def _():
        tile(x_ref[...], g_ref[...], outs[:nout])

    @pl.when(s == pl.num_programs(0) - 1)
    def _():
        tile(xs_ref[...], gs_ref[...], outs[nout:])


def _down_proj(x, g, xs, gs, w_down, gain, l, tm, tk, norm_dtype, emit_x):
    m, ms = x.shape[0], xs.shape[0]
    ncast = D_FF // tk
    rows = lambda n: pl.BlockSpec((tm, n), lambda s: (jnp.maximum(s - ncast, 0), 0))
    fixed = lambda n: pl.BlockSpec((ms, n), lambda s: (0, 0))
    dts = ([F32] if emit_x else []) + [norm_dtype]
    return pl.pallas_call(
        functools.partial(_down_kernel, tk=tk, ncast=ncast, emit_x=emit_x),
        grid=(ncast + m // tm,),
        in_specs=[rows(D_MODEL), rows(D_FF), _resident((ms, D_MODEL)), _resident((ms, D_FF)),
                  pl.BlockSpec((None, tk, D_MODEL), lambda s: (l, jnp.minimum(s, ncast - 1), 0)),
                  _resident((1, D_MODEL))],
        out_specs=[rows(D_MODEL) for _ in dts] + [fixed(D_MODEL) for _ in dts],
        out_shape=[jax.ShapeDtypeStruct((m, D_MODEL), dt) for dt in dts]
                  + [jax.ShapeDtypeStruct((ms, D_MODEL), dt) for dt in dts],
        scratch_shapes=[pltpu.VMEM((D_FF, D_MODEL), BF)],
        compiler_params=_params("arbitrary"),
        name="down_proj",
    )(x, g, xs, gs, w_down, gain)


def _conv_state_kernel(st_ref, *rest, nt):
    o_ref = rest[-1]
    first_half = pl.program_id(0) < nt
    for l in range(DEPTH):
        ha_ref, hb_ref = rest[2 * l], rest[2 * l + 1]
        o_ref[l, :, 0, :] = st_ref[l, :, CONV_W - 2, :]
        o_ref[l, :, 1, :] = jnp.where(first_half, ha_ref[...], hb_ref[...])


def _conv_state(state, halves, tn):
    depth, ms = state.shape[:2]
    nt = D_FF // tn
    blk = pl.BlockSpec((depth, ms, CONV_W - 1, tn), lambda j: (0, 0, 0, j))
    a_spec = pl.BlockSpec((ms, tn), lambda j: (0, jnp.minimum(j, nt - 1)))
    b_spec = pl.BlockSpec((ms, tn), lambda j: (0, jnp.maximum(j - nt, 0)))
    return pl.pallas_call(
        functools.partial(_conv_state_kernel, nt=nt),
        grid=(2 * nt,),
        in_specs=[blk] + [a_spec, b_spec] * depth,
        out_specs=blk,
        out_shape=jax.ShapeDtypeStruct(state.shape, F32),
        compiler_params=_params("arbitrary"),
        name="conv_state_sample",
    )(state, *halves)


def _layer(l, last, next_gain, xp, xs, xn, xns, mkv, batch, seq, kt, vt, mk4, mv4, state_conv, slope_b, sink_b,
           prev_knc, prev_vnc,
           norm_mix_g, w_in, gmlp_norm_g, gmlp_ws, gmlp_bs, attn_sinks, mem_norm_g, w_mem_kv,
           w_br_g, w_br_a, w_br_m, w_out, norm_ffn_g, w_up, conv_w, conv_b, w_down):
    dec = xs.shape[0]
    nreq = 8
    bs = gmlp_bs[l]
    bsb = jnp.repeat(bs.T, G_GDIM, axis=1)
    front = _front(attn_sinks, xp if xn is None else xn, xs if xn is None else xns, norm_mix_g, w_in,
                   gmlp_ws, gmlp_norm_g, bsb, mkv, l, seq, 512)
    kv, oa, og, vg_last, om, zs = front[:6]
    if xn is None:
        xn, xns = front[6:]

    coef = jnp.stack([jnp.repeat(gmlp_ws[l, :, 0, 0], G_GDIM), jnp.repeat(bs[:, 0], G_GDIM)])
    ogs, vgs = _gmlp_sample(zs, gmlp_norm_g, coef, l)

    q3 = zs[:, C_Q:C_K].reshape(dec, SWA_HEADS, SWA_HD)
    newcol = lambda a, b: zs[:, a:b].reshape(dec // nreq, nreq, SWA_KV_WIDTH).transpose(0, 2, 1)
    oas = _swa_sample(q3, kt, vt, zs, slope_b, sink_b, l, 4 * nreq)

    knc, vnc = newcol(C_K, C_VV), newcol(C_VV, C_M)
    qm3 = zs[:, C_M:C_G].reshape(dec, MEM_HEADS, MEM_HD)
    nk = nv = None
    if last:
        window = (kt, vt, jnp.stack([*prev_knc, knc]), jnp.stack([*prev_vnc, vnc]))
        oms, nk, nv = _mem_sample(qm3, mk4, mv4, l, nreq, window)
    else:
        (oms,) = _mem_sample(qm3, mk4, mv4, l, 2 * nreq)

    merged, mergeds = _merge(xn, og, oa, om, xns, ogs, oas.reshape(dec, SWA_WIDTH).astype(BF),
                             oms.reshape(dec, MEM_WIDTH).astype(BF), w_in, w_br_g, w_br_a, w_br_m, l, 1024, 256)
    xp, xn2, xs, xn2s = _out_proj(xp, merged, xs, mergeds, w_out, norm_ffn_g, l, 512, 512)
    g, ca, cb, gs, has, hbs = _up_conv(xn2, xn2s, w_up, conv_w, conv_b, state_conv, l, batch, seq, 1024, 512)
    outs = _down_proj(xp, g, xs, gs, w_down, next_gain, l, 256, 512, F32 if last else BF, not last)

    tail = lambda t: t.reshape(batch, SUBLANES, D_FF)[:, SUBLANES - (CONV_W - 1):]
    kvb = kv.reshape(batch, seq, 2 * SWA_KV_WIDTH)[:, seq - WINDOW:]
    st = dict(
        pk=kvb[:, :, :SWA_KV_WIDTH].reshape(batch, WINDOW, SWA_KV, SWA_HD),
        pv=kvb[:, :, SWA_KV_WIDTH:].reshape(batch, WINDOW, SWA_KV, SWA_HD),
        gvp=vg_last.reshape(batch, CHUNK, G_WIDTH),
        gvs=vgs.reshape(dec, 1, G_WIDTH),
        cvp=jnp.concatenate([tail(ca), tail(cb)], axis=-1),
        cvs=(has, hbs),
        knc=knc,
        vnc=vnc,
        nk=nk,
        nv=nv,
    )
    return outs, st


def kernel(x_prompt, x_sample, cache_swa_k, cache_swa_v, cache_mem_k, cache_mem_v, state_conv, mem_prompt, norm_mix_g, w_in, gmlp_norm_g, gmlp_ws, gmlp_bs, attn_sinks, mem_norm_g, w_mem_kv, w_br_g, w_br_a, w_br_m, w_out, norm_ffn_g, w_up, conv_w, conv_b, w_down, final_norm_g):
    batch, seq, _ = x_prompt.shape
    dec = x_sample.shape[0]
    assert x_sample.shape[1] == 1 and PAST_LEN % CHUNK == 0 and PAST_LEN >= WINDOW
    assert seq % 2048 == 0 and seq >= WINDOW and (seq - 1) // CHUNK * CHUNK == seq - CHUNK
    xp = x_prompt.reshape(batch * seq, D_MODEL)
    xs = x_sample.reshape(dec, D_MODEL)
    mem = mem_prompt.reshape(batch * MEM_LEN, D_MODEL)
    kt = cache_swa_k.transpose(0, 1, 3, 4, 2)
    vt = cache_swa_v.transpose(0, 1, 3, 4, 2)
    mk4 = cache_mem_k.reshape(DEPTH, dec, MEM_LEN * MEM_HEADS, MEM_HD)
    mv4 = cache_mem_v.reshape(DEPTH, dec, MEM_LEN * MEM_HEADS, MEM_HD)
    slope_b = jnp.broadcast_to(jnp.asarray(SLOPES, F32)[:, None], (SWA_HEADS, LANES))
    sink_b = jnp.broadcast_to(attn_sinks[:, :, None], (DEPTH, SWA_HEADS, LANES))
    sts = []
    mkv = _memkv(mem, mem_norm_g, w_mem_kv, 512)
    xn = xns = None
    for l in range(DEPTH):
        last = l == DEPTH - 1
        next_gain = (final_norm_g if last else norm_mix_g[l + 1]).reshape(1, D_MODEL)
        outs, st = _layer(l, last, next_gain, xp, xs, xn, xns, mkv, batch, seq, kt, vt, mk4, mv4, state_conv,
                          slope_b, sink_b, [t["knc"] for t in sts], [t["vnc"] for t in sts],
                          norm_mix_g, w_in, gmlp_norm_g, gmlp_ws, gmlp_bs, attn_sinks, mem_norm_g,
                          w_mem_kv, w_br_g, w_br_a, w_br_m, w_out, norm_ffn_g, w_up, conv_w, conv_b, w_down)
        sts.append(st)
        if last:
            y_prompt, y_sample = outs
        else:
            xp, xn, xs, xns = outs
    stack = lambda key: jnp.stack([s[key] for s in sts])
    mem_state = lambda c: mkv[:, :, c:c + MEM_WIDTH].reshape(DEPTH, batch, MEM_LEN, MEM_HEADS, MEM_HD)
    nk, nv = sts[-1]["nk"], sts[-1]["nv"]
    return (y_prompt.reshape(batch, seq, D_MODEL), y_sample.reshape(dec, 1, D_MODEL), stack("pk"), stack("pv"),
            nk.transpose(0, 1, 4, 2, 3), nv.transpose(0, 1, 4, 2, 3),
            mem_state(0), mem_state(MEM_WIDTH), stack("gvp"), stack("gvs"), stack("cvp"),
            _conv_state(state_conv, [h for s in sts for h in s["cvs"]], D_FF // 2))
```
